```python
import math
import jax, jax.numpy as jnp
from jax import lax
import numpy as np

D_MODEL = 1024
BATCH = 4
SEQ = 8192
DEPTH = 2

HEAD_DIM = 64
ROPE_THETA = 10000.0
BLOCK = 128
EPS = 1e-6

A_HEADS = 4
A_VDIM = 2 * HEAD_DIM
A_WIDTH = A_HEADS * A_VDIM
A_QK = A_HEADS * 2 * HEAD_DIM
B_HEADS = 8
B_KV_HEADS = 2
B_WIDTH = B_HEADS * HEAD_DIM
B_KV = B_KV_HEADS * HEAD_DIM
WINDOW = 128
C_HEADS = 16
C_WIDTH = C_HEADS * HEAD_DIM

EVEN_SIZES = (A_QK, A_QK, A_WIDTH, A_WIDTH, B_WIDTH, B_KV, B_KV, B_WIDTH)
EVEN_IN = sum(EVEN_SIZES)
EVEN_MIX = A_WIDTH + B_WIDTH
ODD_IN = 4 * C_WIDTH
N_EVEN = (DEPTH + 1) // 2
N_ODD = DEPTH // 2

kernel_name = "hybrid_diffattn_swa_sink_stickbreaking_adaln"


def _split_points(sizes):
    pts, acc = [], 0
    for s in sizes[:-1]:
        acc += s
        pts.append(acc)
    return pts


def rms_norm(x, g):
    xf = x.astype(jnp.float32)
    y = xf * lax.rsqrt(jnp.mean(xf * xf, axis=-1, keepdims=True) + EPS)
    return (y * g.astype(jnp.float32)).astype(x.dtype)


def rope(x, pos):
    half = x.shape[-1] // 2
    inv = ROPE_THETA ** (-jnp.arange(half, dtype=jnp.float32) / half)
    ang = pos.astype(jnp.float32)[..., None] * inv
    cos = jnp.cos(ang)[:, :, None, :]
    sin = jnp.sin(ang)[:, :, None, :]
    xf = x.astype(jnp.float32)
    x1, x2 = xf[..., :half], xf[..., half:]
    out = jnp.concatenate([x1 * cos - x2 * sin, x2 * cos + x1 * sin], axis=-1)
    return out.astype(x.dtype)


def modulate(x, c, norm_g, w_mod, b_mod):
    mod = jax.nn.silu(c) @ w_mod + b_mod
    shift, scale, gate = jnp.split(mod, 3, axis=-1)
    h = rms_norm(x, norm_g) * (1 + scale[:, None, :]) + shift[:, None, :]
    return h, gate[:, None, :]


def diff_attention(q, k, v, lam):
    B, S, H, _, d = q.shape
    nb = S // BLOCK
    scale = d ** -0.5
    qb = q.astype(jnp.float32).reshape(B, nb, BLOCK, H, 2, d).transpose(1, 0, 2, 3, 4, 5)
    kf = k.astype(jnp.float32)
    vf = v.astype(jnp.float32)
    key_idx = jnp.arange(S)

    def block(args):
        qi, i = args
        s = jnp.einsum('bthcd,bshcd->bhcts', qi, kf) * scale
        q_idx = i * BLOCK + jnp.arange(BLOCK)
        mask = key_idx[None, :] <= q_idx[:, None]
        p = jax.nn.softmax(jnp.where(mask, s, -jnp.inf), axis=-1)
        w = p[:, :, 0] - lam * p[:, :, 1]
        return jnp.einsum('bhts,bshe->bthe', w, vf)

    out = lax.map(block, (qb, jnp.arange(nb)))
    return out.transpose(1, 0, 2, 3, 4).reshape(B, S, H, 2 * d).astype(v.dtype)


def sliding_window_sink_attention(q, k, v, sinks):
    B, S, Hq, d = q.shape
    Hkv = k.shape[2]
    G = Hq // Hkv
    nb = S // BLOCK
    scale = d ** -0.5
    qb = q.astype(jnp.float32).reshape(B, nb, BLOCK, Hkv, G, d)
    kb = k.astype(jnp.float32).reshape(B, nb, BLOCK, Hkv, d)
    vb = v.astype(jnp.float32).reshape(B, nb, BLOCK, Hkv, d)
    prev = lambda t: jnp.concatenate([jnp.zeros_like(t[:, :1]), t[:, :-1]], axis=1)
    kk = jnp.concatenate([prev(kb), kb], axis=2)
    vv = jnp.concatenate([prev(vb), vb], axis=2)
    s = jnp.einsum('bntkgd,bnskd->bnkgts', qb, kk) * scale
    t_rel = jnp.arange(BLOCK)[:, None] + BLOCK
    s_rel = jnp.arange(2 * BLOCK)[None, :]
    blk = jnp.arange(nb)[:, None, None]
    mask = (s_rel <= t_rel) & (t_rel - s_rel < WINDOW) & ((blk > 0) | (s_rel >= BLOCK))
    s = jnp.where(mask[None, :, None, None], s, -jnp.inf)
    sink = sinks.astype(jnp.float32).reshape(Hkv, G)[None, None, :, :, None, None]
    m = jnp.maximum(jnp.max(s, axis=-1, keepdims=True), sink)
    e = jnp.exp(s - m)
    p = e / (jnp.sum(e, axis=-1, keepdims=True) + jnp.exp(sink - m))
    out = jnp.einsum('bnkgts,bnskd->bntkgd', p, vv)
    return out.reshape(B, S, Hq, d).astype(v.dtype)


def stick_breaking_attention(q, k, v):
    B, S, H, d = q.shape
    nb = S // BLOCK
    scale = d ** -0.5
    qb = q.astype(jnp.float32).reshape(B, nb, BLOCK, H, d).transpose(1, 0, 2, 3, 4)
    kf = k.astype(jnp.float32)
    vf = v.astype(jnp.float32)
    key_idx = jnp.arange(S)

    def block(args):
        qi, i = args
        z = jnp.einsum('bthd,bshd->bhts', qi, kf) * scale
        q_idx = i * BLOCK + jnp.arange(BLOCK)
        strict = key_idx[None, :] < q_idx[:, None]
        log_1m = jnp.where(strict, jax.nn.log_sigmoid(-z), 0.0)
        suffix = lax.cumsum(log_1m, axis=3, reverse=True) - log_1m
        w = jnp.where(strict, jnp.exp(jax.nn.log_sigmoid(z) + suffix), 0.0)
        return jnp.einsum('bhts,bshd->bthd', w, vf)

    out = lax.map(block, (qb, jnp.arange(nb)))
    return out.transpose(1, 0, 2, 3, 4).reshape(B, S, H, d).astype(v.dtype)


def even_layer(x, c, positions, layer, norm_g, w_mod, b_mod, w_in, a_q_gain, a_k_gain,
               lq1, lk1, lq2, lk2, a_subln_g, b_q_gain, b_k_gain, b_sinks, w_out):
    B, S, _ = x.shape
    h, gate = modulate(x, c, norm_g, w_mod, b_mod)
    u = h @ w_in
    qa, ka, va, ga, qb, kb, vb, gb = jnp.split(u, _split_points(EVEN_SIZES), axis=-1)
    qa = rope(rms_norm(qa.reshape(B, S, 2 * A_HEADS, HEAD_DIM), a_q_gain), positions)
    ka = rope(rms_norm(ka.reshape(B, S, 2 * A_HEADS, HEAD_DIM), a_k_gain), positions)
    lambda_init = 0.8 - 0.6 * math.exp(-0.3 * layer)
    f32 = jnp.float32
    lam = (jnp.exp(jnp.sum(lq1.astype(f32) * lk1.astype(f32)))
           - jnp.exp(jnp.sum(lq2.astype(f32) * lk2.astype(f32))) + lambda_init)
    ya = diff_attention(qa.reshape(B, S, A_HEADS, 2, HEAD_DIM),
                        ka.reshape(B, S, A_HEADS, 2, HEAD_DIM),
                        va.reshape(B, S, A_HEADS, A_VDIM), lam)
    ya = (rms_norm(ya, a_subln_g) * (1 - lambda_init)).reshape(B, S, A_WIDTH)
    qb = rope(rms_norm(qb.reshape(B, S, B_HEADS, HEAD_DIM), b_q_gain), positions)
    kb = rope(rms_norm(kb.reshape(B, S, B_KV_HEADS, HEAD_DIM), b_k_gain), positions)
    yb = sliding_window_sink_attention(qb, kb, vb.reshape(B, S, B_KV_HEADS, HEAD_DIM), b_sinks)
    yb = yb.reshape(B, S, B_WIDTH)
    y = jnp.concatenate([ya * jax.nn.silu(ga), yb * jax.nn.silu(gb)], axis=-1) @ w_out
    return x + gate * y


def odd_layer(x, c, norm_g, w_mod, b_mod, w_in, w_out):
    B, S, _ = x.shape
    h, gate = modulate(x, c, norm_g, w_mod, b_mod)
    q, k, v, g = jnp.split(h @ w_in, 4, axis=-1)
    shp = (B, S, C_HEADS, HEAD_DIM)
    y = stick_breaking_attention(q.reshape(shp), k.reshape(shp), v.reshape(shp)).reshape(B, S, C_WIDTH)
    y = (y * jax.nn.silu(g)) @ w_out
    return x + gate * y


def setup_inputs(seed: int = 0) -> dict:
    key = jax.random.key(seed)
    ks = iter(jax.random.split(key, 32))
    nrm = lambda shape, s: jax.random.normal(next(ks), shape, jnp.float32) * s
    D = D_MODEL
    return {
        "x": nrm((BATCH, SEQ, D), 1.0),
        "c": nrm((BATCH, D), 1.0),
        "positions": jnp.broadcast_to(jnp.arange(SEQ, dtype=jnp.int32), (BATCH, SEQ)),
        "even_norm_g": 1.0 + nrm((N_EVEN, D), 0.05),
        "even_w_mod": nrm((N_EVEN, D, 3 * D), 0.2 * D ** -0.5),
        "even_b_mod": nrm((N_EVEN, 3 * D), 0.02),
        "even_w_in": nrm((N_EVEN, D, EVEN_IN), D ** -0.5),
        "a_q_gain": 1.0 + nrm((N_EVEN, HEAD_DIM), 0.05),
        "a_k_gain": 1.0 + nrm((N_EVEN, HEAD_DIM), 0.05),
        "a_lambda_q1": nrm((N_EVEN, HEAD_DIM), 0.1),
        "a_lambda_k1": nrm((N_EVEN, HEAD_DIM), 0.1),
        "a_lambda_q2": nrm((N_EVEN, HEAD_DIM), 0.1),
        "a_lambda_k2": nrm((N_EVEN, HEAD_DIM), 0.1),
        "a_subln_g": 1.0 + nrm((N_EVEN, A_VDIM), 0.05),
        "b_q_gain": 1.0 + nrm((N_EVEN, HEAD_DIM), 0.05),
        "b_k_gain": 1.0 + nrm((N_EVEN, HEAD_DIM), 0.05),
        "b_sinks": nrm((N_EVEN, B_HEADS), 0.5),
        "even_w_out": nrm((N_EVEN, EVEN_MIX, D), EVEN_MIX ** -0.5),
        "odd_norm_g": 1.0 + nrm((N_ODD, D), 0.05),
        "odd_w_mod": nrm((N_ODD, D, 3 * D), 0.2 * D ** -0.5),
        "odd_b_mod": nrm((N_ODD, 3 * D), 0.02),
        "odd_w_in": nrm((N_ODD, D, ODD_IN), D ** -0.5),
        "odd_w_out": nrm((N_ODD, C_WIDTH, D), C_WIDTH ** -0.5),
    }


def reference(x, c, positions, even_norm_g, even_w_mod, even_b_mod, even_w_in, a_q_gain, a_k_gain,
              a_lambda_q1, a_lambda_k1, a_lambda_q2, a_lambda_k2, a_subln_g, b_q_gain, b_k_gain,
              b_sinks, even_w_out, odd_norm_g, odd_w_mod, odd_b_mod, odd_w_in, odd_w_out):
    for layer in range(DEPTH):
        j = layer // 2
        if layer % 2 == 0:
            x = even_layer(x, c, positions, layer, even_norm_g[j], even_w_mod[j], even_b_mod[j],
                           even_w_in[j], a_q_gain[j], a_k_gain[j], a_lambda_q1[j], a_lambda_k1[j],
                           a_lambda_q2[j], a_lambda_k2[j], a_subln_g[j], b_q_gain[j], b_k_gain[j],
                           b_sinks[j], even_w_out[j])
        else:
            x = odd_layer(x, c, odd_norm_g[j], odd_w_mod[j], odd_b_mod[j], odd_w_in[j], odd_w_out[j])
    return x
```

```python
import functools
import math

import jax
import jax.numpy as jnp
from jax import lax
from jax.experimental import pallas as pl
from jax.experimental.pallas import tpu as pltpu

F32 = jnp.float32
BF16 = jnp.bfloat16

HEAD_DIM = 64
ROPE_THETA = 10000.0
EPS = 1e-6
WINDOW = 128
LANES = 128
QK_SCALE = HEAD_DIM ** -0.5

A_HEADS = 4
A_QK = A_HEADS * 2 * HEAD_DIM
A_WIDTH = A_HEADS * 2 * HEAD_DIM
B_HEADS = 8
B_KV_HEADS = 2
B_WIDTH = B_HEADS * HEAD_DIM
B_KV = B_KV_HEADS * HEAD_DIM
C_HEADS = 16
C_WIDTH = C_HEADS * HEAD_DIM

OFF_QA = 0
OFF_KA = OFF_QA + A_QK
OFF_VA = OFF_KA + A_QK
OFF_GA = OFF_VA + A_WIDTH
OFF_QB = OFF_GA + A_WIDTH
OFF_KB = OFF_QB + B_WIDTH
OFF_VB = OFF_KB + B_KV
OFF_GB = OFF_VB + B_KV
EVEN_IN = OFF_GB + B_WIDTH

EVEN_CHUNKS = (
    (OFF_QA, 256, "qk"), (OFF_QA + 256, 256, "qk"),
    (OFF_KA, 256, "qk"), (OFF_KA + 256, 256, "qk"),
    (OFF_VA, 512, "plain"), (OFF_GA, 512, "silu"),
    (OFF_QB, 256, "qk"), (OFF_QB + 256, 256, "qk"),
    (OFF_KB, 128, "qk"), (OFF_VB, 128, "plain"), (OFF_GB, 512, "silu"),
)
ODD_CHUNKS = tuple(
    (kind_i * C_WIDTH + half * 512, 512, kind)
    for kind_i, kind in enumerate(("scale", "plain", "plain", "silu"))
    for half in range(2)
)

NEG_BIG = -1e30
VMEM_LIMIT = 48 * 1024 * 1024


def _cparams(sem):
    return pltpu.CompilerParams(dimension_semantics=sem, vmem_limit_bytes=VMEM_LIMIT)


def _mod_kernel(c_ref, w_ref, b_ref, o_ref):
    c = c_ref[...]
    sc = c * jax.nn.sigmoid(c)
    o_ref[0] = jnp.dot(sc, w_ref[...], preferred_element_type=F32,
                       precision=lax.Precision.HIGHEST) + b_ref[0]


def _modulation(c, w_mod, b_mod):
    B, D = c.shape
    rows = 8
    cp = jnp.pad(c, ((0, rows - B), (0, 0)))
    out = pl.pallas_call(
        _mod_kernel,
        grid=(3,),
        in_specs=[
            pl.BlockSpec((rows, D), lambda j: (0, 0)),
            pl.BlockSpec((D, D), lambda j: (0, j)),
            pl.BlockSpec((1, 1, D), lambda j: (j, 0, 0)),
        ],
        out_specs=pl.BlockSpec((1, rows, D), lambda j: (j, 0, 0)),
        out_shape=jax.ShapeDtypeStruct((3, rows, D), F32),
        compiler_params=_cparams(("arbitrary",)),
        name="modulation",
    )(cp, w_mod, b_mod.reshape(3, 1, D))
    return out[:, :B].reshape(3, B, 1, D)


def _modulated_rows(x_ref, mod_ref, ng_ref):
    x = x_ref[0]
    ms = jnp.mean(x * x, axis=-1, keepdims=True)
    y = x * lax.rsqrt(ms + EPS) * ng_ref[...]
    return y * (1.0 + mod_ref[1, 0]) + mod_ref[0, 0]


def _inproj_even_kernel(x_ref, mod_ref, ng_ref, w_ref, pos_ref, inv_ref, gain_ref, p_ref,
                        o_ref, h_scr, *, tm):
    h_scr[...] = _modulated_rows(x_ref, mod_ref, ng_ref).astype(BF16)
    ang = pos_ref[0].astype(F32) * inv_ref[...]
    cos = jnp.cos(ang)
    sin = jnp.sin(ang)
    lane = lax.broadcasted_iota(jnp.int32, (tm, LANES), 1)
    first = (lane & (HEAD_DIM // 2)) == 0
    sin_s = jnp.where(first, -sin, sin)
    for start, width, kind in EVEN_CHUNKS:
        u = jnp.dot(h_scr[...], w_ref[:, start:start + width], preferred_element_type=F32)
        if kind == "qk":
            sq = u * u
            hi = sq.astype(BF16)
            lo = (sq - hi.astype(F32)).astype(BF16)
            avg = p_ref[:width, :width]
            ms = (jnp.dot(hi, avg, preferred_element_type=F32)
                  + jnp.dot(lo, avg, preferred_element_type=F32))
            un = u * lax.rsqrt(ms + EPS) * gain_ref[:, start:start + width]
            for s in range(width // LANES):
                xs = un[:, s * LANES:(s + 1) * LANES]
                rot = jnp.where(first, pltpu.roll(xs, LANES - HEAD_DIM // 2, 1),
                                pltpu.roll(xs, HEAD_DIM // 2, 1))
                c0 = start + s * LANES
                o_ref[0, :, c0:c0 + LANES] = (xs * cos + rot * sin_s).astype(BF16)
        elif kind == "silu":
            o_ref[0, :, start:start + width] = (u * jax.nn.sigmoid(u)).astype(BF16)
        else:
            o_ref[0, :, start:start + width] = u.astype(BF16)


def _inproj_odd_kernel(x_ref, mod_ref, ng_ref, w_ref, o_ref, h_scr):
    h_scr[...] = _modulated_rows(x_ref, mod_ref, ng_ref).astype(BF16)
    for start, width, kind in ODD_CHUNKS:
        u = jnp.dot(h_scr[...], w_ref[:, start:start + width], preferred_element_type=F32)
        if kind == "scale":
            u = u * QK_SCALE
        elif kind == "silu":
            u = u * jax.nn.sigmoid(u)
        o_ref[0, :, start:start + width] = u.astype(BF16)


def _inproj_even(x, mod, norm_g, w_in, positions, gain_cols, tm):
    B, S, D = x.shape
    half = HEAD_DIM // 2
    inv = ROPE_THETA ** (-jnp.arange(half, dtype=F32) / half)
    inv_cols = jnp.tile(inv, LANES // half).reshape(1, LANES)
    blk = jnp.arange(256) // HEAD_DIM
    avg = jnp.where(blk[:, None] == blk[None, :], 1.0 / HEAD_DIM, 0.0).astype(BF16)
    return pl.pallas_call(
        functools.partial(_inproj_even_kernel, tm=tm),
        grid=(B, S // tm),
        in_specs=[
            pl.BlockSpec((1, tm, D), lambda b, i: (b, i, 0)),
            pl.BlockSpec((3, 1, 1, D), lambda b, i: (0, b, 0, 0)),
            pl.BlockSpec((1, D), lambda b, i: (0, 0)),
            pl.BlockSpec((D, EVEN_IN), lambda b, i: (0, 0)),
            pl.BlockSpec((1, tm, 1), lambda b, i: (b, i, 0)),
            pl.BlockSpec((1, LANES), lambda b, i: (0, 0)),
            pl.BlockSpec((1, EVEN_IN), lambda b, i: (0, 0)),
            pl.BlockSpec((256, 256), lambda b, i: (0, 0)),
        ],
        out_specs=pl.BlockSpec((1, tm, EVEN_IN), lambda b, i: (b, i, 0)),
        out_shape=jax.ShapeDtypeStruct((B, S, EVEN_IN), BF16),
        scratch_shapes=[pltpu.VMEM((tm, D), BF16)],
        compiler_params=_cparams(("parallel", "parallel")),
        name="inproj_even",
    )(x, mod, norm_g.reshape(1, D), w_in.astype(BF16), positions.reshape(B, S, 1),
      inv_cols, gain_cols, avg)


def _inproj_odd(x, mod, norm_g, w_in, tm):
    B, S, D = x.shape
    N = w_in.shape[1]
    return pl.pallas_call(
        _inproj_odd_kernel,
        grid=(B, S // tm),
        in_specs=[
            pl.BlockSpec((1, tm, D), lambda b, i: (b, i, 0)),
            pl.BlockSpec((3, 1, 1, D), lambda b, i: (0, b, 0, 0)),
            pl.BlockSpec((1, D), lambda b, i: (0, 0)),
            pl.BlockSpec((D, N), lambda b, i: (0, 0)),
        ],
        out_specs=pl.BlockSpec((1, tm, N), lambda b, i: (b, i, 0)),
        out_shape=jax.ShapeDtypeStruct((B, S, N), BF16),
        scratch_shapes=[pltpu.VMEM((tm, D), BF16)],
        compiler_params=_cparams(("parallel", "parallel")),
        name="inproj_odd",
    )(x, mod, norm_g.reshape(1, D), w_in.astype(BF16))


def _outproj_kernel(*refs, n_in):
    y_refs = refs[:n_in]
    w_ref, x_ref, mod_ref, o_ref = refs[n_in:]
    acc = None
    off = 0
    for y_ref in y_refs:
        wd = y_ref.shape[-1]
        part = jnp.dot(y_ref[0], w_ref[off:off + wd, :], preferred_element_type=F32)
        acc = part if acc is None else acc + part
        off += wd
    o_ref[0] = x_ref[0] + mod_ref[2, 0] * acc


def _outproj(ys, w_out, x, mod, tm):
    B, S, D = x.shape
    K = w_out.shape[0]
    n_in = len(ys)
    in_specs = [pl.BlockSpec((1, tm, y.shape[-1]), lambda b, i: (b, i, 0)) for y in ys]
    in_specs += [
        pl.BlockSpec((K, D), lambda b, i: (0, 0)),
        pl.BlockSpec((1, tm, D), lambda b, i: (b, i, 0)),
        pl.BlockSpec((3, 1, 1, D), lambda b, i: (0, b, 0, 0)),
    ]
    return pl.pallas_call(
        functools.partial(_outproj_kernel, n_in=n_in),
        grid=(B, S // tm),
        in_specs=in_specs,
        out_specs=pl.BlockSpec((1, tm, D), lambda b, i: (b, i, 0)),
        out_shape=jax.ShapeDtypeStruct((B, S, D), F32),
        compiler_params=_cparams(("parallel", "parallel")),
        name="outproj",
    )(*ys, w_out.astype(BF16), x, mod)


def _diff_attn_kernel(lam_ref, sub_ref, q_ref, k_ref, v_ref, g_ref, o_ref, *, t, lambda_init):
    i = pl.program_id(2)
    q = q_ref[0]
    lane = lax.broadcasted_iota(jnp.int32, (t, LANES), 1)
    zero = jnp.zeros_like(q)
    qs = jnp.concatenate([jnp.where(lane < HEAD_DIM, q, zero),
                          jnp.where(lane >= HEAD_DIM, q, zero)], axis=0)

    def step(j, carry, diagonal):
        m, l, acc = carry
        off = pl.multiple_of(j * t, t)
        k = k_ref[0, pl.ds(off, t), :]
        v = v_ref[0, pl.ds(off, t), :]
        s = lax.dot_general(qs, k, (((1,), (1,)), ((), ())), preferred_element_type=F32)
        if diagonal:
            row = lax.broadcasted_iota(jnp.int32, (2 * t, t), 0) & (t - 1)
            col = lax.broadcasted_iota(jnp.int32, (2 * t, t), 1)
            s = jnp.where(col <= row, s, NEG_BIG)
        m_new = jnp.maximum(m, jnp.max(s, axis=1, keepdims=True))
        alpha = jnp.exp(m - m_new)
        p = jnp.exp(s - m_new)
        l = alpha * l + jnp.sum(p, axis=1, keepdims=True)
        acc = alpha * acc + jnp.dot(p.astype(BF16), v, preferred_element_type=F32)
        return m_new, l, acc

    init = (jnp.full((2 * t, 1), NEG_BIG, F32), jnp.zeros((2 * t, 1), F32),
            jnp.zeros((2 * t, LANES), F32))
    carry = lax.fori_loop(0, i, lambda j, c: step(j, c, False), init)
    _, l, acc = step(i, carry, True)

    lam_rows = lam_ref[...]
    lam = (jnp.exp(jnp.sum(lam_rows[0:1] * lam_rows[1:2], axis=1, keepdims=True))
           - jnp.exp(jnp.sum(lam_rows[2:3] * lam_rows[3:4], axis=1, keepdims=True))
           + lambda_init)
    o = acc[:t] / l[:t] - lam * (acc[t:] / l[t:])
    ms = jnp.mean(o * o, axis=-1, keepdims=True)
    y = o * lax.rsqrt(ms + EPS) * sub_ref[...] * (1.0 - lambda_init)
    o_ref[0] = (y * g_ref[0].astype(F32)).astype(BF16)


def _diff_attention(u, lam_rows, subln_g, lambda_init, t):
    B, S, _ = u.shape
    qb, kb, vb, gb = (OFF_QA // LANES, OFF_KA // LANES, OFF_VA // LANES, OFF_GA // LANES)
    return pl.pallas_call(
        functools.partial(_diff_attn_kernel, t=t, lambda_init=lambda_init),
        grid=(B, A_HEADS, S // t),
        in_specs=[
            pl.BlockSpec((4, HEAD_DIM), lambda b, h, i: (0, 0)),
            pl.BlockSpec((1, LANES), lambda b, h, i: (0, 0)),
            pl.BlockSpec((1, t, LANES), lambda b, h, i: (b, i, qb + h)),
            pl.BlockSpec((1, S, LANES), lambda b, h, i: (b, 0, kb + h)),
            pl.BlockSpec((1, S, LANES), lambda b, h, i: (b, 0, vb + h)),
            pl.BlockSpec((1, t, LANES), lambda b, h, i: (b, i, gb + h)),
        ],
        out_specs=pl.BlockSpec((1, t, LANES), lambda b, h, i: (b, i, h)),
        out_shape=jax.ShapeDtypeStruct((B, S, A_WIDTH), BF16),
        compiler_params=_cparams(("parallel", "parallel", "arbitrary")),
        name="diff_attention",
    )(lam_rows, subln_g.reshape(1, LANES), u, u, u, u)


def _swa_kernel(sink_ref, q_ref, kc_ref, kp_ref, vc_ref, vp_ref, g0_ref, g1_ref, o_ref, *, t):
    i = pl.program_id(1)
    g_refs = (g0_ref, g1_ref)
    heads_per_g = B_HEADS // len(g_refs)
    kk = jnp.concatenate([kp_ref[0], kc_ref[0]], axis=0)
    vv = jnp.concatenate([vp_ref[0], vc_ref[0]], axis=0)
    row = lax.broadcasted_iota(jnp.int32, (t, WINDOW + t), 0)
    col = lax.broadcasted_iota(jnp.int32, (t, WINDOW + t), 1)
    mask = (col > row) & (col <= row + WINDOW) & jnp.logical_or(i > 0, col >= WINDOW)
    group = B_HEADS // B_KV_HEADS
    for h in range(B_HEADS):
        kv = h // group
        qh = q_ref[0, :, h * HEAD_DIM:(h + 1) * HEAD_DIM]
        kh = kk[:, kv * HEAD_DIM:(kv + 1) * HEAD_DIM]
        vh = vv[:, kv * HEAD_DIM:(kv + 1) * HEAD_DIM]
        s = lax.dot_general(qh, kh, (((1,), (1,)), ((), ())), preferred_element_type=F32)
        s = jnp.where(mask, s, NEG_BIG)
        sink = sink_ref[h]
        m = jnp.maximum(jnp.max(s, axis=1, keepdims=True), sink)
        e = jnp.exp(s - m)
        den = jnp.sum(e, axis=1, keepdims=True) + jnp.exp(sink - m)
        o = jnp.dot(e.astype(BF16), vh, preferred_element_type=F32) / den
        gc = (h % heads_per_g) * HEAD_DIM
        gate = g_refs[h // heads_per_g][0, :, gc:gc + HEAD_DIM].astype(F32)
        o_ref[0, :, h * HEAD_DIM:(h + 1) * HEAD_DIM] = (o * gate).astype(BF16)


def _swa_attention(u, sinks, t):
    B, S, _ = u.shape
    per = t // WINDOW
    gw = B_WIDTH // 2
    qb, kb, vb, gb = OFF_QB // B_WIDTH, OFF_KB // LANES, OFF_VB // LANES, OFF_GB // gw
    assert OFF_QB % B_WIDTH == 0 and OFF_GB % gw == 0
    prev = lambda b, i, cb: (b, jnp.maximum(i * per - 1, 0), cb)
    return pl.pallas_call(
        functools.partial(_swa_kernel, t=t),
        grid=(B, S // t),
        in_specs=[
            pl.BlockSpec(memory_space=pltpu.SMEM),
            pl.BlockSpec((1, t, B_WIDTH), lambda b, i: (b, i, qb)),
            pl.BlockSpec((1, t, LANES), lambda b, i: (b, i, kb)),
            pl.BlockSpec((1, WINDOW, LANES), lambda b, i: prev(b, i, kb)),
            pl.BlockSpec((1, t, LANES), lambda b, i: (b, i, vb)),
            pl.BlockSpec((1, WINDOW, LANES), lambda b, i: prev(b, i, vb)),
            pl.BlockSpec((1, t, gw), lambda b, i: (b, i, gb)),
            pl.BlockSpec((1, t, gw), lambda b, i: (b, i, gb + 1)),
        ],
        out_specs=pl.BlockSpec((1, t, B_WIDTH), lambda b, i: (b, i, 0)),
        out_shape=jax.ShapeDtypeStruct((B, S, B_WIDTH), BF16),
        compiler_params=_cparams(("parallel", "parallel")),
        name="swa_attention",
    )(sinks.astype(F32), u, u, u, u, u, u, u)


def _stick_kernel(tri_ref, q_ref, k_ref, v_ref, g_ref, o_ref, *, t):
    i = pl.program_id(2)
    q = q_ref[0]
    lane = lax.broadcasted_iota(jnp.int32, (t, LANES), 1)
    zero = jnp.zeros_like(q)
    qs = jnp.concatenate([jnp.where(lane < HEAD_DIM, q, zero),
                          jnp.where(lane >= HEAD_DIM, q, zero)], axis=0)
    tri = tri_ref[...]

    def step(j, carry, diagonal):
        run, acc = carry
        off = pl.multiple_of(j * t, t)
        k = k_ref[0, pl.ds(off, t), :]
        v = v_ref[0, pl.ds(off, t), :]
        z = lax.dot_general(qs, k, (((1,), (1,)), ((), ())), preferred_element_type=F32)
        log1m = -(jnp.maximum(z, 0.0) + jnp.log(1.0 + jnp.exp(-jnp.abs(z))))
        if diagonal:
            row = lax.broadcasted_iota(jnp.int32, (2 * t, t), 0) & (t - 1)
            col = lax.broadcasted_iota(jnp.int32, (2 * t, t), 1)
            strict = col < row
            log1m = jnp.where(strict, log1m, 0.0)
        hi = log1m.astype(BF16)
        lo = (log1m - hi.astype(F32)).astype(BF16)
        csum = (jnp.dot(hi, tri, preferred_element_type=F32)
                + jnp.dot(lo, tri, preferred_element_type=F32))
        w = jnp.exp(z + csum + run)
        if diagonal:
            w = jnp.where(strict, w, 0.0)
        acc = acc + jnp.dot(w.astype(BF16), v, preferred_element_type=F32)
        return run + csum[:, 0:1], acc

    init = (jnp.zeros((2 * t, 1), F32), jnp.zeros((2 * t, LANES), F32))
    carry = step(i, init, True)
    _, acc = lax.fori_loop(0, i, lambda jj, c: step(i - 1 - jj, c, False), carry)
    o = jnp.where(lane < HEAD_DIM, acc[:t], acc[t:])
    o_ref[0] = (o * g_ref[0].astype(F32)).astype(BF16)


def _stick_attention(u, t):
    B, S, _ = u.shape
    pairs = C_WIDTH // LANES
    row = jnp.arange(t)
    tri = (row[:, None] >= row[None, :]).astype(BF16)
    return pl.pallas_call(
        functools.partial(_stick_kernel, t=t),
        grid=(B, pairs, S // t),
        in_specs=[
            pl.BlockSpec((t, t), lambda b, h, i: (0, 0)),
            pl.BlockSpec((1, t, LANES), lambda b, h, i: (b, i, h)),
            pl.BlockSpec((1, S, LANES), lambda b, h, i: (b, 0, pairs + h)),
            pl.BlockSpec((1, S, LANES), lambda b, h, i: (b, 0, 2 * pairs + h)),
            pl.BlockSpec((1, t, LANES), lambda b, h, i: (b, i, 3 * pairs + h)),
        ],
        out_specs=pl.BlockSpec((1, t, LANES), lambda b, h, i: (b, i, h)),
        out_shape=jax.ShapeDtypeStruct((B, S, C_WIDTH), BF16),
        compiler_params=_cparams(("parallel", "parallel", "arbitrary")),
        name="stick_attention",
    )(tri, u, u, u, u)


def _tile(n, pref):
    t = min(pref, n)
    assert n % t == 0
    return t


def _even_layer(x, c, positions, layer, norm_g, w_mod, b_mod, w_in, a_q_gain, a_k_gain,
                lq1, lk1, lq2, lk2, a_subln_g, b_q_gain, b_k_gain, b_sinks, w_out):
    B, S, D = x.shape
    mod = _modulation(c, w_mod, b_mod)
    ones = lambda n: jnp.ones((n,), F32)
    gain_cols = jnp.concatenate([
        jnp.tile(a_q_gain * QK_SCALE, A_QK // HEAD_DIM), jnp.tile(a_k_gain, A_QK // HEAD_DIM),
        ones(2 * A_WIDTH),
        jnp.tile(b_q_gain * QK_SCALE, B_HEADS), jnp.tile(b_k_gain, B_KV_HEADS),
        ones(B_KV + B_WIDTH),
    ]).astype(F32).reshape(1, EVEN_IN)
    u = _inproj_even(x, mod, norm_g, w_in, positions, gain_cols, _tile(S, 512))
    lambda_init = 0.8 - 0.6 * math.exp(-0.3 * layer)
    lam_rows = jnp.stack([lq1, lk1, lq2, lk2]).astype(F32)
    ya = _diff_attention(u, lam_rows, a_subln_g.astype(F32), lambda_init, _tile(S, 256))
    yb = _swa_attention(u, b_sinks, _tile(S, 256))
    return _outproj([ya, yb], w_out, x, mod, _tile(S, 512))


def _odd_layer(x, c, norm_g, w_mod, b_mod, w_in, w_out):
    B, S, D = x.shape
    mod = _modulation(c, w_mod, b_mod)
    u = _inproj_odd(x, mod, norm_g, w_in, _tile(S, 512))
    y = _stick_attention(u, _tile(S, 256))
    return _outproj([y], w_out, x, mod, _tile(S, 512))


def kernel(x, c, positions, even_norm_g, even_w_mod, even_b_mod, even_w_in, a_q_gain, a_k_gain,
           a_lambda_q1, a_lambda_k1, a_lambda_q2, a_lambda_k2, a_subln_g, b_q_gain, b_k_gain,
           b_sinks, even_w_out, odd_norm_g, odd_w_mod, odd_b_mod, odd_w_in, odd_w_out):
    depth = even_norm_g.shape[0] + odd_norm_g.shape[0]
    for layer in range(depth):
        j = layer // 2
        if layer % 2 == 0:
            x = _even_layer(x, c, positions, layer, even_norm_g[j], even_w_mod[j], even_b_mod[j],
                            even_w_in[j], a_q_gain[j], a_k_gain[j], a_lambda_q1[j], a_lambda_k1[j],
                            a_lambda_q2[j], a_lambda_k2[j], a_subln_g[j], b_q_gain[j], b_k_gain[j],
                            b_sinks[j], even_w_out[j])
        else:
            x = _odd_layer(x, c, odd_norm_g[j], odd_w_mod[j], odd_b_mod[j], odd_w_in[j], odd_w_out[j])
    return x
```

```python
import functools
import math

import jax
import jax.numpy as jnp
from jax import lax
from jax.experimental import pallas as pl
from jax.experimental.pallas import tpu as pltpu

F32 = jnp.float32
BF16 = jnp.bfloat16

HEAD_DIM = 64
ROPE_THETA = 10000.0
EPS = 1e-6
WINDOW = 128
LANES = 128
QK_SCALE = HEAD_DIM ** -0.5

A_HEADS = 4
A_QK = A_HEADS * 2 * HEAD_DIM
A_WIDTH = A_HEADS * 2 * HEAD_DIM
B_HEADS = 8
B_KV_HEADS = 2
B_WIDTH = B_HEADS * HEAD_DIM
B_KV = B_KV_HEADS * HEAD_DIM
C_HEADS = 16
C_WIDTH = C_HEADS * HEAD_DIM

OFF_QA = 0
OFF_KA = OFF_QA + A_QK
OFF_VA = OFF_KA + A_QK
OFF_GA = OFF_VA + A_WIDTH
OFF_QB = OFF_GA + A_WIDTH
OFF_KB = OFF_QB + B_WIDTH
OFF_VB = OFF_KB + B_KV
OFF_GB = OFF_VB + B_KV
EVEN_IN = OFF_GB + B_WIDTH

EVEN_CHUNKS = (
    (OFF_QA, 256, "qk"), (OFF_QA + 256, 256, "qk"),
    (OFF_KA, 256, "qk"), (OFF_KA + 256, 256, "qk"),
    (OFF_VA, 512, "plain"), (OFF_GA, 512, "silu"),
    (OFF_QB, 256, "qk"), (OFF_QB + 256, 256, "qk"),
    (OFF_KB, 128, "qk"), (OFF_VB, 128, "plain"), (OFF_GB, 512, "silu"),
)
ODD_CHUNKS = tuple(
    (kind_i * C_WIDTH + half * 512, 512, kind)
    for kind_i, kind in enumerate(("scale", "plain", "plain", "silu"))
    for half in range(2)
)

NEG_BIG = -1e30
EXP_UNDERFLOW = -104.0
VMEM_LIMIT = 48 * 1024 * 1024


def _cparams(sem):
    return pltpu.CompilerParams(dimension_semantics=sem, vmem_limit_bytes=VMEM_LIMIT)


def _mod_kernel(c_ref, w_ref, b_ref, o_ref):
    c = c_ref[...]
    sc = c * jax.nn.sigmoid(c)
    o_ref[0] = jnp.dot(sc, w_ref[...], preferred_element_type=F32,
                       precision=lax.Precision.HIGHEST) + b_ref[0]


def _modulation(c, w_mod, b_mod):
    B, D = c.shape
    rows = 8
    cp = jnp.pad(c, ((0, rows - B), (0, 0)))
    out = pl.pallas_call(
        _mod_kernel,
        grid=(3,),
        in_specs=[
            pl.BlockSpec((rows, D), lambda j: (0, 0)),
            pl.BlockSpec((D, D), lambda j: (0, j)),
            pl.BlockSpec((1, 1, D), lambda j: (j, 0, 0)),
        ],
        out_specs=pl.BlockSpec((1, rows, D), lambda j: (j, 0, 0)),
        out_shape=jax.ShapeDtypeStruct((3, rows, D), F32),
        compiler_params=_cparams(("arbitrary",)),
        name="modulation",
    )(cp, w_mod, b_mod.reshape(3, 1, D))
    return out[:, :B].reshape(3, B, 1, D)


def _modulated_rows(x_ref, mod_ref, ng_ref):
    x = x_ref[0]
    ms = jnp.mean(x * x, axis=-1, keepdims=True)
    y = x * lax.rsqrt(ms + EPS) * ng_ref[...]
    return y * (1.0 + mod_ref[1, 0]) + mod_ref[0, 0]


def _inproj_even_kernel(x_ref, mod_ref, ng_ref, w_ref, pos_ref, inv_ref, gain_ref, p_ref,
                        o_ref, h_scr, *, tm):
    h_scr[...] = _modulated_rows(x_ref, mod_ref, ng_ref).astype(BF16)
    ang = pos_ref[0].astype(F32) * inv_ref[...]
    cos = jnp.cos(ang)
    sin = jnp.sin(ang)
    lane = lax.broadcasted_iota(jnp.int32, (tm, LANES), 1)
    first = (lane & (HEAD_DIM // 2)) == 0
    sin_s = jnp.where(first, -sin, sin)
    for start, width, kind in EVEN_CHUNKS:
        u = jnp.dot(h_scr[...], w_ref[:, start:start + width], preferred_element_type=F32)
        if kind == "qk":
            sq = u * u
            hi = sq.astype(BF16)
            lo = (sq - hi.astype(F32)).astype(BF16)
            avg = p_ref[:width, :width]
            ms = (jnp.dot(hi, avg, preferred_element_type=F32)
                  + jnp.dot(lo, avg, preferred_element_type=F32))
            un = u * lax.rsqrt(ms + EPS) * gain_ref[:, start:start + width]
            for s in range(width // LANES):
                xs = un[:, s * LANES:(s + 1) * LANES]
                rot = jnp.where(first, pltpu.roll(xs, LANES - HEAD_DIM // 2, 1),
                                pltpu.roll(xs, HEAD_DIM // 2, 1))
                c0 = start + s * LANES
                o_ref[0, :, c0:c0 + LANES] = (xs * cos + rot * sin_s).astype(BF16)
        elif kind == "silu":
            o_ref[0, :, start:start + width] = (u * jax.nn.sigmoid(u)).astype(BF16)
        else:
            o_ref[0, :, start:start + width] = u.astype(BF16)


def _inproj_odd_kernel(x_ref, mod_ref, ng_ref, w_ref, o_ref, h_scr):
    h_scr[...] = _modulated_rows(x_ref, mod_ref, ng_ref).astype(BF16)
    for start, width, kind in ODD_CHUNKS:
        u = jnp.dot(h_scr[...], w_ref[:, start:start + width], preferred_element_type=F32)
        if kind == "scale":
            u = u * QK_SCALE
        elif kind == "silu":
            u = u * jax.nn.sigmoid(u)
        o_ref[0, :, start:start + width] = u.astype(BF16)


def _inproj_even(x, mod, norm_g, w_in, positions, gain_cols, tm):
    B, S, D = x.shape
    half = HEAD_DIM // 2
    inv = ROPE_THETA ** (-jnp.arange(half, dtype=F32) / half)
    inv_cols = jnp.tile(inv, LANES // half).reshape(1, LANES)
    blk = jnp.arange(256) // HEAD_DIM
    avg = jnp.where(blk[:, None] == blk[None, :], 1.0 / HEAD_DIM, 0.0).astype(BF16)
    return pl.pallas_call(
        functools.partial(_inproj_even_kernel, tm=tm),
        grid=(B, S // tm),
        in_specs=[
            pl.BlockSpec((1, tm, D), lambda b, i: (b, i, 0)),
            pl.BlockSpec((3, 1, 1, D), lambda b, i: (0, b, 0, 0)),
            pl.BlockSpec((1, D), lambda b, i: (0, 0)),
            pl.BlockSpec((D, EVEN_IN), lambda b, i: (0, 0)),
            pl.BlockSpec((1, tm, 1), lambda b, i: (b, i, 0)),
            pl.BlockSpec((1, LANES), lambda b, i: (0, 0)),
            pl.BlockSpec((1, EVEN_IN), lambda b, i: (0, 0)),
            pl.BlockSpec((256, 256), lambda b, i: (0, 0)),
        ],
        out_specs=pl.BlockSpec((1, tm, EVEN_IN), lambda b, i: (b, i, 0)),
        out_shape=jax.ShapeDtypeStruct((B, S, EVEN_IN), BF16),
        scratch_shapes=[pltpu.VMEM((tm, D), BF16)],
        compiler_params=_cparams(("parallel", "parallel")),
        name="inproj_even",
    )(x, mod, norm_g.reshape(1, D), w_in.astype(BF16), positions.reshape(B, S, 1),
      inv_cols, gain_cols, avg)


def _inproj_odd(x, mod, norm_g, w_in, tm):
    B, S, D = x.shape
    N = w_in.shape[1]
    return pl.pallas_call(
        _inproj_odd_kernel,
        grid=(B, S // tm),
        in_specs=[
            pl.BlockSpec((1, tm, D), lambda b, i: (b, i, 0)),
            pl.BlockSpec((3, 1, 1, D), lambda b, i: (0, b, 0, 0)),
            pl.BlockSpec((1, D), lambda b, i: (0, 0)),
            pl.BlockSpec((D, N), lambda b, i: (0, 0)),
        ],
        out_specs=pl.BlockSpec((1, tm, N), lambda b, i: (b, i, 0)),
        out_shape=jax.ShapeDtypeStruct((B, S, N), BF16),
        scratch_shapes=[pltpu.VMEM((tm, D), BF16)],
        compiler_params=_cparams(("parallel", "parallel")),
        name="inproj_odd",
    )(x, mod, norm_g.reshape(1, D), w_in.astype(BF16))


def _outproj_kernel(*refs, n_in):
    y_refs = refs[:n_in]
    w_ref, x_ref, mod_ref, o_ref = refs[n_in:]
    acc = None
    off = 0
    for y_ref in y_refs:
        wd = y_ref.shape[-1]
        part = jnp.dot(y_ref[0], w_ref[off:off + wd, :], preferred_element_type=F32)
        acc = part if acc is None else acc + part
        off += wd
    o_ref[0] = x_ref[0] + mod_ref[2, 0] * acc


def _outproj(ys, w_out, x, mod, tm):
    B, S, D = x.shape
    K = w_out.shape[0]
    n_in = len(ys)
    in_specs = [pl.BlockSpec((1, tm, y.shape[-1]), lambda b, i: (b, i, 0)) for y in ys]
    in_specs += [
        pl.BlockSpec((K, D), lambda b, i: (0, 0)),
        pl.BlockSpec((1, tm, D), lambda b, i: (b, i, 0)),
        pl.BlockSpec((3, 1, 1, D), lambda b, i: (0, b, 0, 0)),
    ]
    return pl.pallas_call(
        functools.partial(_outproj_kernel, n_in=n_in),
        grid=(B, S // tm),
        in_specs=in_specs,
        out_specs=pl.BlockSpec((1, tm, D), lambda b, i: (b, i, 0)),
        out_shape=jax.ShapeDtypeStruct((B, S, D), F32),
        compiler_params=_cparams(("parallel", "parallel")),
        name="outproj",
    )(*ys, w_out.astype(BF16), x, mod)


def _diff_attn_kernel(lam_ref, sub_ref, q_ref, k_ref, v_ref, g_ref, o_ref, *, t, lambda_init):
    i = pl.program_id(2)
    q = q_ref[0]
    lane = lax.broadcasted_iota(jnp.int32, (t, LANES), 1)
    zero = jnp.zeros_like(q)
    qs = jnp.concatenate([jnp.where(lane < HEAD_DIM, q, zero),
                          jnp.where(lane >= HEAD_DIM, q, zero)], axis=0)

    def step(j, carry, diagonal):
        m, l, acc = carry
        off = pl.multiple_of(j * t, t)
        k = k_ref[0, pl.ds(off, t), :]
        v = v_ref[0, pl.ds(off, t), :]
        s = lax.dot_general(qs, k, (((1,), (1,)), ((), ())), preferred_element_type=F32)
        if diagonal:
            row = lax.broadcasted_iota(jnp.int32, (2 * t, t), 0) & (t - 1)
            col = lax.broadcasted_iota(jnp.int32, (2 * t, t), 1)
            s = jnp.where(col <= row, s, NEG_BIG)
        m_new = jnp.maximum(m, jnp.max(s, axis=1, keepdims=True))
        alpha = jnp.exp(m - m_new)
        p = jnp.exp(s - m_new)
        l = alpha * l + jnp.sum(p, axis=1, keepdims=True)
        acc = alpha * acc + jnp.dot(p.astype(BF16), v, preferred_element_type=F32)
        return m_new, l, acc

    init = (jnp.full((2 * t, 1), NEG_BIG, F32), jnp.zeros((2 * t, 1), F32),
            jnp.zeros((2 * t, LANES), F32))
    carry = lax.fori_loop(0, i, lambda j, c: step(j, c, False), init)
    _, l, acc = step(i, carry, True)

    lam_rows = lam_ref[...]
    lam = (jnp.exp(jnp.sum(lam_rows[0:1] * lam_rows[1:2], axis=1, keepdims=True))
           - jnp.exp(jnp.sum(lam_rows[2:3] * lam_rows[3:4], axis=1, keepdims=True))
           + lambda_init)
    o = acc[:t] / l[:t] - lam * (acc[t:] / l[t:])
    ms = jnp.mean(o * o, axis=-1, keepdims=True)
    y = o * lax.rsqrt(ms + EPS) * sub_ref[...] * (1.0 - lambda_init)
    o_ref[0] = (y * g_ref[0].astype(F32)).astype(BF16)


def _diff_attention(u, lam_rows, subln_g, lambda_init, t):
    B, S, _ = u.shape
    qb, kb, vb, gb = (OFF_QA // LANES, OFF_KA // LANES, OFF_VA // LANES, OFF_GA // LANES)
    return pl.pallas_call(
        functools.partial(_diff_attn_kernel, t=t, lambda_init=lambda_init),
        grid=(B, A_HEADS, S // t),
        in_specs=[
            pl.BlockSpec((4, HEAD_DIM), lambda b, h, i: (0, 0)),
            pl.BlockSpec((1, LANES), lambda b, h, i: (0, 0)),
            pl.BlockSpec((1, t, LANES), lambda b, h, i: (b, i, qb + h)),
            pl.BlockSpec((1, S, LANES), lambda b, h, i: (b, 0, kb + h)),
            pl.BlockSpec((1, S, LANES), lambda b, h, i: (b, 0, vb + h)),
            pl.BlockSpec((1, t, LANES), lambda b, h, i: (b, i, gb + h)),
        ],
        out_specs=pl.BlockSpec((1, t, LANES), lambda b, h, i: (b, i, h)),
        out_shape=jax.ShapeDtypeStruct((B, S, A_WIDTH), BF16),
        compiler_params=_cparams(("parallel", "parallel", "arbitrary")),
        name="diff_attention",
    )(lam_rows, subln_g.reshape(1, LANES), u, u, u, u)


def _swa_kernel(sink_ref, q_ref, kc_ref, kp_ref, vc_ref, vp_ref, g0_ref, g1_ref, o_ref, *, t):
    i = pl.program_id(1)
    g_refs = (g0_ref, g1_ref)
    heads_per_g = B_HEADS // len(g_refs)
    kk = jnp.concatenate([kp_ref[0], kc_ref[0]], axis=0)
    vv = jnp.concatenate([vp_ref[0], vc_ref[0]], axis=0)
    row = lax.broadcasted_iota(jnp.int32, (t, WINDOW + t), 0)
    col = lax.broadcasted_iota(jnp.int32, (t, WINDOW + t), 1)
    mask = (col > row) & (col <= row + WINDOW) & jnp.logical_or(i > 0, col >= WINDOW)
    group = B_HEADS // B_KV_HEADS
    for h in range(B_HEADS):
        kv = h // group
        qh = q_ref[0, :, h * HEAD_DIM:(h + 1) * HEAD_DIM]
        kh = kk[:, kv * HEAD_DIM:(kv + 1) * HEAD_DIM]
        vh = vv[:, kv * HEAD_DIM:(kv + 1) * HEAD_DIM]
        s = lax.dot_general(qh, kh, (((1,), (1,)), ((), ())), preferred_element_type=F32)
        s = jnp.where(mask, s, NEG_BIG)
        sink = sink_ref[h]
        m = jnp.maximum(jnp.max(s, axis=1, keepdims=True), sink)
        e = jnp.exp(s - m)
        den = jnp.sum(e, axis=1, keepdims=True) + jnp.exp(sink - m)
        o = jnp.dot(e.astype(BF16), vh, preferred_element_type=F32) / den
        gc = (h % heads_per_g) * HEAD_DIM
        gate = g_refs[h // heads_per_g][0, :, gc:gc + HEAD_DIM].astype(F32)
        o_ref[0, :, h * HEAD_DIM:(h + 1) * HEAD_DIM] = (o * gate).astype(BF16)


def _swa_attention(u, sinks, t):
    B, S, _ = u.shape
    per = t // WINDOW
    gw = B_WIDTH // 2
    qb, kb, vb, gb = OFF_QB // B_WIDTH, OFF_KB // LANES, OFF_VB // LANES, OFF_GB // gw
    assert OFF_QB % B_WIDTH == 0 and OFF_GB % gw == 0
    prev = lambda b, i, cb: (b, jnp.maximum(i * per - 1, 0), cb)
    return pl.pallas_call(
        functools.partial(_swa_kernel, t=t),
        grid=(B, S // t),
        in_specs=[
            pl.BlockSpec(memory_space=pltpu.SMEM),
            pl.BlockSpec((1, t, B_WIDTH), lambda b, i: (b, i, qb)),
            pl.BlockSpec((1, t, LANES), lambda b, i: (b, i, kb)),
            pl.BlockSpec((1, WINDOW, LANES), lambda b, i: prev(b, i, kb)),
            pl.BlockSpec((1, t, LANES), lambda b, i: (b, i, vb)),
            pl.BlockSpec((1, WINDOW, LANES), lambda b, i: prev(b, i, vb)),
            pl.BlockSpec((1, t, gw), lambda b, i: (b, i, gb)),
            pl.BlockSpec((1, t, gw), lambda b, i: (b, i, gb + 1)),
        ],
        out_specs=pl.BlockSpec((1, t, B_WIDTH), lambda b, i: (b, i, 0)),
        out_shape=jax.ShapeDtypeStruct((B, S, B_WIDTH), BF16),
        compiler_params=_cparams(("parallel", "parallel")),
        name="swa_attention",
    )(sinks.astype(F32), u, u, u, u, u, u, u)


def _stick_kernel(tri_ref, q_ref, k_ref, v_ref, g_ref, o_ref, *, t):
    i = pl.program_id(2)
    q = q_ref[0]
    lane = lax.broadcasted_iota(jnp.int32, (t, LANES), 1)
    zero = jnp.zeros_like(q)
    qs = jnp.concatenate([jnp.where(lane < HEAD_DIM, q, zero),
                          jnp.where(lane >= HEAD_DIM, q, zero)], axis=0)
    tri = tri_ref[...]

    def step(j, carry, diagonal):
        run, acc = carry
        off = pl.multiple_of(j * t, t)
        k = k_ref[0, pl.ds(off, t), :]
        v = v_ref[0, pl.ds(off, t), :]
        z = lax.dot_general(qs, k, (((1,), (1,)), ((), ())), preferred_element_type=F32)
        log1m = -(jnp.maximum(z, 0.0) + jnp.log(1.0 + jnp.exp(-jnp.abs(z))))
        if diagonal:
            row = lax.broadcasted_iota(jnp.int32, (2 * t, t), 0) & (t - 1)
            col = lax.broadcasted_iota(jnp.int32, (2 * t, t), 1)
            strict = col < row
            log1m = jnp.where(strict, log1m, 0.0)
        hi = log1m.astype(BF16)
        lo = (log1m - hi.astype(F32)).astype(BF16)
        csum = (jnp.dot(hi, tri, preferred_element_type=F32)
                + jnp.dot(lo, tri, preferred_element_type=F32))
        w = jnp.exp(z + csum + run)
        if diagonal:
            w = jnp.where(strict, w, 0.0)
        acc = acc + jnp.dot(w.astype(BF16), v, preferred_element_type=F32)
        return run + csum[:, 0:1], acc

    init = (jnp.zeros((2 * t, 1), F32), jnp.zeros((2 * t, LANES), F32))
    run, acc = step(i, init, True)

    def alive(run):
        return jnp.max(run) > EXP_UNDERFLOW

    def cond(c):
        jj, go, _, _ = c
        return jnp.logical_and(jj < i, go)

    def body(c):
        jj, _, run, acc = c
        run, acc = step(i - 1 - jj, (run, acc), False)
        return jj + 1, alive(run), run, acc

    _, _, _, acc = lax.while_loop(cond, body, (jnp.int32(0), alive(run), run, acc))
    o = jnp.where(lane < HEAD_DIM, acc[:t], acc[t:])
    o_ref[0] = (o * g_ref[0].astype(F32)).astype(BF16)


def _stick_attention(u, t):
    B, S, _ = u.shape
    pairs = C_WIDTH // LANES
    row = jnp.arange(t)
    tri = (row[:, None] >= row[None, :]).astype(BF16)
    return pl.pallas_call(
        functools.partial(_stick_kernel, t=t),
        grid=(B, pairs, S // t),
        in_specs=[
            pl.BlockSpec((t, t), lambda b, h, i: (0, 0)),
            pl.BlockSpec((1, t, LANES), lambda b, h, i: (b, i, h)),
            pl.BlockSpec((1, S, LANES), lambda b, h, i: (b, 0, pairs + h)),
            pl.BlockSpec((1, S, LANES), lambda b, h, i: (b, 0, 2 * pairs + h)),
            pl.BlockSpec((1, t, LANES), lambda b, h, i: (b, i, 3 * pairs + h)),
        ],
        out_specs=pl.BlockSpec((1, t, LANES), lambda b, h, i: (b, i, h)),
        out_shape=jax.ShapeDtypeStruct((B, S, C_WIDTH), BF16),
        compiler_params=_cparams(("parallel", "parallel", "arbitrary")),
        name="stick_attention",
    )(tri, u, u, u, u)


def _tile(n, pref):
    t = min(pref, n)
    assert n % t == 0
    return t


def _even_layer(x, c, positions, layer, norm_g, w_mod, b_mod, w_in, a_q_gain, a_k_gain,
                lq1, lk1, lq2, lk2, a_subln_g, b_q_gain, b_k_gain, b_sinks, w_out):
    B, S, D = x.shape
    mod = _modulation(c, w_mod, b_mod)
    ones = lambda n: jnp.ones((n,), F32)
    gain_cols = jnp.concatenate([
        jnp.tile(a_q_gain * QK_SCALE, A_QK // HEAD_DIM), jnp.tile(a_k_gain, A_QK // HEAD_DIM),
        ones(2 * A_WIDTH),
        jnp.tile(b_q_gain * QK_SCALE, B_HEADS), jnp.tile(b_k_gain, B_KV_HEADS),
        ones(B_KV + B_WIDTH),
    ]).astype(F32).reshape(1, EVEN_IN)
    u = _inproj_even(x, mod, norm_g, w_in, positions, gain_cols, _tile(S, 512))
    lambda_init = 0.8 - 0.6 * math.exp(-0.3 * layer)
    lam_rows = jnp.stack([lq1, lk1, lq2, lk2]).astype(F32)
    ya = _diff_attention(u, lam_rows, a_subln_g.astype(F32), lambda_init, _tile(S, 256))
    yb = _swa_attention(u, b_sinks, _tile(S, 256))
    return _outproj([ya, yb], w_out, x, mod, _tile(S, 512))


def _odd_layer(x, c, norm_g, w_mod, b_mod, w_in, w_out):
    B, S, D = x.shape
    mod = _modulation(c, w_mod, b_mod)
    u = _inproj_odd(x, mod, norm_g, w_in, _tile(S, 512))
    y = _stick_attention(u, _tile(S, 256))
    return _outproj([y], w_out, x, mod, _tile(S, 512))


def kernel(x, c, positions, even_norm_g, even_w_mod, even_b_mod, even_w_in, a_q_gain, a_k_gain,
           a_lambda_q1, a_lambda_k1, a_lambda_q2, a_lambda_k2, a_subln_g, b_q_gain, b_k_gain,
           b_sinks, even_w_out, odd_norm_g, odd_w_mod, odd_b_mod, odd_w_in, odd_w_out):
    depth = even_norm_g.shape[0] + odd_norm_g.shape[0]
    for layer in range(depth):
        j = layer // 2
        if layer % 2 == 0:
            x = _even_layer(x, c, positions, layer, even_norm_g[j], even_w_mod[j], even_b_mod[j],
                            even_w_in[j], a_q_gain[j], a_k_gain[j], a_lambda_q1[j], a_lambda_k1[j],
                            a_lambda_q2[j], a_lambda_k2[j], a_subln_g[j], b_q_gain[j], b_k_gain[j],
                            b_sinks[j], even_w_out[j])
        else:
            x = _odd_layer(x, c, odd_norm_g[j], odd_w_mod[j], odd_b_mod[j], odd_w_in[j], odd_w_out[j])
    return x
```

```python
import functools
import math

import jax
import jax.numpy as jnp
from jax import lax
from jax.experimental import pallas as pl
from jax.experimental.pallas import tpu as pltpu

F32 = jnp.float32
BF16 = jnp.bfloat16

HEAD_DIM = 64
ROPE_THETA = 10000.0
EPS = 1e-6
WINDOW = 128
LANES = 128
QK_SCALE = HEAD_DIM ** -0.5

A_HEADS = 4
A_QK = A_HEADS * 2 * HEAD_DIM
A_WIDTH = A_HEADS * 2 * HEAD_DIM
B_HEADS = 8
B_KV_HEADS = 2
B_WIDTH = B_HEADS * HEAD_DIM
B_KV = B_KV_HEADS * HEAD_DIM
C_HEADS = 16
C_WIDTH = C_HEADS * HEAD_DIM

W_QA = 0
W_KA = W_QA + A_QK
W_VA = W_KA + A_QK
W_GA = W_VA + A_WIDTH
W_QB = W_GA + A_WIDTH
W_END = W_QB + B_WIDTH + 2 * B_KV + B_WIDTH

OFF_QA = 0
OFF_KA = OFF_QA + A_QK
OFF_GA = OFF_KA + A_QK
OFF_QB = OFF_GA + A_WIDTH
OFF_KB = OFF_QB + B_WIDTH
OFF_VB = OFF_KB + B_KV
OFF_GB = OFF_VB + B_KV
EVEN_IN = OFF_GB + B_WIDTH

EVEN_CHUNKS = (
    (OFF_QA, 256, "qk"), (OFF_QA + 256, 256, "qk"),
    (OFF_KA, 256, "qk"), (OFF_KA + 256, 256, "qk"),
    (OFF_GA, 512, "silu"),
    (OFF_QB, 256, "qk"), (OFF_QB + 256, 256, "qk"),
    (OFF_KB, 128, "qk"), (OFF_VB, 128, "plain"), (OFF_GB, 512, "silu"),
)
LOG2E = 1.4426950408889634
KV_BLOCK = 256
ODD_IN = 3 * C_WIDTH
ODD_CHUNKS = tuple(
    (kind_i * C_WIDTH + half * 512, 512, kind)
    for kind_i, kind in enumerate(("scale", "plain", "silu"))
    for half in range(2)
)

NEG_BIG = -1e30
EXP2_UNDERFLOW = 151.0
VMEM_LIMIT = 48 * 1024 * 1024


def _neg_abs(x):
    bits = lax.bitcast_convert_type(x, jnp.uint32) | jnp.uint32(0x80000000)
    return lax.bitcast_convert_type(bits, F32)


def _truncate_to_bf16(x):
    bits = lax.bitcast_convert_type(x, jnp.uint32) & jnp.uint32(0xFFFF0000)
    return lax.bitcast_convert_type(bits, F32)


def _cparams(sem):
    return pltpu.CompilerParams(dimension_semantics=sem, vmem_limit_bytes=VMEM_LIMIT)


def _mod_kernel(c_ref, w_ref, b_ref, o_ref):
    c = c_ref[...]
    sc = c * jax.nn.sigmoid(c)
    o_ref[0] = jnp.dot(sc, w_ref[...], preferred_element_type=F32,
                       precision=lax.Precision.HIGHEST) + b_ref[0]


def _modulation(c, w_mod, b_mod):
    B, D = c.shape
    rows = 8
    cp = jnp.pad(c, ((0, rows - B), (0, 0)))
    out = pl.pallas_call(
        _mod_kernel,
        grid=(3,),
        in_specs=[
            pl.BlockSpec((rows, D), lambda j: (0, 0)),
            pl.BlockSpec((D, D), lambda j: (0, j)),
            pl.BlockSpec((1, 1, D), lambda j: (j, 0, 0)),
        ],
        out_specs=pl.BlockSpec((1, rows, D), lambda j: (j, 0, 0)),
        out_shape=jax.ShapeDtypeStruct((3, rows, D), F32),
        compiler_params=_cparams(("arbitrary",)),
        name="modulation",
    )(cp, w_mod, b_mod.reshape(3, 1, D))
    return out[:, :B].reshape(3, B, 1, D)


def _modulated_rows(x_ref, mod_ref, ng_ref):
    x = x_ref[0]
    ms = jnp.mean(x * x, axis=-1, keepdims=True)
    y = x * lax.rsqrt(ms + EPS) * ng_ref[...]
    return y * (1.0 + mod_ref[1, 0]) + mod_ref[0, 0]


def _store_transposed_values(wvt_ref, h_scr, vt_ref, tm):
    vt = lax.dot_general(wvt_ref[...], h_scr[...], (((1,), (1,)), ((), ())),
                         preferred_element_type=F32)
    for g in range(vt.shape[0] // LANES):
        for cb in range(tm // KV_BLOCK):
            vt_ref[0, g, cb] = vt[g * LANES:(g + 1) * LANES,
                                  cb * KV_BLOCK:(cb + 1) * KV_BLOCK].astype(BF16)


def _inproj_even_kernel(x_ref, mod_ref, ng_ref, w_ref, wvt_ref, pos_ref, inv_ref, gain_ref, p_ref,
                        o_ref, vt_ref, h_scr, *, tm):
    h_scr[...] = _modulated_rows(x_ref, mod_ref, ng_ref).astype(BF16)
    _store_transposed_values(wvt_ref, h_scr, vt_ref, tm)
    ang = pos_ref[0].astype(F32) * inv_ref[...]
    cos = jnp.cos(ang)
    sin = jnp.sin(ang)
    lane = lax.broadcasted_iota(jnp.int32, (tm, LANES), 1)
    first = (lane & (HEAD_DIM // 2)) == 0
    sin_s = jnp.where(first, -sin, sin)
    for start, width, kind in EVEN_CHUNKS:
        u = jnp.dot(h_scr[...], w_ref[:, start:start + width], preferred_element_type=F32)
        if kind == "qk":
            sq = u * u
            hi = sq.astype(BF16)
            lo = (sq - hi.astype(F32)).astype(BF16)
            avg = p_ref[:width, :width]
            ms = (jnp.dot(hi, avg, preferred_element_type=F32)
                  + jnp.dot(lo, avg, preferred_element_type=F32))
            un = u * lax.rsqrt(ms + EPS) * gain_ref[:, start:start + width]
            for s in range(width // LANES):
                xs = un[:, s * LANES:(s + 1) * LANES]
                rot = jnp.where(first, pltpu.roll(xs, LANES - HEAD_DIM // 2, 1),
                                pltpu.roll(xs, HEAD_DIM // 2, 1))
                c0 = start + s * LANES
                o_ref[0, :, c0:c0 + LANES] = (xs * cos + rot * sin_s).astype(BF16)
        elif kind == "silu":
            o_ref[0, :, start:start + width] = (u * jax.nn.sigmoid(u)).astype(BF16)
        else:
            o_ref[0, :, start:start + width] = u.astype(BF16)


def _inproj_odd_kernel(x_ref, mod_ref, ng_ref, w_ref, wvt_ref, o_ref, vt_ref, h_scr, *, tm):
    h_scr[...] = _modulated_rows(x_ref, mod_ref, ng_ref).astype(BF16)
    _store_transposed_values(wvt_ref, h_scr, vt_ref, tm)
    for start, width, kind in ODD_CHUNKS:
        u = jnp.dot(h_scr[...], w_ref[:, start:start + width], preferred_element_type=F32)
        if kind == "scale":
            u = u * (QK_SCALE * LOG2E)
        elif kind == "silu":
            u = u * jax.nn.sigmoid(u)
        o_ref[0, :, start:start + width] = u.astype(BF16)


def _inproj_even(x, mod, norm_g, w_in, positions, gain_cols, tm):
    B, S, D = x.shape
    w_rows = jnp.concatenate([w_in[:, :W_VA], w_in[:, W_GA:]], axis=1).astype(BF16)
    w_vat = w_in[:, W_VA:W_GA].T.astype(BF16)
    n_groups = A_WIDTH // LANES
    half = HEAD_DIM // 2
    inv = ROPE_THETA ** (-jnp.arange(half, dtype=F32) / half)
    inv_cols = jnp.tile(inv, LANES // half).reshape(1, LANES)
    blk = jnp.arange(256) // HEAD_DIM
    avg = jnp.where(blk[:, None] == blk[None, :], 1.0 / HEAD_DIM, 0.0).astype(BF16)
    return pl.pallas_call(
        functools.partial(_inproj_even_kernel, tm=tm),
        grid=(B, S // tm),
        in_specs=[
            pl.BlockSpec((1, tm, D), lambda b, i: (b, i, 0)),
            pl.BlockSpec((3, 1, 1, D), lambda b, i: (0, b, 0, 0)),
            pl.BlockSpec((1, D), lambda b, i: (0, 0)),
            pl.BlockSpec((D, EVEN_IN), lambda b, i: (0, 0)),
            pl.BlockSpec((A_WIDTH, D), lambda b, i: (0, 0)),
            pl.BlockSpec((1, tm, 1), lambda b, i: (b, i, 0)),
            pl.BlockSpec((1, LANES), lambda b, i: (0, 0)),
            pl.BlockSpec((1, EVEN_IN), lambda b, i: (0, 0)),
            pl.BlockSpec((256, 256), lambda b, i: (0, 0)),
        ],
        out_specs=[
            pl.BlockSpec((1, tm, EVEN_IN), lambda b, i: (b, i, 0)),
            pl.BlockSpec((1, n_groups, tm // KV_BLOCK, LANES, KV_BLOCK), lambda b, i: (b, 0, i, 0, 0)),
        ],
        out_shape=[
            jax.ShapeDtypeStruct((B, S, EVEN_IN), BF16),
            jax.ShapeDtypeStruct((B, n_groups, S // KV_BLOCK, LANES, KV_BLOCK), BF16),
        ],
        scratch_shapes=[pltpu.VMEM((tm, D), BF16)],
        compiler_params=_cparams(("parallel", "parallel")),
        name="inproj_even",
    )(x, mod, norm_g.reshape(1, D), w_rows, w_vat, positions.reshape(B, S, 1),
      inv_cols, gain_cols, avg)


def _inproj_odd(x, mod, norm_g, w_in, tm):
    B, S, D = x.shape
    w_rows = jnp.concatenate([w_in[:, :2 * C_WIDTH], w_in[:, 3 * C_WIDTH:]], axis=1).astype(BF16)
    w_vt = w_in[:, 2 * C_WIDTH:3 * C_WIDTH].T.astype(BF16)
    n_groups = C_WIDTH // LANES
    return pl.pallas_call(
        functools.partial(_inproj_odd_kernel, tm=tm),
        grid=(B, S // tm),
        in_specs=[
            pl.BlockSpec((1, tm, D), lambda b, i: (b, i, 0)),
            pl.BlockSpec((3, 1, 1, D), lambda b, i: (0, b, 0, 0)),
            pl.BlockSpec((1, D), lambda b, i: (0, 0)),
            pl.BlockSpec((D, ODD_IN), lambda b, i: (0, 0)),
            pl.BlockSpec((C_WIDTH, D), lambda b, i: (0, 0)),
        ],
        out_specs=[
            pl.BlockSpec((1, tm, ODD_IN), lambda b, i: (b, i, 0)),
            pl.BlockSpec((1, n_groups, tm // KV_BLOCK, LANES, KV_BLOCK), lambda b, i: (b, 0, i, 0, 0)),
        ],
        out_shape=[
            jax.ShapeDtypeStruct((B, S, ODD_IN), BF16),
            jax.ShapeDtypeStruct((B, n_groups, S // KV_BLOCK, LANES, KV_BLOCK), BF16),
        ],
        scratch_shapes=[pltpu.VMEM((tm, D), BF16)],
        compiler_params=_cparams(("parallel", "parallel")),
        name="inproj_odd",
    )(x, mod, norm_g.reshape(1, D), w_rows, w_vt)


def _outproj_kernel(*refs, n_in):
    y_refs = refs[:n_in]
    w_ref, x_ref, mod_ref, o_ref = refs[n_in:]
    acc = None
    off = 0
    for y_ref in y_refs:
        wd = y_ref.shape[-1]
        part = jnp.dot(y_ref[0], w_ref[off:off + wd, :], preferred_element_type=F32)
        acc = part if acc is None else acc + part
        off += wd
    o_ref[0] = x_ref[0] + mod_ref[2, 0] * acc


def _outproj(ys, w_out, x, mod, tm):
    B, S, D = x.shape
    K = w_out.shape[0]
    n_in = len(ys)
    in_specs = [pl.BlockSpec((1, tm, y.shape[-1]), lambda b, i: (b, i, 0)) for y in ys]
    in_specs += [
        pl.BlockSpec((K, D), lambda b, i: (0, 0)),
        pl.BlockSpec((1, tm, D), lambda b, i: (b, i, 0)),
        pl.BlockSpec((3, 1, 1, D), lambda b, i: (0, b, 0, 0)),
    ]
    return pl.pallas_call(
        functools.partial(_outproj_kernel, n_in=n_in),
        grid=(B, S // tm),
        in_specs=in_specs,
        out_specs=pl.BlockSpec((1, tm, D), lambda b, i: (b, i, 0)),
        out_shape=jax.ShapeDtypeStruct((B, S, D), F32),
        compiler_params=_cparams(("parallel", "parallel")),
        name="outproj",
    )(*ys, w_out.astype(BF16), x, mod)


def _diff_attn_kernel(lam_ref, sub_ref, q_ref, k_ref, vt_ref, g_ref, o_ref,
                      qs_scr, s0_scr, s1_scr, smax0_scr, smax1_scr, m_scr, l_scr, acc_scr, *,
                      tq, tk, lambda_init):
    i = pl.program_id(2)
    q = q_ref[0]
    lane = lax.broadcasted_iota(jnp.int32, (tq, LANES), 1)
    zero = jnp.zeros_like(q)
    qs_scr[:tq] = jnp.where(lane < HEAD_DIM, q, zero)
    qs_scr[tq:] = jnp.where(lane >= HEAD_DIM, q, zero)

    slots = ((s0_scr, smax0_scr), (s1_scr, smax1_scr))

    def scores(j, slot):
        s_ref, smax_ref = slots[slot]
        k = k_ref[0, pl.ds(pl.multiple_of(j * tk, tk), tk), :]
        s = lax.dot_general(k, qs_scr[...], (((1,), (1,)), ((), ())),
                            preferred_element_type=F32)
        s_ref[...] = s
        smax_ref[...] = jnp.max(s, axis=0, keepdims=True)

    def softmax_pv(j, slot, diag=None):
        s_ref, smax_ref = slots[slot]
        s = s_ref[...]
        if diag is None:
            smax = smax_ref[...]
        else:
            key = lax.broadcasted_iota(jnp.int32, (tk, 2 * tq), 0) + diag * tk
            qry = lax.broadcasted_iota(jnp.int32, (tk, 2 * tq), 1) & (tq - 1)
            s = jnp.where(key <= qry, s, NEG_BIG)
            smax = jnp.max(s, axis=0, keepdims=True)
        m = m_scr[...]
        m_new = jnp.maximum(m, smax)
        alpha = jnp.exp2(m - m_new)
        p = jnp.exp2(s - m_new)
        l_scr[...] = alpha * l_scr[...] + jnp.sum(p, axis=0, keepdims=True)
        m_scr[...] = m_new
        pv = jnp.dot(vt_ref[0, 0, j], p.astype(BF16), preferred_element_type=F32)
        acc_scr[...] = alpha * acc_scr[...] + pv

    m_scr[...] = jnp.full((1, 2 * tq), NEG_BIG, F32)
    l_scr[...] = jnp.zeros((1, 2 * tq), F32)
    acc_scr[...] = jnp.zeros((LANES, 2 * tq), F32)
    n_full = 2 * i
    scores(0, 0)

    def body(jj, carry):
        j0 = 2 * jj
        scores(j0 + 1, 1)
        softmax_pv(j0, 0)
        scores(j0 + 2, 0)
        softmax_pv(j0 + 1, 1)
        return carry

    lax.fori_loop(0, i, body, 0)
    scores(n_full + 1, 1)
    softmax_pv(n_full, 0, diag=0)
    softmax_pv(n_full + 1, 1, diag=1)
    l = l_scr[...]
    acc = acc_scr[...]

    lam_rows = lam_ref[...]
    lam = (jnp.exp(jnp.sum(lam_rows[0:1] * lam_rows[1:2], axis=1, keepdims=True))
           - jnp.exp(jnp.sum(lam_rows[2:3] * lam_rows[3:4], axis=1, keepdims=True))
           + lambda_init)
    ot = acc[:, :tq] / l[:, :tq] - lam * (acc[:, tq:] / l[:, tq:])
    ms = jnp.mean(ot * ot, axis=0, keepdims=True)
    yt = ot * lax.rsqrt(ms + EPS) * (sub_ref[...] * (1.0 - lambda_init))
    o_ref[0] = (yt.T * g_ref[0].astype(F32)).astype(BF16)


def _diff_attention(u, vat, lam_rows, subln_g, lambda_init, tq):
    B, S, _ = u.shape
    tk = KV_BLOCK
    assert tq == 2 * tk and S % tq == 0
    qb, kb, gb = OFF_QA // LANES, OFF_KA // LANES, OFF_GA // LANES
    stat = pltpu.VMEM((1, 2 * tq), F32)
    score = pltpu.VMEM((tk, 2 * tq), F32)
    return pl.pallas_call(
        functools.partial(_diff_attn_kernel, tq=tq, tk=tk, lambda_init=lambda_init),
        grid=(B, A_HEADS, S // tq),
        in_specs=[
            pl.BlockSpec((4, HEAD_DIM), lambda b, h, i: (0, 0)),
            pl.BlockSpec((LANES, 1), lambda b, h, i: (0, 0)),
            pl.BlockSpec((1, tq, LANES), lambda b, h, i: (b, i, qb + h)),
            pl.BlockSpec((1, S, LANES), lambda b, h, i: (b, 0, kb + h)),
            pl.BlockSpec((1, 1, S // tk, LANES, tk), lambda b, h, i: (b, h, 0, 0, 0)),
            pl.BlockSpec((1, tq, LANES), lambda b, h, i: (b, i, gb + h)),
        ],
        out_specs=pl.BlockSpec((1, tq, LANES), lambda b, h, i: (b, i, h)),
        out_shape=jax.ShapeDtypeStruct((B, S, A_WIDTH), BF16),
        scratch_shapes=[pltpu.VMEM((2 * tq, LANES), BF16), score, score, stat, stat, stat, stat,
                        pltpu.VMEM((LANES, 2 * tq), F32)],
        compiler_params=_cparams(("parallel", "parallel", "arbitrary")),
        name="diff_attention",
    )(lam_rows, subln_g.reshape(LANES, 1), u, u, vat, u)


def _swa_kernel(sink_ref, q_ref, kc_ref, kp_ref, vc_ref, vp_ref, g0_ref, g1_ref, o_ref, *, t):
    i = pl.program_id(1)
    g_refs = (g0_ref, g1_ref)
    heads_per_g = B_HEADS // len(g_refs)
    kk = jnp.concatenate([kp_ref[0], kc_ref[0]], axis=0)
    vv = jnp.concatenate([vp_ref[0], vc_ref[0]], axis=0)
    row = lax.broadcasted_iota(jnp.int32, (t, WINDOW + t), 0)
    col = lax.broadcasted_iota(jnp.int32, (t, WINDOW + t), 1)
    mask = (col > row) & (col <= row + WINDOW) & jnp.logical_or(i > 0, col >= WINDOW)
    group = B_HEADS // B_KV_HEADS
    for h in range(B_HEADS):
        kv = h // group
        qh = q_ref[0, :, h * HEAD_DIM:(h + 1) * HEAD_DIM]
        kh = kk[:, kv * HEAD_DIM:(kv + 1) * HEAD_DIM]
        vh = vv[:, kv * HEAD_DIM:(kv + 1) * HEAD_DIM]
        s = lax.dot_general(qh, kh, (((1,), (1,)), ((), ())), preferred_element_type=F32)
        s = jnp.where(mask, s, NEG_BIG)
        sink = sink_ref[h]
        m = jnp.maximum(jnp.max(s, axis=1, keepdims=True), sink)
        e = jnp.exp(s - m)
        den = jnp.sum(e, axis=1, keepdims=True) + jnp.exp(sink - m)
        o = jnp.dot(e.astype(BF16), vh, preferred_element_type=F32) / den
        gc = (h % heads_per_g) * HEAD_DIM
        gate = g_refs[h // heads_per_g][0, :, gc:gc + HEAD_DIM].astype(F32)
        o_ref[0, :, h * HEAD_DIM:(h + 1) * HEAD_DIM] = (o * gate).astype(BF16)


def _swa_attention(u, sinks, t):
    B, S, _ = u.shape
    per = t // WINDOW
    gw = B_WIDTH // 2
    qb, kb, vb, gb = OFF_QB // B_WIDTH, OFF_KB // LANES, OFF_VB // LANES, OFF_GB // gw
    assert OFF_QB % B_WIDTH == 0 and OFF_GB % gw == 0
    prev = lambda b, i, cb: (b, jnp.maximum(i * per - 1, 0), cb)
    return pl.pallas_call(
        functools.partial(_swa_kernel, t=t),
        grid=(B, S // t),
        in_specs=[
            pl.BlockSpec(memory_space=pltpu.SMEM),
            pl.BlockSpec((1, t, B_WIDTH), lambda b, i: (b, i, qb)),
            pl.BlockSpec((1, t, LANES), lambda b, i: (b, i, kb)),
            pl.BlockSpec((1, WINDOW, LANES), lambda b, i: prev(b, i, kb)),
            pl.BlockSpec((1, t, LANES), lambda b, i: (b, i, vb)),
            pl.BlockSpec((1, WINDOW, LANES), lambda b, i: prev(b, i, vb)),
            pl.BlockSpec((1, t, gw), lambda b, i: (b, i, gb)),
            pl.BlockSpec((1, t, gw), lambda b, i: (b, i, gb + 1)),
        ],
        out_specs=pl.BlockSpec((1, t, B_WIDTH), lambda b, i: (b, i, 0)),
        out_shape=jax.ShapeDtypeStruct((B, S, B_WIDTH), BF16),
        compiler_params=_cparams(("parallel", "parallel")),
        name="swa_attention",
    )(sinks.astype(F32), u, u, u, u, u, u, u)


def _stick_kernel(tri_ref, q_ref, k_ref, vt_ref, g_ref, o_ref, qs_scr, run_scr, acc_scr, *, t, pairs):
    i = pl.program_id(2)
    heads = 2 * pairs
    width = heads * t
    lane = lax.broadcasted_iota(jnp.int32, (t, LANES), 1)
    for p in range(pairs):
        q = q_ref[0, :, p * LANES:(p + 1) * LANES]
        zero = jnp.zeros_like(q)
        qs_scr[(2 * p) * t:(2 * p + 1) * t] = jnp.where(lane < HEAD_DIM, q, zero)
        qs_scr[(2 * p + 1) * t:(2 * p + 2) * t] = jnp.where(lane >= HEAD_DIM, q, zero)
    tri = tri_ref[...]
    run_scr[...] = jnp.zeros((1, width), F32)
    acc_scr[...] = jnp.zeros((LANES, width), F32)

    def block(j, diagonal):
        off = pl.multiple_of(j * t, t)
        z = jnp.concatenate([
            lax.dot_general(k_ref[0, pl.ds(off, t), p * LANES:(p + 1) * LANES],
                            qs_scr[2 * p * t:(2 * p + 2) * t],
                            (((1,), (1,)), ((), ())), preferred_element_type=F32)
            for p in range(pairs)], axis=1)
        sp = jnp.maximum(z, 0.0) + jnp.log(1.0 + jnp.exp2(_neg_abs(z))) * LOG2E
        if diagonal:
            key = lax.broadcasted_iota(jnp.int32, (t, width), 0)
            qry = lax.broadcasted_iota(jnp.int32, (t, width), 1) & (t - 1)
            strict = key < qry
            sp = jnp.where(strict, sp, 0.0)
        hi = _truncate_to_bf16(sp)
        lo = sp - hi
        csum = (jnp.dot(tri, hi.astype(BF16), preferred_element_type=F32)
                + jnp.dot(tri, lo.astype(BF16), preferred_element_type=F32))
        run = run_scr[...]
        w = jnp.exp2(z - csum - run)
        if diagonal:
            w = jnp.where(strict, w, 0.0)
        w = w.astype(BF16)
        for p in range(pairs):
            cols = slice(2 * p * t, (2 * p + 2) * t)
            acc_scr[:, cols] += jnp.dot(vt_ref[0, p, j], w[:, cols], preferred_element_type=F32)
        run = run + csum[0:1, :]
        run_scr[...] = run
        return jnp.min(run) < EXP2_UNDERFLOW

    def cond(c):
        jj, go = c
        return jnp.logical_and(jj < i, go)

    def body(c):
        jj, _ = c
        return jj + 1, block(i - 1 - jj, False)

    lax.while_loop(cond, body, (jnp.int32(0), block(i, True)))
    first = lax.broadcasted_iota(jnp.int32, (LANES, t), 0) < HEAD_DIM
    for p in range(pairs):
        a = 2 * p * t
        ot = jnp.where(first, acc_scr[:, a:a + t], acc_scr[:, a + t:a + 2 * t])
        gate = g_ref[0, :, p * LANES:(p + 1) * LANES].astype(F32)
        o_ref[0, :, p * LANES:(p + 1) * LANES] = (ot.T * gate).astype(BF16)


def _stick_attention(u, vt, pairs):
    B, S, _ = u.shape
    t = KV_BLOCK
    width = pairs * LANES
    groups = C_WIDTH // width
    row = jnp.arange(t)
    tri = (row[None, :] >= row[:, None]).astype(BF16)
    return pl.pallas_call(
        functools.partial(_stick_kernel, t=t, pairs=pairs),
        grid=(B, groups, S // t),
        in_specs=[
            pl.BlockSpec((t, t), lambda b, h, i: (0, 0)),
            pl.BlockSpec((1, t, width), lambda b, h, i: (b, i, h)),
            pl.BlockSpec((1, S, width), lambda b, h, i: (b, 0, groups + h)),
            pl.BlockSpec((1, pairs, S // t, LANES, t), lambda b, h, i: (b, h, 0, 0, 0)),
            pl.BlockSpec((1, t, width), lambda b, h, i: (b, i, 2 * groups + h)),
        ],
        out_specs=pl.BlockSpec((1, t, width), lambda b, h, i: (b, i, h)),
        out_shape=jax.ShapeDtypeStruct((B, S, C_WIDTH), BF16),
        scratch_shapes=[pltpu.VMEM((2 * pairs * t, LANES), BF16),
                        pltpu.VMEM((1, 2 * pairs * t), F32),
                        pltpu.VMEM((LANES, 2 * pairs * t), F32)],
        compiler_params=_cparams(("parallel", "parallel", "arbitrary")),
        name="stick_attention",
    )(tri, u, u, vt, u)


def _tile(n, pref):
    t = min(pref, n)
    assert n % t == 0
    return t


def _even_layer(x, c, positions, layer, norm_g, w_mod, b_mod, w_in, a_q_gain, a_k_gain,
                lq1, lk1, lq2, lk2, a_subln_g, b_q_gain, b_k_gain, b_sinks, w_out):
    B, S, D = x.shape
    mod = _modulation(c, w_mod, b_mod)
    ones = lambda n: jnp.ones((n,), F32)
    gain_cols = jnp.concatenate([
        jnp.tile(a_q_gain * (QK_SCALE * LOG2E), A_QK // HEAD_DIM),
        jnp.tile(a_k_gain, A_QK // HEAD_DIM),
        ones(A_WIDTH),
        jnp.tile(b_q_gain * QK_SCALE, B_HEADS), jnp.tile(b_k_gain, B_KV_HEADS),
        ones(B_KV + B_WIDTH),
    ]).astype(F32).reshape(1, EVEN_IN)
    u, vat = _inproj_even(x, mod, norm_g, w_in, positions, gain_cols, _tile(S, 512))
    lambda_init = 0.8 - 0.6 * math.exp(-0.3 * layer)
    lam_rows = jnp.stack([lq1, lk1, lq2, lk2]).astype(F32)
    ya = _diff_attention(u, vat, lam_rows, a_subln_g.astype(F32), lambda_init, _tile(S, 512))
    yb = _swa_attention(u, b_sinks, _tile(S, 256))
    return _outproj([ya, yb], w_out, x, mod, _tile(S, 512))


def _odd_layer(x, c, norm_g, w_mod, b_mod, w_in, w_out):
    B, S, D = x.shape
    mod = _modulation(c, w_mod, b_mod)
    u, vt = _inproj_odd(x, mod, norm_g, w_in, _tile(S, 512))
    y = _stick_attention(u, vt, pairs=2)
    return _outproj([y], w_out, x, mod, _tile(S, 512))


def kernel(x, c, positions, even_norm_g, even_w_mod, even_b_mod, even_w_in, a_q_gain, a_k_gain,
           a_lambda_q1, a_lambda_k1, a_lambda_q2, a_lambda_k2, a_subln_g, b_q_gain, b_k_gain,
           b_sinks, even_w_out, odd_norm_g, odd_w_mod, odd_b_mod, odd_w_in, odd_w_out):
    depth = even_norm_g.shape[0] + odd_norm_g.shape[0]
    for layer in range(depth):
        j = layer // 2
        if layer % 2 == 0:
            x = _even_layer(x, c, positions, layer, even_norm_g[j], even_w_mod[j], even_b_mod[j],
                            even_w_in[j], a_q_gain[j], a_k_gain[j], a_lambda_q1[j], a_lambda_k1[j],
                            a_lambda_q2[j], a_lambda_k2[j], a_subln_g[j], b_q_gain[j], b_k_gain[j],
                            b_sinks[j], even_w_out[j])
        else:
            x = _odd_layer(x, c, odd_norm_g[j], odd_w_mod[j], odd_b_mod[j], odd_w_in[j], odd_w_out[j])
    return x
```

```python
import functools
import math

import jax
import jax.numpy as jnp
from jax import lax
from jax.experimental import pallas as pl
from jax.experimental.pallas import tpu as pltpu

F32 = jnp.float32
BF16 = jnp.bfloat16

HEAD_DIM = 64
ROPE_THETA = 10000.0
EPS = 1e-6
WINDOW = 128
LANES = 128
QK_SCALE = HEAD_DIM ** -0.5

A_HEADS = 4
A_QK = A_HEADS * 2 * HEAD_DIM
A_WIDTH = A_HEADS * 2 * HEAD_DIM
B_HEADS = 8
B_KV_HEADS = 2
B_WIDTH = B_HEADS * HEAD_DIM
B_KV = B_KV_HEADS * HEAD_DIM
C_HEADS = 16
C_WIDTH = C_HEADS * HEAD_DIM

W_QA = 0
W_KA = W_QA + A_QK
W_VA = W_KA + A_QK
W_GA = W_VA + A_WIDTH
W_QB = W_GA + A_WIDTH
W_END = W_QB + B_WIDTH + 2 * B_KV + B_WIDTH

OFF_QA = 0
OFF_KA = OFF_QA + A_QK
OFF_GA = OFF_KA + A_QK
OFF_QB = OFF_GA + A_WIDTH
OFF_KB = OFF_QB + B_WIDTH
OFF_VB = OFF_KB + B_KV
OFF_GB = OFF_VB + B_KV
EVEN_IN = OFF_GB + B_WIDTH

EVEN_CHUNKS = (
    (OFF_QA, 256, "qk"), (OFF_QA + 256, 256, "qk"),
    (OFF_KA, 256, "qk"), (OFF_KA + 256, 256, "qk"),
    (OFF_GA, 512, "silu"),
    (OFF_QB, 256, "qk"), (OFF_QB + 256, 256, "qk"),
    (OFF_KB, 128, "qk"), (OFF_VB, 128, "plain"), (OFF_GB, 512, "silu"),
)
LOG2E = 1.4426950408889634
KV_BLOCK = 256
ODD_IN = 3 * C_WIDTH
ODD_CHUNKS = tuple(
    (kind_i * C_WIDTH + half * 512, 512, kind)
    for kind_i, kind in enumerate(("scale", "plain", "silu"))
    for half in range(2)
)

NEG_BIG = -1e30
EXP2_UNDERFLOW = 151.0
LOGIT_BOUND = 64.0
LOGIT_BOUND_MARGIN = 1.01
VMEM_LIMIT = 48 * 1024 * 1024


def _cparams(sem):
    return pltpu.CompilerParams(dimension_semantics=sem, vmem_limit_bytes=VMEM_LIMIT)


def _mod_kernel(c_ref, w_ref, b_ref, o_ref):
    c = c_ref[...]
    sc = c * jax.nn.sigmoid(c)
    o_ref[0] = jnp.dot(sc, w_ref[...], preferred_element_type=F32,
                       precision=lax.Precision.HIGHEST) + b_ref[0]


def _modulation(c, w_mod, b_mod):
    B, D = c.shape
    rows = 8
    cp = jnp.pad(c, ((0, rows - B), (0, 0)))
    out = pl.pallas_call(
        _mod_kernel,
        grid=(3,),
        in_specs=[
            pl.BlockSpec((rows, D), lambda j: (0, 0)),
            pl.BlockSpec((D, D), lambda j: (0, j)),
            pl.BlockSpec((1, 1, D), lambda j: (j, 0, 0)),
        ],
        out_specs=pl.BlockSpec((1, rows, D), lambda j: (j, 0, 0)),
        out_shape=jax.ShapeDtypeStruct((3, rows, D), F32),
        compiler_params=_cparams(("arbitrary",)),
        name="modulation",
    )(cp, w_mod, b_mod.reshape(3, 1, D))
    return out[:, :B].reshape(3, B, 1, D)


def _modulated_rows(x_ref, mod_ref, ng_ref):
    x = x_ref[0]
    ms = jnp.mean(x * x, axis=-1, keepdims=True)
    y = x * lax.rsqrt(ms + EPS) * ng_ref[...]
    return y * (1.0 + mod_ref[1, 0]) + mod_ref[0, 0]


def _store_transposed_values(wvt_ref, h_scr, vt_ref, tm):
    vt = lax.dot_general(wvt_ref[...], h_scr[...], (((1,), (1,)), ((), ())),
                         preferred_element_type=F32)
    for g in range(vt.shape[0] // LANES):
        for cb in range(tm // KV_BLOCK):
            vt_ref[0, g, cb] = vt[g * LANES:(g + 1) * LANES,
                                  cb * KV_BLOCK:(cb + 1) * KV_BLOCK].astype(BF16)


def _inproj_even_kernel(x_ref, mod_ref, ng_ref, w_ref, wvt_ref, pos_ref, inv_ref, gain_ref, p_ref,
                        o_ref, vt_ref, h_scr, *, tm):
    h_scr[...] = _modulated_rows(x_ref, mod_ref, ng_ref).astype(BF16)
    _store_transposed_values(wvt_ref, h_scr, vt_ref, tm)
    ang = pos_ref[0].astype(F32) * inv_ref[...]
    cos = jnp.cos(ang)
    sin = jnp.sin(ang)
    lane = lax.broadcasted_iota(jnp.int32, (tm, LANES), 1)
    first = (lane & (HEAD_DIM // 2)) == 0
    sin_s = jnp.where(first, -sin, sin)
    for start, width, kind in EVEN_CHUNKS:
        u = jnp.dot(h_scr[...], w_ref[:, start:start + width], preferred_element_type=F32)
        if kind == "qk":
            sq = u * u
            hi = sq.astype(BF16)
            lo = (sq - hi.astype(F32)).astype(BF16)
            avg = p_ref[:width, :width]
            ms = (jnp.dot(hi, avg, preferred_element_type=F32)
                  + jnp.dot(lo, avg, preferred_element_type=F32))
            un = u * lax.rsqrt(ms + EPS) * gain_ref[:, start:start + width]
            for s in range(width // LANES):
                xs = un[:, s * LANES:(s + 1) * LANES]
                rot = jnp.where(first, pltpu.roll(xs, LANES - HEAD_DIM // 2, 1),
                                pltpu.roll(xs, HEAD_DIM // 2, 1))
                c0 = start + s * LANES
                o_ref[0, :, c0:c0 + LANES] = (xs * cos + rot * sin_s).astype(BF16)
        elif kind == "silu":
            o_ref[0, :, start:start + width] = (u * jax.nn.sigmoid(u)).astype(BF16)
        else:
            o_ref[0, :, start:start + width] = u.astype(BF16)


def _inproj_odd_kernel(x_ref, mod_ref, ng_ref, w_ref, wvt_ref, o_ref, vt_ref, h_scr, *, tm):
    h_scr[...] = _modulated_rows(x_ref, mod_ref, ng_ref).astype(BF16)
    _store_transposed_values(wvt_ref, h_scr, vt_ref, tm)
    for start, width, kind in ODD_CHUNKS:
        u = jnp.dot(h_scr[...], w_ref[:, start:start + width], preferred_element_type=F32)
        if kind == "scale":
            u = u * (QK_SCALE * LOG2E)
        elif kind == "silu":
            u = u * jax.nn.sigmoid(u)
        o_ref[0, :, start:start + width] = u.astype(BF16)


def _inproj_even(x, mod, norm_g, w_in, positions, gain_cols, tm):
    B, S, D = x.shape
    w_rows = jnp.concatenate([w_in[:, :W_VA], w_in[:, W_GA:]], axis=1).astype(BF16)
    w_vat = w_in[:, W_VA:W_GA].T.astype(BF16)
    n_groups = A_WIDTH // LANES
    half = HEAD_DIM // 2
    inv = ROPE_THETA ** (-jnp.arange(half, dtype=F32) / half)
    inv_cols = jnp.tile(inv, LANES // half).reshape(1, LANES)
    blk = jnp.arange(256) // HEAD_DIM
    avg = jnp.where(blk[:, None] == blk[None, :], 1.0 / HEAD_DIM, 0.0).astype(BF16)
    return pl.pallas_call(
        functools.partial(_inproj_even_kernel, tm=tm),
        grid=(B, S // tm),
        in_specs=[
            pl.BlockSpec((1, tm, D), lambda b, i: (b, i, 0)),
            pl.BlockSpec((3, 1, 1, D), lambda b, i: (0, b, 0, 0)),
            pl.BlockSpec((1, D), lambda b, i: (0, 0)),
            pl.BlockSpec((D, EVEN_IN), lambda b, i: (0, 0)),
            pl.BlockSpec((A_WIDTH, D), lambda b, i: (0, 0)),
            pl.BlockSpec((1, tm, 1), lambda b, i: (b, i, 0)),
            pl.BlockSpec((1, LANES), lambda b, i: (0, 0)),
            pl.BlockSpec((1, EVEN_IN), lambda b, i: (0, 0)),
            pl.BlockSpec((256, 256), lambda b, i: (0, 0)),
        ],
        out_specs=[
            pl.BlockSpec((1, tm, EVEN_IN), lambda b, i: (b, i, 0)),
            pl.BlockSpec((1, n_groups, tm // KV_BLOCK, LANES, KV_BLOCK), lambda b, i: (b, 0, i, 0, 0)),
        ],
        out_shape=[
            jax.ShapeDtypeStruct((B, S, EVEN_IN), BF16),
            jax.ShapeDtypeStruct((B, n_groups, S // KV_BLOCK, LANES, KV_BLOCK), BF16),
        ],
        scratch_shapes=[pltpu.VMEM((tm, D), BF16)],
        compiler_params=_cparams(("parallel", "parallel")),
        name="inproj_even",
    )(x, mod, norm_g.reshape(1, D), w_rows, w_vat, positions.reshape(B, S, 1),
      inv_cols, gain_cols, avg)


def _inproj_odd(x, mod, norm_g, w_in, tm):
    B, S, D = x.shape
    w_rows = jnp.concatenate([w_in[:, :2 * C_WIDTH], w_in[:, 3 * C_WIDTH:]], axis=1).astype(BF16)
    w_vt = w_in[:, 2 * C_WIDTH:3 * C_WIDTH].T.astype(BF16)
    n_groups = C_WIDTH // LANES
    return pl.pallas_call(
        functools.partial(_inproj_odd_kernel, tm=tm),
        grid=(B, S // tm),
        in_specs=[
            pl.BlockSpec((1, tm, D), lambda b, i: (b, i, 0)),
            pl.BlockSpec((3, 1, 1, D), lambda b, i: (0, b, 0, 0)),
            pl.BlockSpec((1, D), lambda b, i: (0, 0)),
            pl.BlockSpec((D, ODD_IN), lambda b, i: (0, 0)),
            pl.BlockSpec((C_WIDTH, D), lambda b, i: (0, 0)),
        ],
        out_specs=[
            pl.BlockSpec((1, tm, ODD_IN), lambda b, i: (b, i, 0)),
            pl.BlockSpec((1, n_groups, tm // KV_BLOCK, LANES, KV_BLOCK), lambda b, i: (b, 0, i, 0, 0)),
        ],
        out_shape=[
            jax.ShapeDtypeStruct((B, S, ODD_IN), BF16),
            jax.ShapeDtypeStruct((B, n_groups, S // KV_BLOCK, LANES, KV_BLOCK), BF16),
        ],
        scratch_shapes=[pltpu.VMEM((tm, D), BF16)],
        compiler_params=_cparams(("parallel", "parallel")),
        name="inproj_odd",
    )(x, mod, norm_g.reshape(1, D), w_rows, w_vt)


def _outproj_kernel(*refs, n_in):
    y_refs = refs[:n_in]
    w_ref, x_ref, mod_ref, o_ref = refs[n_in:]
    acc = None
    off = 0
    for y_ref in y_refs:
        wd = y_ref.shape[-1]
        part = jnp.dot(y_ref[0], w_ref[off:off + wd, :], preferred_element_type=F32)
        acc = part if acc is None else acc + part
        off += wd
    o_ref[0] = x_ref[0] + mod_ref[2, 0] * acc


def _outproj(ys, w_out, x, mod, tm):
    B, S, D = x.shape
    K = w_out.shape[0]
    n_in = len(ys)
    in_specs = [pl.BlockSpec((1, tm, y.shape[-1]), lambda b, i: (b, i, 0)) for y in ys]
    in_specs += [
        pl.BlockSpec((K, D), lambda b, i: (0, 0)),
        pl.BlockSpec((1, tm, D), lambda b, i: (b, i, 0)),
        pl.BlockSpec((3, 1, 1, D), lambda b, i: (0, b, 0, 0)),
    ]
    return pl.pallas_call(
        functools.partial(_outproj_kernel, n_in=n_in),
        grid=(B, S // tm),
        in_specs=in_specs,
        out_specs=pl.BlockSpec((1, tm, D), lambda b, i: (b, i, 0)),
        out_shape=jax.ShapeDtypeStruct((B, S, D), F32),
        compiler_params=_cparams(("parallel", "parallel")),
        name="outproj",
    )(*ys, w_out.astype(BF16), x, mod)


def _diff_attn_kernel(bound_ref, lam_ref, sub_ref, q_ref, k_ref, vt_ref, g_ref, o_ref,
                      qs_scr, s0_scr, s1_scr, smax0_scr, smax1_scr, m_scr, l_scr, acc_scr, *,
                      tq, tk, lambda_init):
    i = pl.program_id(2)
    q = q_ref[0]
    lane = lax.broadcasted_iota(jnp.int32, (tq, LANES), 1)
    zero = jnp.zeros_like(q)
    qs_scr[:tq] = jnp.where(lane < HEAD_DIM, q, zero)
    qs_scr[tq:] = jnp.where(lane >= HEAD_DIM, q, zero)

    slots = ((s0_scr, smax0_scr), (s1_scr, smax1_scr))

    def scores(j, slot, bounded):
        s_ref, smax_ref = slots[slot]
        k = k_ref[0, pl.ds(pl.multiple_of(j * tk, tk), tk), :]
        s = lax.dot_general(k, qs_scr[...], (((1,), (1,)), ((), ())),
                            preferred_element_type=F32)
        s_ref[...] = s
        if not bounded:
            smax_ref[...] = jnp.max(s, axis=0, keepdims=True)

    def causal(s, diag):
        key = lax.broadcasted_iota(jnp.int32, (tk, 2 * tq), 0) + diag * tk
        qry = lax.broadcasted_iota(jnp.int32, (tk, 2 * tq), 1) & (tq - 1)
        return jnp.where(key <= qry, s, NEG_BIG)

    def softmax_pv(j, slot, bounded, diag=None):
        s_ref, smax_ref = slots[slot]
        s = s_ref[...] if diag is None else causal(s_ref[...], diag)
        if bounded:
            p = jnp.exp2(s)
            l_scr[...] += jnp.sum(p, axis=0, keepdims=True)
            acc_scr[...] += jnp.dot(vt_ref[0, 0, j], p.astype(BF16), preferred_element_type=F32)
            return
        smax = smax_ref[...] if diag is None else jnp.max(s, axis=0, keepdims=True)
        m = m_scr[...]
        m_new = jnp.maximum(m, smax)
        alpha = jnp.exp2(m - m_new)
        p = jnp.exp2(s - m_new)
        l_scr[...] = alpha * l_scr[...] + jnp.sum(p, axis=0, keepdims=True)
        m_scr[...] = m_new
        pv = jnp.dot(vt_ref[0, 0, j], p.astype(BF16), preferred_element_type=F32)
        acc_scr[...] = alpha * acc_scr[...] + pv

    def sweep(bounded):
        m_scr[...] = jnp.full((1, 2 * tq), NEG_BIG, F32)
        l_scr[...] = jnp.zeros((1, 2 * tq), F32)
        acc_scr[...] = jnp.zeros((LANES, 2 * tq), F32)
        n_diag = tq // tk
        n_full = n_diag * i
        scores(0, 0, bounded)

        def body(jj, carry):
            j0 = 2 * jj
            scores(j0 + 1, 1, bounded)
            softmax_pv(j0, 0, bounded)
            scores(j0 + 2, 0, bounded)
            softmax_pv(j0 + 1, 1, bounded)
            return carry

        lax.fori_loop(0, (n_diag // 2) * i, body, 0)
        for d in range(n_diag):
            if d + 1 < n_diag:
                scores(n_full + d + 1, (d + 1) % 2, bounded)
            softmax_pv(n_full + d, d % 2, bounded, diag=d)

    is_bounded = bound_ref[0] <= LOGIT_BOUND
    pl.when(is_bounded)(lambda: sweep(True))
    pl.when(jnp.logical_not(is_bounded))(lambda: sweep(False))
    l = l_scr[...]
    acc = acc_scr[...]

    lam_rows = lam_ref[...]
    lam = (jnp.exp(jnp.sum(lam_rows[0:1] * lam_rows[1:2], axis=1, keepdims=True))
           - jnp.exp(jnp.sum(lam_rows[2:3] * lam_rows[3:4], axis=1, keepdims=True))
           + lambda_init)
    ot = acc[:, :tq] / l[:, :tq] - lam * (acc[:, tq:] / l[:, tq:])
    ms = jnp.mean(ot * ot, axis=0, keepdims=True)
    yt = ot * lax.rsqrt(ms + EPS) * (sub_ref[...] * (1.0 - lambda_init))
    o_ref[0] = (yt.T * g_ref[0].astype(F32)).astype(BF16)


def _diff_attention(u, vat, logit_bound, lam_rows, subln_g, lambda_init, tq):
    B, S, _ = u.shape
    tk = KV_BLOCK
    assert tq % (2 * tk) == 0 and S % tq == 0
    qb, kb, gb = OFF_QA // LANES, OFF_KA // LANES, OFF_GA // LANES
    stat = pltpu.VMEM((1, 2 * tq), F32)
    score = pltpu.VMEM((tk, 2 * tq), F32)
    return pl.pallas_call(
        functools.partial(_diff_attn_kernel, tq=tq, tk=tk, lambda_init=lambda_init),
        grid=(B, A_HEADS, S // tq),
        in_specs=[
            pl.BlockSpec(memory_space=pltpu.SMEM),
            pl.BlockSpec((4, HEAD_DIM), lambda b, h, i: (0, 0)),
            pl.BlockSpec((LANES, 1), lambda b, h, i: (0, 0)),
            pl.BlockSpec((1, tq, LANES), lambda b, h, i: (b, i, qb + h)),
            pl.BlockSpec((1, S, LANES), lambda b, h, i: (b, 0, kb + h)),
            pl.BlockSpec((1, 1, S // tk, LANES, tk), lambda b, h, i: (b, h, 0, 0, 0)),
            pl.BlockSpec((1, tq, LANES), lambda b, h, i: (b, i, gb + h)),
        ],
        out_specs=pl.BlockSpec((1, tq, LANES), lambda b, h, i: (b, i, h)),
        out_shape=jax.ShapeDtypeStruct((B, S, A_WIDTH), BF16),
        scratch_shapes=[pltpu.VMEM((2 * tq, LANES), BF16), score, score, stat, stat, stat, stat,
                        pltpu.VMEM((LANES, 2 * tq), F32)],
        compiler_params=_cparams(("parallel", "parallel", "arbitrary")),
        name="diff_attention",
    )(logit_bound.reshape(1).astype(F32), lam_rows, subln_g.reshape(LANES, 1), u, u, vat, u)


def _swa_kernel(sink_ref, q_ref, kc_ref, kp_ref, vc_ref, vp_ref, g0_ref, g1_ref, o_ref, *, t):
    i = pl.program_id(1)
    g_refs = (g0_ref, g1_ref)
    heads_per_g = B_HEADS // len(g_refs)
    kk = jnp.concatenate([kp_ref[0], kc_ref[0]], axis=0)
    vv = jnp.concatenate([vp_ref[0], vc_ref[0]], axis=0)
    row = lax.broadcasted_iota(jnp.int32, (t, WINDOW + t), 0)
    col = lax.broadcasted_iota(jnp.int32, (t, WINDOW + t), 1)
    mask = (col > row) & (col <= row + WINDOW) & jnp.logical_or(i > 0, col >= WINDOW)
    group = B_HEADS // B_KV_HEADS
    for h in range(B_HEADS):
        kv = h // group
        qh = q_ref[0, :, h * HEAD_DIM:(h + 1) * HEAD_DIM]
        kh = kk[:, kv * HEAD_DIM:(kv + 1) * HEAD_DIM]
        vh = vv[:, kv * HEAD_DIM:(kv + 1) * HEAD_DIM]
        s = lax.dot_general(qh, kh, (((1,), (1,)), ((), ())), preferred_element_type=F32)
        s = jnp.where(mask, s, NEG_BIG)
        sink = sink_ref[h]
        m = jnp.maximum(jnp.max(s, axis=1, keepdims=True), sink)
        e = jnp.exp(s - m)
        den = jnp.sum(e, axis=1, keepdims=True) + jnp.exp(sink - m)
        o = jnp.dot(e.astype(BF16), vh, preferred_element_type=F32) / den
        gc = (h % heads_per_g) * HEAD_DIM
        gate = g_refs[h // heads_per_g][0, :, gc:gc + HEAD_DIM].astype(F32)
        o_ref[0, :, h * HEAD_DIM:(h + 1) * HEAD_DIM] = (o * gate).astype(BF16)


def _swa_attention(u, sinks, t):
    B, S, _ = u.shape
    per = t // WINDOW
    gw = B_WIDTH // 2
    qb, kb, vb, gb = OFF_QB // B_WIDTH, OFF_KB // LANES, OFF_VB // LANES, OFF_GB // gw
    assert OFF_QB % B_WIDTH == 0 and OFF_GB % gw == 0
    prev = lambda b, i, cb: (b, jnp.maximum(i * per - 1, 0), cb)
    return pl.pallas_call(
        functools.partial(_swa_kernel, t=t),
        grid=(B, S // t),
        in_specs=[
            pl.BlockSpec(memory_space=pltpu.SMEM),
            pl.BlockSpec((1, t, B_WIDTH), lambda b, i: (b, i, qb)),
            pl.BlockSpec((1, t, LANES), lambda b, i: (b, i, kb)),
            pl.BlockSpec((1, WINDOW, LANES), lambda b, i: prev(b, i, kb)),
            pl.BlockSpec((1, t, LANES), lambda b, i: (b, i, vb)),
            pl.BlockSpec((1, WINDOW, LANES), lambda b, i: prev(b, i, vb)),
            pl.BlockSpec((1, t, gw), lambda b, i: (b, i, gb)),
            pl.BlockSpec((1, t, gw), lambda b, i: (b, i, gb + 1)),
        ],
        out_specs=pl.BlockSpec((1, t, B_WIDTH), lambda b, i: (b, i, 0)),
        out_shape=jax.ShapeDtypeStruct((B, S, B_WIDTH), BF16),
        compiler_params=_cparams(("parallel", "parallel")),
        name="swa_attention",
    )(sinks.astype(F32), u, u, u, u, u, u, u)


def _stick_kernel(tri_ref, q_ref, k_ref, vt_ref, g_ref, o_ref, qs_scr, run_scr, acc_scr, *, t, pairs):
    i = pl.program_id(2)
    heads = 2 * pairs
    width = heads * t
    lane = lax.broadcasted_iota(jnp.int32, (t, LANES), 1)
    for p in range(pairs):
        q = q_ref[0, :, p * LANES:(p + 1) * LANES]
        zero = jnp.zeros_like(q)
        qs_scr[(2 * p) * t:(2 * p + 1) * t] = jnp.where(lane < HEAD_DIM, q, zero)
        qs_scr[(2 * p + 1) * t:(2 * p + 2) * t] = jnp.where(lane >= HEAD_DIM, q, zero)
    tri = tri_ref[...]
    run_scr[...] = jnp.zeros((1, width), F32)
    acc_scr[...] = jnp.zeros((LANES, width), F32)

    def block(j, diagonal):
        off = pl.multiple_of(j * t, t)
        z = jnp.concatenate([
            lax.dot_general(k_ref[0, pl.ds(off, t), p * LANES:(p + 1) * LANES],
                            qs_scr[2 * p * t:(2 * p + 2) * t],
                            (((1,), (1,)), ((), ())), preferred_element_type=F32)
            for p in range(pairs)], axis=1)
        sp = jnp.maximum(z, 0.0) + jnp.log(1.0 + jnp.exp2(-jnp.abs(z))) * LOG2E
        if diagonal:
            key = lax.broadcasted_iota(jnp.int32, (t, width), 0)
            qry = lax.broadcasted_iota(jnp.int32, (t, width), 1) & (t - 1)
            strict = key < qry
            sp = jnp.where(strict, sp, 0.0)
        hi = sp.astype(BF16)
        lo = (sp - hi.astype(F32)).astype(BF16)
        csum = (jnp.dot(tri, hi, preferred_element_type=F32)
                + jnp.dot(tri, lo, preferred_element_type=F32))
        run = run_scr[...]
        w = jnp.exp2(z - csum - run)
        if diagonal:
            w = jnp.where(strict, w, 0.0)
        w = w.astype(BF16)
        for p in range(pairs):
            cols = slice(2 * p * t, (2 * p + 2) * t)
            acc_scr[:, cols] += jnp.dot(vt_ref[0, p, j], w[:, cols], preferred_element_type=F32)
        run = run + csum[0:1, :]
        run_scr[...] = run
        return jnp.min(run) < EXP2_UNDERFLOW

    def cond(c):
        jj, go = c
        return jnp.logical_and(jj < i, go)

    def body(c):
        jj, _ = c
        return jj + 1, block(i - 1 - jj, False)

    lax.while_loop(cond, body, (jnp.int32(0), block(i, True)))
    first = lax.broadcasted_iota(jnp.int32, (LANES, t), 0) < HEAD_DIM
    for p in range(pairs):
        a = 2 * p * t
        ot = jnp.where(first, acc_scr[:, a:a + t], acc_scr[:, a + t:a + 2 * t])
        gate = g_ref[0, :, p * LANES:(p + 1) * LANES].astype(F32)
        o_ref[0, :, p * LANES:(p + 1) * LANES] = (ot.T * gate).astype(BF16)


def _stick_attention(u, vt, pairs):
    B, S, _ = u.shape
    t = KV_BLOCK
    width = pairs * LANES
    groups = C_WIDTH // width
    row = jnp.arange(t)
    tri = (row[None, :] >= row[:, None]).astype(BF16)
    return pl.pallas_call(
        functools.partial(_stick_kernel, t=t, pairs=pairs),
        grid=(B, groups, S // t),
        in_specs=[
            pl.BlockSpec((t, t), lambda b, h, i: (0, 0)),
            pl.BlockSpec((1, t, width), lambda b, h, i: (b, i, h)),
            pl.BlockSpec((1, S, width), lambda b, h, i: (b, 0, groups + h)),
            pl.BlockSpec((1, pairs, S // t, LANES, t), lambda b, h, i: (b, h, 0, 0, 0)),
            pl.BlockSpec((1, t, width), lambda b, h, i: (b, i, 2 * groups + h)),
        ],
        out_specs=pl.BlockSpec((1, t, width), lambda b, h, i: (b, i, h)),
        out_shape=jax.ShapeDtypeStruct((B, S, C_WIDTH), BF16),
        scratch_shapes=[pltpu.VMEM((2 * pairs * t, LANES), BF16),
                        pltpu.VMEM((1, 2 * pairs * t), F32),
                        pltpu.VMEM((LANES, 2 * pairs * t), F32)],
        compiler_params=_cparams(("parallel", "parallel", "arbitrary")),
        name="stick_attention",
    )(tri, u, u, vt, u)


def _tile(n, pref):
    t = min(pref, n)
    assert n % t == 0
    return t


def _even_layer(x, c, positions, layer, norm_g, w_mod, b_mod, w_in, a_q_gain, a_k_gain,
                lq1, lk1, lq2, lk2, a_subln_g, b_q_gain, b_k_gain, b_sinks, w_out):
    B, S, D = x.shape
    mod = _modulation(c, w_mod, b_mod)
    ones = lambda n: jnp.ones((n,), F32)
    gain_cols = jnp.concatenate([
        jnp.tile(a_q_gain * (QK_SCALE * LOG2E), A_QK // HEAD_DIM),
        jnp.tile(a_k_gain, A_QK // HEAD_DIM),
        ones(A_WIDTH),
        jnp.tile(b_q_gain * QK_SCALE, B_HEADS), jnp.tile(b_k_gain, B_KV_HEADS),
        ones(B_KV + B_WIDTH),
    ]).astype(F32).reshape(1, EVEN_IN)
    u, vat = _inproj_even(x, mod, norm_g, w_in, positions, gain_cols, _tile(S, 512))
    lambda_init = 0.8 - 0.6 * math.exp(-0.3 * layer)
    lam_rows = jnp.stack([lq1, lk1, lq2, lk2]).astype(F32)
    logit_bound = (HEAD_DIM * QK_SCALE * LOG2E * LOGIT_BOUND_MARGIN
                   * jnp.max(jnp.abs(a_q_gain)) * jnp.max(jnp.abs(a_k_gain)))
    ya = _diff_attention(u, vat, logit_bound, lam_rows, a_subln_g.astype(F32), lambda_init,
                         _tile(S, 1024))
    yb = _swa_attention(u, b_sinks, _tile(S, 256))
    return _outproj([ya, yb], w_out, x, mod, _tile(S, 512))


def _odd_layer(x, c, norm_g, w_mod, b_mod, w_in, w_out):
    B, S, D = x.shape
    mod = _modulation(c, w_mod, b_mod)
    u, vt = _inproj_odd(x, mod, norm_g, w_in, _tile(S, 512))
    y = _stick_attention(u, vt, pairs=2)
    return _outproj([y], w_out, x, mod, _tile(S, 512))


def kernel(x, c, positions, even_norm_g, even_w_mod, even_b_mod, even_w_in, a_q_gain, a_k_gain,
           a_lambda_q1, a_lambda_k1, a_lambda_q2, a_lambda_k2, a_subln_g, b_q_gain, b_k_gain,
           b_sinks, even_w_out, odd_norm_g, odd_w_mod, odd_b_mod, odd_w_in, odd_w_out):
    depth = even_norm_g.shape[0] + odd_norm_g.shape[0]
    for layer in range(depth):
        j = layer // 2
        if layer % 2 == 0:
            x = _even_layer(x, c, positions, layer, even_norm_g[j], even_w_mod[j], even_b_mod[j],
                            even_w_in[j], a_q_gain[j], a_k_gain[j], a_lambda_q1[j], a_lambda_k1[j],
                            a_lambda_q2[j], a_lambda_k2[j], a_subln_g[j], b_q_gain[j], b_k_gain[j],
                            b_sinks[j], even_w_out[j])
        else:
            x = _odd_layer(x, c, odd_norm_g[j], odd_w_mod[j], odd_b_mod[j], odd_w_in[j], odd_w_out[j])
    return x
```

```python
import functools
import math

import jax
import jax.numpy as jnp
from jax import lax
from jax.experimental import pallas as pl
from jax.experimental.pallas import tpu as pltpu

F32 = jnp.float32
BF16 = jnp.bfloat16

HEAD_DIM = 64
ROPE_THETA = 10000.0
EPS = 1e-6
WINDOW = 128
LANES = 128
QK_SCALE = HEAD_DIM ** -0.5

A_HEADS = 4
A_QK = A_HEADS * 2 * HEAD_DIM
A_WIDTH = A_HEADS * 2 * HEAD_DIM
B_HEADS = 8
B_KV_HEADS = 2
B_WIDTH = B_HEADS * HEAD_DIM
B_KV = B_KV_HEADS * HEAD_DIM
C_HEADS = 16
C_WIDTH = C_HEADS * HEAD_DIM

W_QA = 0
W_KA = W_QA + A_QK
W_VA = W_KA + A_QK
W_GA = W_VA + A_WIDTH
W_QB = W_GA + A_WIDTH
W_END = W_QB + B_WIDTH + 2 * B_KV + B_WIDTH

OFF_QA = 0
OFF_KA = OFF_QA + A_QK
OFF_GA = OFF_KA + A_QK
OFF_QB = OFF_GA + A_WIDTH
OFF_KB = OFF_QB + B_WIDTH
OFF_VB = OFF_KB + B_KV
OFF_GB = OFF_VB + B_KV
EVEN_IN = OFF_GB + B_WIDTH

EVEN_CHUNKS = (
    (OFF_QA, 256, "qk"), (OFF_QA + 256, 256, "qk"),
    (OFF_KA, 256, "qk"), (OFF_KA + 256, 256, "qk"),
    (OFF_GA, 512, "silu"),
    (OFF_QB, 256, "qk"), (OFF_QB + 256, 256, "qk"),
    (OFF_KB, 128, "qk"), (OFF_VB, 128, "plain"), (OFF_GB, 512, "silu"),
)
LOG2E = 1.4426950408889634
KV_BLOCK = 256
ODD_IN = 3 * C_WIDTH
ODD_CHUNKS = tuple(
    (kind_i * C_WIDTH + half * 512, 512, kind)
    for kind_i, kind in enumerate(("scale", "plain", "silu"))
    for half in range(2)
)

NEG_BIG = -1e30
EXP2_UNDERFLOW = 151.0
LOGIT_BOUND = 64.0
LOGIT_BOUND_MARGIN = 1.01
VMEM_LIMIT = 48 * 1024 * 1024


def _cparams(sem):
    return pltpu.CompilerParams(dimension_semantics=sem, vmem_limit_bytes=VMEM_LIMIT)


def _mod_kernel(c_ref, w_ref, b_ref, o_ref):
    c = c_ref[...]
    sc = c * jax.nn.sigmoid(c)
    o_ref[0] = jnp.dot(sc, w_ref[...], preferred_element_type=F32,
                       precision=lax.Precision.HIGHEST) + b_ref[0]


def _modulation(c, w_mod, b_mod):
    B, D = c.shape
    rows = 8
    cp = jnp.pad(c, ((0, rows - B), (0, 0)))
    out = pl.pallas_call(
        _mod_kernel,
        grid=(3,),
        in_specs=[
            pl.BlockSpec((rows, D), lambda j: (0, 0)),
            pl.BlockSpec((D, D), lambda j: (0, j)),
            pl.BlockSpec((1, 1, D), lambda j: (j, 0, 0)),
        ],
        out_specs=pl.BlockSpec((1, rows, D), lambda j: (j, 0, 0)),
        out_shape=jax.ShapeDtypeStruct((3, rows, D), F32),
        compiler_params=_cparams(("arbitrary",)),
        name="modulation",
    )(cp, w_mod, b_mod.reshape(3, 1, D))
    return out[:, :B].reshape(3, B, 1, D)


def _modulated_rows(x_ref, mod_ref, ng_ref):
    x = x_ref[0]
    ms = jnp.mean(x * x, axis=-1, keepdims=True)
    y = x * lax.rsqrt(ms + EPS) * ng_ref[...]
    return y * (1.0 + mod_ref[1, 0]) + mod_ref[0, 0]


def _store_transposed_values(wvt_ref, h_scr, vt_ref, tm):
    vt = lax.dot_general(wvt_ref[...], h_scr[...], (((1,), (1,)), ((), ())),
                         preferred_element_type=F32)
    for g in range(vt.shape[0] // LANES):
        for cb in range(tm // KV_BLOCK):
            vt_ref[0, g, cb] = vt[g * LANES:(g + 1) * LANES,
                                  cb * KV_BLOCK:(cb + 1) * KV_BLOCK].astype(BF16)


def _inproj_even_kernel(x_ref, mod_ref, ng_ref, w_ref, wvt_ref, pos_ref, inv_ref, gain_ref, p_ref,
                        o_ref, vt_ref, h_scr, *, tm):
    h_scr[...] = _modulated_rows(x_ref, mod_ref, ng_ref).astype(BF16)
    _store_transposed_values(wvt_ref, h_scr, vt_ref, tm)
    ang = pos_ref[0].astype(F32) * inv_ref[...]
    cos = jnp.cos(ang)
    sin = jnp.sin(ang)
    lane = lax.broadcasted_iota(jnp.int32, (tm, LANES), 1)
    first = (lane & (HEAD_DIM // 2)) == 0
    sin_s = jnp.where(first, -sin, sin)
    for start, width, kind in EVEN_CHUNKS:
        u = jnp.dot(h_scr[...], w_ref[:, start:start + width], preferred_element_type=F32)
        if kind == "qk":
            ms = jnp.dot((u * u).astype(BF16), p_ref[:width, :width], preferred_element_type=F32)
            un = u * lax.rsqrt(ms + EPS) * gain_ref[:, start:start + width]
            for s in range(width // LANES):
                xs = un[:, s * LANES:(s + 1) * LANES]
                rot = jnp.where(first, pltpu.roll(xs, LANES - HEAD_DIM // 2, 1),
                                pltpu.roll(xs, HEAD_DIM // 2, 1))
                c0 = start + s * LANES
                o_ref[0, :, c0:c0 + LANES] = (xs * cos + rot * sin_s).astype(BF16)
        elif kind == "silu":
            o_ref[0, :, start:start + width] = (u * jax.nn.sigmoid(u)).astype(BF16)
        else:
            o_ref[0, :, start:start + width] = u.astype(BF16)


def _inproj_odd_kernel(x_ref, mod_ref, ng_ref, w_ref, wvt_ref, o_ref, vt_ref, h_scr, *, tm):
    h_scr[...] = _modulated_rows(x_ref, mod_ref, ng_ref).astype(BF16)
    _store_transposed_values(wvt_ref, h_scr, vt_ref, tm)
    for start, width, kind in ODD_CHUNKS:
        u = jnp.dot(h_scr[...], w_ref[:, start:start + width], preferred_element_type=F32)
        if kind == "scale":
            u = u * (QK_SCALE * LOG2E)
        elif kind == "silu":
            u = u * jax.nn.sigmoid(u)
        o_ref[0, :, start:start + width] = u.astype(BF16)


def _inproj_even(x, mod, norm_g, w_in, positions, gain_cols, tm):
    B, S, D = x.shape
    w_rows = jnp.concatenate([w_in[:, :W_VA], w_in[:, W_GA:]], axis=1).astype(BF16)
    w_vat = w_in[:, W_VA:W_GA].T.astype(BF16)
    n_groups = A_WIDTH // LANES
    half = HEAD_DIM // 2
    inv = ROPE_THETA ** (-jnp.arange(half, dtype=F32) / half)
    inv_cols = jnp.tile(inv, LANES // half).reshape(1, LANES)
    blk = jnp.arange(256) // HEAD_DIM
    avg = jnp.where(blk[:, None] == blk[None, :], 1.0 / HEAD_DIM, 0.0).astype(BF16)
    return pl.pallas_call(
        functools.partial(_inproj_even_kernel, tm=tm),
        grid=(B, S // tm),
        in_specs=[
            pl.BlockSpec((1, tm, D), lambda b, i: (b, i, 0)),
            pl.BlockSpec((3, 1, 1, D), lambda b, i: (0, b, 0, 0)),
            pl.BlockSpec((1, D), lambda b, i: (0, 0)),
            pl.BlockSpec((D, EVEN_IN), lambda b, i: (0, 0)),
            pl.BlockSpec((A_WIDTH, D), lambda b, i: (0, 0)),
            pl.BlockSpec((1, tm, 1), lambda b, i: (b, i, 0)),
            pl.BlockSpec((1, LANES), lambda b, i: (0, 0)),
            pl.BlockSpec((1, EVEN_IN), lambda b, i: (0, 0)),
            pl.BlockSpec((256, 256), lambda b, i: (0, 0)),
        ],
        out_specs=[
            pl.BlockSpec((1, tm, EVEN_IN), lambda b, i: (b, i, 0)),
            pl.BlockSpec((1, n_groups, tm // KV_BLOCK, LANES, KV_BLOCK), lambda b, i: (b, 0, i, 0, 0)),
        ],
        out_shape=[
            jax.ShapeDtypeStruct((B, S, EVEN_IN), BF16),
            jax.ShapeDtypeStruct((B, n_groups, S // KV_BLOCK, LANES, KV_BLOCK), BF16),
        ],
        scratch_shapes=[pltpu.VMEM((tm, D), BF16)],
        compiler_params=_cparams(("parallel", "parallel")),
        name="inproj_even",
    )(x, mod, norm_g.reshape(1, D), w_rows, w_vat, positions.reshape(B, S, 1),
      inv_cols, gain_cols, avg)


def _inproj_odd(x, mod, norm_g, w_in, tm):
    B, S, D = x.shape
    w_rows = jnp.concatenate([w_in[:, :2 * C_WIDTH], w_in[:, 3 * C_WIDTH:]], axis=1).astype(BF16)
    w_vt = w_in[:, 2 * C_WIDTH:3 * C_WIDTH].T.astype(BF16)
    n_groups = C_WIDTH // LANES
    return pl.pallas_call(
        functools.partial(_inproj_odd_kernel, tm=tm),
        grid=(B, S // tm),
        in_specs=[
            pl.BlockSpec((1, tm, D), lambda b, i: (b, i, 0)),
            pl.BlockSpec((3, 1, 1, D), lambda b, i: (0, b, 0, 0)),
            pl.BlockSpec((1, D), lambda b, i: (0, 0)),
            pl.BlockSpec((D, ODD_IN), lambda b, i: (0, 0)),
            pl.BlockSpec((C_WIDTH, D), lambda b, i: (0, 0)),
        ],
        out_specs=[
            pl.BlockSpec((1, tm, ODD_IN), lambda b, i: (b, i, 0)),
            pl.BlockSpec((1, n_groups, tm // KV_BLOCK, LANES, KV_BLOCK), lambda b, i: (b, 0, i, 0, 0)),
        ],
        out_shape=[
            jax.ShapeDtypeStruct((B, S, ODD_IN), BF16),
            jax.ShapeDtypeStruct((B, n_groups, S // KV_BLOCK, LANES, KV_BLOCK), BF16),
        ],
        scratch_shapes=[pltpu.VMEM((tm, D), BF16)],
        compiler_params=_cparams(("parallel", "parallel")),
        name="inproj_odd",
    )(x, mod, norm_g.reshape(1, D), w_rows, w_vt)


def _outproj_kernel(*refs, n_in):
    y_refs = refs[:n_in]
    w_ref, x_ref, mod_ref, o_ref = refs[n_in:]
    acc = None
    off = 0
    for y_ref in y_refs:
        wd = y_ref.shape[-1]
        part = jnp.dot(y_ref[0], w_ref[off:off + wd, :], preferred_element_type=F32)
        acc = part if acc is None else acc + part
        off += wd
    o_ref[0] = x_ref[0] + mod_ref[2, 0] * acc


def _outproj(ys, w_out, x, mod, tm):
    B, S, D = x.shape
    K = w_out.shape[0]
    n_in = len(ys)
    in_specs = [pl.BlockSpec((1, tm, y.shape[-1]), lambda b, i: (b, i, 0)) for y in ys]
    in_specs += [
        pl.BlockSpec((K, D), lambda b, i: (0, 0)),
        pl.BlockSpec((1, tm, D), lambda b, i: (b, i, 0)),
        pl.BlockSpec((3, 1, 1, D), lambda b, i: (0, b, 0, 0)),
    ]
    return pl.pallas_call(
        functools.partial(_outproj_kernel, n_in=n_in),
        grid=(B, S // tm),
        in_specs=in_specs,
        out_specs=pl.BlockSpec((1, tm, D), lambda b, i: (b, i, 0)),
        out_shape=jax.ShapeDtypeStruct((B, S, D), F32),
        compiler_params=_cparams(("parallel", "parallel")),
        name="outproj",
    )(*ys, w_out.astype(BF16), x, mod)


def _diff_attn_kernel(bound_ref, lam_ref, sub_ref, q_ref, k_ref, vt_ref, g_ref, o_ref,
                      qs_scr, s0_scr, s1_scr, smax0_scr, smax1_scr, m_scr, l_scr, acc_scr, *,
                      tq, tk, lambda_init):
    i = pl.program_id(2)
    q = q_ref[0]
    lane = lax.broadcasted_iota(jnp.int32, (tq, LANES), 1)
    zero = jnp.zeros_like(q)
    qs_scr[:tq] = jnp.where(lane < HEAD_DIM, q, zero)
    qs_scr[tq:] = jnp.where(lane >= HEAD_DIM, q, zero)

    slots = ((s0_scr, smax0_scr), (s1_scr, smax1_scr))

    def scores(j, slot, bounded):
        s_ref, smax_ref = slots[slot]
        k = k_ref[0, pl.ds(pl.multiple_of(j * tk, tk), tk), :]
        s = lax.dot_general(k, qs_scr[...], (((1,), (1,)), ((), ())),
                            preferred_element_type=F32)
        s_ref[...] = s
        if not bounded:
            smax_ref[...] = jnp.max(s, axis=0, keepdims=True)

    def causal(s, diag):
        key = lax.broadcasted_iota(jnp.int32, (tk, 2 * tq), 0) + diag * tk
        qry = lax.broadcasted_iota(jnp.int32, (tk, 2 * tq), 1) & (tq - 1)
        return jnp.where(key <= qry, s, NEG_BIG)

    def softmax_pv(j, slot, bounded, diag=None):
        s_ref, smax_ref = slots[slot]
        s = s_ref[...] if diag is None else causal(s_ref[...], diag)
        if bounded:
            p = jnp.exp2(s)
            l_scr[...] += jnp.sum(p, axis=0, keepdims=True)
            acc_scr[...] += jnp.dot(vt_ref[0, 0, j], p.astype(BF16), preferred_element_type=F32)
            return
        smax = smax_ref[...] if diag is None else jnp.max(s, axis=0, keepdims=True)
        m = m_scr[...]
        m_new = jnp.maximum(m, smax)
        alpha = jnp.exp2(m - m_new)
        p = jnp.exp2(s - m_new)
        l_scr[...] = alpha * l_scr[...] + jnp.sum(p, axis=0, keepdims=True)
        m_scr[...] = m_new
        pv = jnp.dot(vt_ref[0, 0, j], p.astype(BF16), preferred_element_type=F32)
        acc_scr[...] = alpha * acc_scr[...] + pv

    def sweep(bounded):
        m_scr[...] = jnp.full((1, 2 * tq), NEG_BIG, F32)
        l_scr[...] = jnp.zeros((1, 2 * tq), F32)
        acc_scr[...] = jnp.zeros((LANES, 2 * tq), F32)
        n_diag = tq // tk
        n_full = n_diag * i
        scores(0, 0, bounded)

        def body(jj, carry):
            j0 = 2 * jj
            scores(j0 + 1, 1, bounded)
            softmax_pv(j0, 0, bounded)
            scores(j0 + 2, 0, bounded)
            softmax_pv(j0 + 1, 1, bounded)
            return carry

        lax.fori_loop(0, (n_diag // 2) * i, body, 0)
        for d in range(n_diag):
            if d + 1 < n_diag:
                scores(n_full + d + 1, (d + 1) % 2, bounded)
            softmax_pv(n_full + d, d % 2, bounded, diag=d)

    is_bounded = bound_ref[0] <= LOGIT_BOUND
    pl.when(is_bounded)(lambda: sweep(True))
    pl.when(jnp.logical_not(is_bounded))(lambda: sweep(False))
    l = l_scr[...]
    acc = acc_scr[...]

    lam_rows = lam_ref[...]
    lam = (jnp.exp(jnp.sum(lam_rows[0:1] * lam_rows[1:2], axis=1, keepdims=True))
           - jnp.exp(jnp.sum(lam_rows[2:3] * lam_rows[3:4], axis=1, keepdims=True))
           + lambda_init)
    ot = acc[:, :tq] / l[:, :tq] - lam * (acc[:, tq:] / l[:, tq:])
    ms = jnp.mean(ot * ot, axis=0, keepdims=True)
    yt = ot * lax.rsqrt(ms + EPS) * (sub_ref[...] * (1.0 - lambda_init))
    o_ref[0] = (yt.T * g_ref[0].astype(F32)).astype(BF16)


def _diff_attention(u, vat, logit_bound, lam_rows, subln_g, lambda_init, tq):
    B, S, _ = u.shape
    tk = KV_BLOCK
    assert tq % (2 * tk) == 0 and S % tq == 0
    qb, kb, gb = OFF_QA // LANES, OFF_KA // LANES, OFF_GA // LANES
    stat = pltpu.VMEM((1, 2 * tq), F32)
    score = pltpu.VMEM((tk, 2 * tq), F32)
    return pl.pallas_call(
        functools.partial(_diff_attn_kernel, tq=tq, tk=tk, lambda_init=lambda_init),
        grid=(B, A_HEADS, S // tq),
        in_specs=[
            pl.BlockSpec(memory_space=pltpu.SMEM),
            pl.BlockSpec((4, HEAD_DIM), lambda b, h, i: (0, 0)),
            pl.BlockSpec((LANES, 1), lambda b, h, i: (0, 0)),
            pl.BlockSpec((1, tq, LANES), lambda b, h, i: (b, i, qb + h)),
            pl.BlockSpec((1, S, LANES), lambda b, h, i: (b, 0, kb + h)),
            pl.BlockSpec((1, 1, S // tk, LANES, tk), lambda b, h, i: (b, h, 0, 0, 0)),
            pl.BlockSpec((1, tq, LANES), lambda b, h, i: (b, i, gb + h)),
        ],
        out_specs=pl.BlockSpec((1, tq, LANES), lambda b, h, i: (b, i, h)),
        out_shape=jax.ShapeDtypeStruct((B, S, A_WIDTH), BF16),
        scratch_shapes=[pltpu.VMEM((2 * tq, LANES), BF16), score, score, stat, stat, stat, stat,
                        pltpu.VMEM((LANES, 2 * tq), F32)],
        compiler_params=_cparams(("parallel", "parallel", "arbitrary")),
        name="diff_attention",
    )(logit_bound.reshape(1).astype(F32), lam_rows, subln_g.reshape(LANES, 1), u, u, vat, u)


def _swa_kernel(sink_ref, q_ref, kc_ref, kp_ref, vc_ref, vp_ref, g0_ref, g1_ref, o_ref, *, t):
    i = pl.program_id(1)
    g_refs = (g0_ref, g1_ref)
    heads_per_g = B_HEADS // len(g_refs)
    kk = jnp.concatenate([kp_ref[0], kc_ref[0]], axis=0)
    vv = jnp.concatenate([vp_ref[0], vc_ref[0]], axis=0)
    row = lax.broadcasted_iota(jnp.int32, (t, WINDOW + t), 0)
    col = lax.broadcasted_iota(jnp.int32, (t, WINDOW + t), 1)
    mask = (col > row) & (col <= row + WINDOW) & jnp.logical_or(i > 0, col >= WINDOW)
    group = B_HEADS // B_KV_HEADS
    for h in range(B_HEADS):
        kv = h // group
        qh = q_ref[0, :, h * HEAD_DIM:(h + 1) * HEAD_DIM]
        kh = kk[:, kv * HEAD_DIM:(kv + 1) * HEAD_DIM]
        vh = vv[:, kv * HEAD_DIM:(kv + 1) * HEAD_DIM]
        s = lax.dot_general(qh, kh, (((1,), (1,)), ((), ())), preferred_element_type=F32)
        s = jnp.where(mask, s, NEG_BIG)
        sink = sink_ref[h]
        m = jnp.maximum(jnp.max(s, axis=1, keepdims=True), sink)
        e = jnp.exp(s - m)
        den = jnp.sum(e, axis=1, keepdims=True) + jnp.exp(sink - m)
        o = jnp.dot(e.astype(BF16), vh, preferred_element_type=F32) / den
        gc = (h % heads_per_g) * HEAD_DIM
        gate = g_refs[h // heads_per_g][0, :, gc:gc + HEAD_DIM].astype(F32)
        o_ref[0, :, h * HEAD_DIM:(h + 1) * HEAD_DIM] = (o * gate).astype(BF16)


def _swa_attention(u, sinks, t):
    B, S, _ = u.shape
    per = t // WINDOW
    gw = B_WIDTH // 2
    qb, kb, vb, gb = OFF_QB // B_WIDTH, OFF_KB // LANES, OFF_VB // LANES, OFF_GB // gw
    assert OFF_QB % B_WIDTH == 0 and OFF_GB % gw == 0
    prev = lambda b, i, cb: (b, jnp.maximum(i * per - 1, 0), cb)
    return pl.pallas_call(
        functools.partial(_swa_kernel, t=t),
        grid=(B, S // t),
        in_specs=[
            pl.BlockSpec(memory_space=pltpu.SMEM),
            pl.BlockSpec((1, t, B_WIDTH), lambda b, i: (b, i, qb)),
            pl.BlockSpec((1, t, LANES), lambda b, i: (b, i, kb)),
            pl.BlockSpec((1, WINDOW, LANES), lambda b, i: prev(b, i, kb)),
            pl.BlockSpec((1, t, LANES), lambda b, i: (b, i, vb)),
            pl.BlockSpec((1, WINDOW, LANES), lambda b, i: prev(b, i, vb)),
            pl.BlockSpec((1, t, gw), lambda b, i: (b, i, gb)),
            pl.BlockSpec((1, t, gw), lambda b, i: (b, i, gb + 1)),
        ],
        out_specs=pl.BlockSpec((1, t, B_WIDTH), lambda b, i: (b, i, 0)),
        out_shape=jax.ShapeDtypeStruct((B, S, B_WIDTH), BF16),
        compiler_params=_cparams(("parallel", "parallel")),
        name="swa_attention",
    )(sinks.astype(F32), u, u, u, u, u, u, u)


def _stick_kernel(tri_ref, q_ref, k_ref, vt_ref, g_ref, o_ref, qs_scr, run_scr, acc_scr, *, t, pairs):
    i = pl.program_id(2)
    heads = 2 * pairs
    width = heads * t
    lane = lax.broadcasted_iota(jnp.int32, (t, LANES), 1)
    for p in range(pairs):
        q = q_ref[0, :, p * LANES:(p + 1) * LANES]
        zero = jnp.zeros_like(q)
        qs_scr[(2 * p) * t:(2 * p + 1) * t] = jnp.where(lane < HEAD_DIM, q, zero)
        qs_scr[(2 * p + 1) * t:(2 * p + 2) * t] = jnp.where(lane >= HEAD_DIM, q, zero)
    tri = tri_ref[...]
    acc_scr[...] = jnp.zeros((LANES, width), F32)

    def prepare(j, diagonal):
        off = pl.multiple_of(j * t, t)
        z = jnp.concatenate([
            lax.dot_general(k_ref[0, pl.ds(off, t), p * LANES:(p + 1) * LANES],
                            qs_scr[2 * p * t:(2 * p + 2) * t],
                            (((1,), (1,)), ((), ())), preferred_element_type=F32)
            for p in range(pairs)], axis=1)
        sp = jnp.maximum(z, 0.0) + jnp.log(1.0 + jnp.exp2(-jnp.abs(z))) * LOG2E
        strict = None
        if diagonal:
            key = lax.broadcasted_iota(jnp.int32, (t, width), 0)
            qry = lax.broadcasted_iota(jnp.int32, (t, width), 1) & (t - 1)
            strict = key < qry
            sp = jnp.where(strict, sp, 0.0)
        csum = jnp.dot(tri, sp.astype(BF16), preferred_element_type=F32)
        return z, csum, strict

    def apply(j, z, csum, strict, run):
        w = jnp.exp2(z - csum - run)
        if strict is not None:
            w = jnp.where(strict, w, 0.0)
        w = w.astype(BF16)
        for p in range(pairs):
            cols = slice(2 * p * t, (2 * p + 2) * t)
            acc_scr[:, cols] += jnp.dot(vt_ref[0, p, j], w[:, cols], preferred_element_type=F32)
        return run + csum[0:1, :]

    prev = jnp.maximum(i - 1, 0)
    zd, cd, strict = prepare(i, True)
    zp, cp, _ = prepare(prev, False)
    run = apply(i, zd, cd, strict, jnp.zeros((1, width), F32))
    run = apply(prev, zp, cp, None, run + jnp.where(i > 0, 0.0, -NEG_BIG))
    run_scr[...] = run

    def cond(c):
        jj, go = c
        return jnp.logical_and(jj < i - 1, go)

    def body(c):
        jj, _ = c
        j = i - 2 - jj
        z, csum, _ = prepare(j, False)
        run = apply(j, z, csum, None, run_scr[...])
        run_scr[...] = run
        return jj + 1, jnp.min(run) < EXP2_UNDERFLOW

    lax.while_loop(cond, body, (jnp.int32(0), jnp.min(run) < EXP2_UNDERFLOW))
    first = lax.broadcasted_iota(jnp.int32, (LANES, t), 0) < HEAD_DIM
    for p in range(pairs):
        a = 2 * p * t
        ot = jnp.where(first, acc_scr[:, a:a + t], acc_scr[:, a + t:a + 2 * t])
        gate = g_ref[0, :, p * LANES:(p + 1) * LANES].astype(F32)
        o_ref[0, :, p * LANES:(p + 1) * LANES] = (ot.T * gate).astype(BF16)


def _stick_attention(u, vt, pairs):
    B, S, _ = u.shape
    t = KV_BLOCK
    width = pairs * LANES
    groups = C_WIDTH // width
    row = jnp.arange(t)
    tri = (row[None, :] >= row[:, None]).astype(BF16)
    return pl.pallas_call(
        functools.partial(_stick_kernel, t=t, pairs=pairs),
        grid=(B, groups, S // t),
        in_specs=[
            pl.BlockSpec((t, t), lambda b, h, i: (0, 0)),
            pl.BlockSpec((1, t, width), lambda b, h, i: (b, i, h)),
            pl.BlockSpec((1, S, width), lambda b, h, i: (b, 0, groups + h)),
            pl.BlockSpec((1, pairs, S // t, LANES, t), lambda b, h, i: (b, h, 0, 0, 0)),
            pl.BlockSpec((1, t, width), lambda b, h, i: (b, i, 2 * groups + h)),
        ],
        out_specs=pl.BlockSpec((1, t, width), lambda b, h, i: (b, i, h)),
        out_shape=jax.ShapeDtypeStruct((B, S, C_WIDTH), BF16),
        scratch_shapes=[pltpu.VMEM((2 * pairs * t, LANES), BF16),
                        pltpu.VMEM((1, 2 * pairs * t), F32),
                        pltpu.VMEM((LANES, 2 * pairs * t), F32)],
        compiler_params=_cparams(("parallel", "parallel", "arbitrary")),
        name="stick_attention",
    )(tri, u, u, vt, u)


def _tile(n, pref):
    t = min(pref, n)
    assert n % t == 0
    return t


def _even_layer(x, c, positions, layer, norm_g, w_mod, b_mod, w_in, a_q_gain, a_k_gain,
                lq1, lk1, lq2, lk2, a_subln_g, b_q_gain, b_k_gain, b_sinks, w_out):
    B, S, D = x.shape
    mod = _modulation(c, w_mod, b_mod)
    ones = lambda n: jnp.ones((n,), F32)
    gain_cols = jnp.concatenate([
        jnp.tile(a_q_gain * (QK_SCALE * LOG2E), A_QK // HEAD_DIM),
        jnp.tile(a_k_gain, A_QK // HEAD_DIM),
        ones(A_WIDTH),
        jnp.tile(b_q_gain * QK_SCALE, B_HEADS), jnp.tile(b_k_gain, B_KV_HEADS),
        ones(B_KV + B_WIDTH),
    ]).astype(F32).reshape(1, EVEN_IN)
    u, vat = _inproj_even(x, mod, norm_g, w_in, positions, gain_cols, _tile(S, 512))
    lambda_init = 0.8 - 0.6 * math.exp(-0.3 * layer)
    lam_rows = jnp.stack([lq1, lk1, lq2, lk2]).astype(F32)
    logit_bound = (HEAD_DIM * QK_SCALE * LOG2E * LOGIT_BOUND_MARGIN
                   * jnp.max(jnp.abs(a_q_gain)) * jnp.max(jnp.abs(a_k_gain)))
    ya = _diff_attention(u, vat, logit_bound, lam_rows, a_subln_g.astype(F32), lambda_init,
                         _tile(S, 1024))
    yb = _swa_attention(u, b_sinks, _tile(S, 256))
    return _outproj([ya, yb], w_out, x, mod, _tile(S, 512))


def _odd_layer(x, c, norm_g, w_mod, b_mod, w_in, w_out):
    B, S, D = x.shape
    mod = _modulation(c, w_mod, b_mod)
    u, vt = _inproj_odd(x, mod, norm_g, w_in, _tile(S, 512))
    y = _stick_attention(u, vt, pairs=2)
    return _outproj([y], w_out, x, mod, _tile(S, 512))


def kernel(x, c, positions, even_norm_g, even_w_mod, even_b_mod, even_w_in, a_q_gain, a_k_gain,
           a_lambda_q1, a_lambda_k1, a_lambda_q2, a_lambda_k2, a_subln_g, b_q_gain, b_k_gain,
           b_sinks, even_w_out, odd_norm_g, odd_w_mod, odd_b_mod, odd_w_in, odd_w_out):
    depth = even_norm_g.shape[0] + odd_norm_g.shape[0]
    for layer in range(depth):
        j = layer // 2
        if layer % 2 == 0:
            x = _even_layer(x, c, positions, layer, even_norm_g[j], even_w_mod[j], even_b_mod[j],
                            even_w_in[j], a_q_gain[j], a_k_gain[j], a_lambda_q1[j], a_lambda_k1[j],
                            a_lambda_q2[j], a_lambda_k2[j], a_subln_g[j], b_q_gain[j], b_k_gain[j],
                            b_sinks[j], even_w_out[j])
        else:
            x = _odd_layer(x, c, odd_norm_g[j], odd_w_mod[j], odd_b_mod[j], odd_w_in[j], odd_w_out[j])
    return x
```

```python
import functools
import math

import jax
import jax.numpy as jnp
from jax import lax
from jax.experimental import pallas as pl
from jax.experimental.pallas import tpu as pltpu

F32 = jnp.float32
BF16 = jnp.bfloat16

HEAD_DIM = 64
ROPE_THETA = 10000.0
EPS = 1e-6
WINDOW = 128
LANES = 128
QK_SCALE = HEAD_DIM ** -0.5

A_HEADS = 4
A_QK = A_HEADS * 2 * HEAD_DIM
A_WIDTH = A_HEADS * 2 * HEAD_DIM
B_HEADS = 8
B_KV_HEADS = 2
B_WIDTH = B_HEADS * HEAD_DIM
B_KV = B_KV_HEADS * HEAD_DIM
C_HEADS = 16
C_WIDTH = C_HEADS * HEAD_DIM

W_QA = 0
W_KA = W_QA + A_QK
W_VA = W_KA + A_QK
W_GA = W_VA + A_WIDTH
W_QB = W_GA + A_WIDTH
W_KB = W_QB + B_WIDTH
W_VB = W_KB + B_KV
W_GB = W_VB + B_KV
W_END = W_GB + B_WIDTH

OFF_QA = 0
OFF_KA = OFF_QA + A_QK
OFF_GA = OFF_KA + A_QK
OFF_QB = OFF_GA + A_WIDTH
OFF_GB = OFF_QB + B_WIDTH
OFF_KB = OFF_GB + B_WIDTH
EVEN_IN = OFF_KB + B_KV

EVEN_CHUNKS = (
    (OFF_QA, 256, "qk"), (OFF_QA + 256, 256, "qk"),
    (OFF_KA, 256, "qk"), (OFF_KA + 256, 256, "qk"),
    (OFF_GA, 512, "silu"),
    (OFF_QB, 256, "qk"), (OFF_QB + 256, 256, "qk"),
    (OFF_GB, 512, "silu"), (OFF_KB, 128, "qk"),
)
LOG2E = 1.4426950408889634
KV_BLOCK = 256
ODD_IN = 3 * C_WIDTH
ODD_CHUNKS = tuple(
    (kind_i * C_WIDTH + half * 512, 512, kind)
    for kind_i, kind in enumerate(("scale", "plain", "silu"))
    for half in range(2)
)

NEG_BIG = -1e30
EXP2_UNDERFLOW = 151.0
LOGIT_BOUND = 64.0
LOGIT_BOUND_MARGIN = 1.01
VMEM_LIMIT = 48 * 1024 * 1024


def _cparams(sem):
    return pltpu.CompilerParams(dimension_semantics=sem, vmem_limit_bytes=VMEM_LIMIT)


def _mod_kernel(c_ref, w_ref, b_ref, o_ref):
    c = c_ref[...]
    sc = c * jax.nn.sigmoid(c)
    o_ref[0] = jnp.dot(sc, w_ref[...], preferred_element_type=F32,
                       precision=lax.Precision.HIGHEST) + b_ref[0]


def _modulation(c, w_mod, b_mod):
    B, D = c.shape
    rows = 8
    cp = jnp.pad(c, ((0, rows - B), (0, 0)))
    out = pl.pallas_call(
        _mod_kernel,
        grid=(3,),
        in_specs=[
            pl.BlockSpec((rows, D), lambda j: (0, 0)),
            pl.BlockSpec((D, D), lambda j: (0, j)),
            pl.BlockSpec((1, 1, D), lambda j: (j, 0, 0)),
        ],
        out_specs=pl.BlockSpec((1, rows, D), lambda j: (j, 0, 0)),
        out_shape=jax.ShapeDtypeStruct((3, rows, D), F32),
        compiler_params=_cparams(("arbitrary",)),
        name="modulation",
    )(cp, w_mod, b_mod.reshape(3, 1, D))
    return out[:, :B].reshape(3, B, 1, D)


def _modulated_rows(x_ref, mod_ref, ng_ref):
    x = x_ref[0]
    ms = jnp.mean(x * x, axis=-1, keepdims=True)
    y = x * lax.rsqrt(ms + EPS) * ng_ref[...]
    return y * (1.0 + mod_ref[1, 0]) + mod_ref[0, 0]


def _store_transposed_values(wvt_ref, h_scr, vt_ref, tm):
    vt = lax.dot_general(wvt_ref[...], h_scr[...], (((1,), (1,)), ((), ())),
                         preferred_element_type=F32)
    for g in range(vt.shape[0] // LANES):
        for cb in range(tm // KV_BLOCK):
            vt_ref[0, g, cb] = vt[g * LANES:(g + 1) * LANES,
                                  cb * KV_BLOCK:(cb + 1) * KV_BLOCK].astype(BF16)


def _inproj_even_kernel(x_ref, mod_ref, ng_ref, w_ref, wvt_ref, pos_ref, inv_ref, gain_ref, p_ref,
                        o_ref, vt_ref, h_scr, *, tm):
    h_scr[...] = _modulated_rows(x_ref, mod_ref, ng_ref).astype(BF16)
    _store_transposed_values(wvt_ref, h_scr, vt_ref, tm)
    ang = pos_ref[0].astype(F32) * inv_ref[...]
    cos = jnp.cos(ang)
    sin = jnp.sin(ang)
    lane = lax.broadcasted_iota(jnp.int32, (tm, LANES), 1)
    first = (lane & (HEAD_DIM // 2)) == 0
    sin_s = jnp.where(first, -sin, sin)
    for start, width, kind in EVEN_CHUNKS:
        u = jnp.dot(h_scr[...], w_ref[:, start:start + width], preferred_element_type=F32)
        if kind == "qk":
            ms = jnp.dot((u * u).astype(BF16), p_ref[:width, :width], preferred_element_type=F32)
            un = u * lax.rsqrt(ms + EPS) * gain_ref[:, start:start + width]
            for s in range(width // LANES):
                xs = un[:, s * LANES:(s + 1) * LANES]
                rot = jnp.where(first, pltpu.roll(xs, LANES - HEAD_DIM // 2, 1),
                                pltpu.roll(xs, HEAD_DIM // 2, 1))
                c0 = start + s * LANES
                o_ref[0, :, c0:c0 + LANES] = (xs * cos + rot * sin_s).astype(BF16)
        elif kind == "silu":
            o_ref[0, :, start:start + width] = (u * jax.nn.sigmoid(u)).astype(BF16)
        else:
            o_ref[0, :, start:start + width] = u.astype(BF16)


def _inproj_odd_kernel(x_ref, mod_ref, ng_ref, w_ref, wvt_ref, o_ref, vt_ref, h_scr, *, tm):
    h_scr[...] = _modulated_rows(x_ref, mod_ref, ng_ref).astype(BF16)
    _store_transposed_values(wvt_ref, h_scr, vt_ref, tm)
    for start, width, kind in ODD_CHUNKS:
        u = jnp.dot(h_scr[...], w_ref[:, start:start + width], preferred_element_type=F32)
        if kind == "scale":
            u = u * (QK_SCALE * LOG2E)
        elif kind == "silu":
            u = u * jax.nn.sigmoid(u)
        o_ref[0, :, start:start + width] = u.astype(BF16)


def _inproj_even(x, mod, norm_g, w_in, positions, gain_cols, tm):
    B, S, D = x.shape
    w_rows = jnp.concatenate([w_in[:, :W_VA], w_in[:, W_GA:W_KB], w_in[:, W_GB:],
                              w_in[:, W_KB:W_VB]], axis=1).astype(BF16)
    w_vat = jnp.concatenate([w_in[:, W_VA:W_GA], w_in[:, W_VB:W_GB]], axis=1).T.astype(BF16)
    n_groups = (A_WIDTH + B_KV) // LANES
    half = HEAD_DIM // 2
    inv = ROPE_THETA ** (-jnp.arange(half, dtype=F32) / half)
    inv_cols = jnp.tile(inv, LANES // half).reshape(1, LANES)
    blk = jnp.arange(256) // HEAD_DIM
    avg = jnp.where(blk[:, None] == blk[None, :], 1.0 / HEAD_DIM, 0.0).astype(BF16)
    return pl.pallas_call(
        functools.partial(_inproj_even_kernel, tm=tm),
        grid=(B, S // tm),
        in_specs=[
            pl.BlockSpec((1, tm, D), lambda b, i: (b, i, 0)),
            pl.BlockSpec((3, 1, 1, D), lambda b, i: (0, b, 0, 0)),
            pl.BlockSpec((1, D), lambda b, i: (0, 0)),
            pl.BlockSpec((D, EVEN_IN), lambda b, i: (0, 0)),
            pl.BlockSpec((A_WIDTH + B_KV, D), lambda b, i: (0, 0)),
            pl.BlockSpec((1, tm, 1), lambda b, i: (b, i, 0)),
            pl.BlockSpec((1, LANES), lambda b, i: (0, 0)),
            pl.BlockSpec((1, EVEN_IN), lambda b, i: (0, 0)),
            pl.BlockSpec((256, 256), lambda b, i: (0, 0)),
        ],
        out_specs=[
            pl.BlockSpec((1, tm, EVEN_IN), lambda b, i: (b, i, 0)),
            pl.BlockSpec((1, n_groups, tm // KV_BLOCK, LANES, KV_BLOCK), lambda b, i: (b, 0, i, 0, 0)),
        ],
        out_shape=[
            jax.ShapeDtypeStruct((B, S, EVEN_IN), BF16),
            jax.ShapeDtypeStruct((B, n_groups, S // KV_BLOCK, LANES, KV_BLOCK), BF16),
        ],
        scratch_shapes=[pltpu.VMEM((tm, D), BF16)],
        compiler_params=_cparams(("parallel", "parallel")),
        name="inproj_even",
    )(x, mod, norm_g.reshape(1, D), w_rows, w_vat, positions.reshape(B, S, 1),
      inv_cols, gain_cols, avg)


def _inproj_odd(x, mod, norm_g, w_in, tm):
    B, S, D = x.shape
    w_rows = jnp.concatenate([w_in[:, :2 * C_WIDTH], w_in[:, 3 * C_WIDTH:]], axis=1).astype(BF16)
    w_vt = w_in[:, 2 * C_WIDTH:3 * C_WIDTH].T.astype(BF16)
    n_groups = C_WIDTH // LANES
    return pl.pallas_call(
        functools.partial(_inproj_odd_kernel, tm=tm),
        grid=(B, S // tm),
        in_specs=[
            pl.BlockSpec((1, tm, D), lambda b, i: (b, i, 0)),
            pl.BlockSpec((3, 1, 1, D), lambda b, i: (0, b, 0, 0)),
            pl.BlockSpec((1, D), lambda b, i: (0, 0)),
            pl.BlockSpec((D, ODD_IN), lambda b, i: (0, 0)),
            pl.BlockSpec((C_WIDTH, D), lambda b, i: (0, 0)),
        ],
        out_specs=[
            pl.BlockSpec((1, tm, ODD_IN), lambda b, i: (b, i, 0)),
            pl.BlockSpec((1, n_groups, tm // KV_BLOCK, LANES, KV_BLOCK), lambda b, i: (b, 0, i, 0, 0)),
        ],
        out_shape=[
            jax.ShapeDtypeStruct((B, S, ODD_IN), BF16),
            jax.ShapeDtypeStruct((B, n_groups, S // KV_BLOCK, LANES, KV_BLOCK), BF16),
        ],
        scratch_shapes=[pltpu.VMEM((tm, D), BF16)],
        compiler_params=_cparams(("parallel", "parallel")),
        name="inproj_odd",
    )(x, mod, norm_g.reshape(1, D), w_rows, w_vt)


def _outproj_kernel(*refs, n_in):
    y_refs = refs[:n_in]
    w_ref, x_ref, mod_ref, o_ref = refs[n_in:]
    acc = None
    off = 0
    for y_ref in y_refs:
        wd = y_ref.shape[-1]
        part = jnp.dot(y_ref[0], w_ref[off:off + wd, :], preferred_element_type=F32)
        acc = part if acc is None else acc + part
        off += wd
    o_ref[0] = x_ref[0] + mod_ref[2, 0] * acc


def _outproj(ys, w_out, x, mod, tm):
    B, S, D = x.shape
    K = w_out.shape[0]
    n_in = len(ys)
    in_specs = [pl.BlockSpec((1, tm, y.shape[-1]), lambda b, i: (b, i, 0)) for y in ys]
    in_specs += [
        pl.BlockSpec((K, D), lambda b, i: (0, 0)),
        pl.BlockSpec((1, tm, D), lambda b, i: (b, i, 0)),
        pl.BlockSpec((3, 1, 1, D), lambda b, i: (0, b, 0, 0)),
    ]
    return pl.pallas_call(
        functools.partial(_outproj_kernel, n_in=n_in),
        grid=(B, S // tm),
        in_specs=in_specs,
        out_specs=pl.BlockSpec((1, tm, D), lambda b, i: (b, i, 0)),
        out_shape=jax.ShapeDtypeStruct((B, S, D), F32),
        compiler_params=_cparams(("parallel", "parallel")),
        name="outproj",
    )(*ys, w_out.astype(BF16), x, mod)


def _diff_attn_kernel(bound_ref, lam_ref, sub_ref, q_ref, k_ref, vt_ref, g_ref, o_ref,
                      qs_scr, s0_scr, s1_scr, smax0_scr, smax1_scr, m_scr, l_scr, acc_scr, *,
                      tq, tk, lambda_init):
    i = pl.program_id(2)
    q = q_ref[0]
    lane = lax.broadcasted_iota(jnp.int32, (tq, LANES), 1)
    zero = jnp.zeros_like(q)
    qs_scr[:tq] = jnp.where(lane < HEAD_DIM, q, zero)
    qs_scr[tq:] = jnp.where(lane >= HEAD_DIM, q, zero)

    slots = ((s0_scr, smax0_scr), (s1_scr, smax1_scr))

    def scores(j, slot, bounded):
        s_ref, smax_ref = slots[slot]
        k = k_ref[0, pl.ds(pl.multiple_of(j * tk, tk), tk), :]
        s = lax.dot_general(k, qs_scr[...], (((1,), (1,)), ((), ())),
                            preferred_element_type=F32)
        s_ref[...] = s
        if not bounded:
            smax_ref[...] = jnp.max(s, axis=0, keepdims=True)

    def causal(s, diag):
        key = lax.broadcasted_iota(jnp.int32, (tk, 2 * tq), 0) + diag * tk
        qry = lax.broadcasted_iota(jnp.int32, (tk, 2 * tq), 1) & (tq - 1)
        return jnp.where(key <= qry, s, NEG_BIG)

    def softmax_pv(j, slot, bounded, diag=None):
        s_ref, smax_ref = slots[slot]
        s = s_ref[...] if diag is None else causal(s_ref[...], diag)
        if bounded:
            p = jnp.exp2(s)
            l_scr[...] += jnp.sum(p, axis=0, keepdims=True)
            acc_scr[...] += jnp.dot(vt_ref[0, 0, j], p.astype(BF16), preferred_element_type=F32)
            return
        smax = smax_ref[...] if diag is None else jnp.max(s, axis=0, keepdims=True)
        m = m_scr[...]
        m_new = jnp.maximum(m, smax)
        alpha = jnp.exp2(m - m_new)
        p = jnp.exp2(s - m_new)
        l_scr[...] = alpha * l_scr[...] + jnp.sum(p, axis=0, keepdims=True)
        m_scr[...] = m_new
        pv = jnp.dot(vt_ref[0, 0, j], p.astype(BF16), preferred_element_type=F32)
        acc_scr[...] = alpha * acc_scr[...] + pv

    def sweep(bounded):
        m_scr[...] = jnp.full((1, 2 * tq), NEG_BIG, F32)
        l_scr[...] = jnp.zeros((1, 2 * tq), F32)
        acc_scr[...] = jnp.zeros((LANES, 2 * tq), F32)
        n_diag = tq // tk
        n_full = n_diag * i
        scores(0, 0, bounded)

        def body(jj, carry):
            j0 = 2 * jj
            scores(j0 + 1, 1, bounded)
            softmax_pv(j0, 0, bounded)
            scores(j0 + 2, 0, bounded)
            softmax_pv(j0 + 1, 1, bounded)
            return carry

        lax.fori_loop(0, (n_diag // 2) * i, body, 0)
        for d in range(n_diag):
            if d + 1 < n_diag:
                scores(n_full + d + 1, (d + 1) % 2, bounded)
            softmax_pv(n_full + d, d % 2, bounded, diag=d)

    is_bounded = bound_ref[0] <= LOGIT_BOUND
    pl.when(is_bounded)(lambda: sweep(True))
    pl.when(jnp.logical_not(is_bounded))(lambda: sweep(False))
    l = l_scr[...]
    acc = acc_scr[...]

    lam_rows = lam_ref[...]
    lam = (jnp.exp(jnp.sum(lam_rows[0:1] * lam_rows[1:2], axis=1, keepdims=True))
           - jnp.exp(jnp.sum(lam_rows[2:3] * lam_rows[3:4], axis=1, keepdims=True))
           + lambda_init)
    ot = acc[:, :tq] / l[:, :tq] - lam * (acc[:, tq:] / l[:, tq:])
    ms = jnp.mean(ot * ot, axis=0, keepdims=True)
    yt = ot * lax.rsqrt(ms + EPS) * (sub_ref[...] * (1.0 - lambda_init))
    o_ref[0] = (yt.T * g_ref[0].astype(F32)).astype(BF16)


def _diff_attention(u, vat, logit_bound, lam_rows, subln_g, lambda_init, tq):
    B, S, _ = u.shape
    tk = KV_BLOCK
    assert tq % (2 * tk) == 0 and S % tq == 0
    qb, kb, gb = OFF_QA // LANES, OFF_KA // LANES, OFF_GA // LANES
    stat = pltpu.VMEM((1, 2 * tq), F32)
    score = pltpu.VMEM((tk, 2 * tq), F32)
    return pl.pallas_call(
        functools.partial(_diff_attn_kernel, tq=tq, tk=tk, lambda_init=lambda_init),
        grid=(B, A_HEADS, S // tq),
        in_specs=[
            pl.BlockSpec(memory_space=pltpu.SMEM),
            pl.BlockSpec((4, HEAD_DIM), lambda b, h, i: (0, 0)),
            pl.BlockSpec((LANES, 1), lambda b, h, i: (0, 0)),
            pl.BlockSpec((1, tq, LANES), lambda b, h, i: (b, i, qb + h)),
            pl.BlockSpec((1, S, LANES), lambda b, h, i: (b, 0, kb + h)),
            pl.BlockSpec((1, 1, S // tk, LANES, tk), lambda b, h, i: (b, h, 0, 0, 0)),
            pl.BlockSpec((1, tq, LANES), lambda b, h, i: (b, i, gb + h)),
        ],
        out_specs=pl.BlockSpec((1, tq, LANES), lambda b, h, i: (b, i, h)),
        out_shape=jax.ShapeDtypeStruct((B, S, A_WIDTH), BF16),
        scratch_shapes=[pltpu.VMEM((2 * tq, LANES), BF16), score, score, stat, stat, stat, stat,
                        pltpu.VMEM((LANES, 2 * tq), F32)],
        compiler_params=_cparams(("parallel", "parallel", "arbitrary")),
        name="diff_attention",
    )(logit_bound.reshape(1).astype(F32), lam_rows, subln_g.reshape(LANES, 1), u, u, vat, u)


def _swa_kernel(sink_ref, q_ref, kc_ref, kp_ref, vtc_ref, vtp_ref, g_ref, o_ref, qs_scr, *, t):
    i = pl.program_id(1)
    group = B_HEADS // B_KV_HEADS
    width = B_HEADS * t
    lane = lax.broadcasted_iota(jnp.int32, (t, LANES), 1)
    for h in range(B_HEADS):
        kv = h // group
        qc = q_ref[0, :, (h // 2) * LANES:(h // 2 + 1) * LANES].astype(F32)
        if h % 2 != kv:
            qc = pltpu.roll(qc, HEAD_DIM, 1)
        keep = (lane >= kv * HEAD_DIM) & (lane < (kv + 1) * HEAD_DIM)
        qs_scr[h * t:(h + 1) * t] = jnp.where(keep, qc, 0.0).astype(BF16)
    kk = jnp.concatenate([kp_ref[0], kc_ref[0]], axis=0)
    vvt = jnp.concatenate([vtp_ref[0, 0, 0][:, KV_BLOCK - WINDOW:], vtc_ref[0, 0, 0]], axis=1)
    s = lax.dot_general(kk, qs_scr[...], (((1,), (1,)), ((), ())),
                        preferred_element_type=F32)
    key = lax.broadcasted_iota(jnp.int32, (WINDOW + t, width), 0)
    seg = lax.broadcasted_iota(jnp.int32, (WINDOW + t, width), 1)
    qry = seg & (t - 1)
    mask = (key > qry) & (key <= qry + WINDOW) & jnp.logical_or(i > 0, key >= WINDOW)
    s = jnp.where(mask, s, NEG_BIG)
    head = lax.broadcasted_iota(jnp.int32, (1, width), 1) // t
    sink = jnp.zeros((1, width), F32)
    for h in range(B_HEADS):
        sink = jnp.where(head == h, sink_ref[h] * LOG2E, sink)
    m = jnp.maximum(jnp.max(s, axis=0, keepdims=True), sink)
    e = jnp.exp2(s - m)
    den = jnp.sum(e, axis=0, keepdims=True) + jnp.exp2(sink - m)
    pv = jnp.dot(vvt, e.astype(BF16), preferred_element_type=F32) / den
    for c in range(B_HEADS // 2):
        rows = []
        for h in (2 * c, 2 * c + 1):
            kv = h // group
            rows.append(pv[kv * HEAD_DIM:(kv + 1) * HEAD_DIM, h * t:(h + 1) * t])
        ot = jnp.concatenate(rows, axis=0)
        gate = g_ref[0, :, c * LANES:(c + 1) * LANES].astype(F32)
        o_ref[0, :, c * LANES:(c + 1) * LANES] = (ot.T * gate).astype(BF16)


def _swa_attention(u, vt, sinks):
    B, S, _ = u.shape
    t = KV_BLOCK
    per = t // WINDOW
    qb, gb, kb = OFF_QB // B_WIDTH, OFF_GB // B_WIDTH, OFF_KB // LANES
    assert OFF_QB % B_WIDTH == 0 and OFF_GB % B_WIDTH == 0
    vgroup = A_WIDTH // LANES
    return pl.pallas_call(
        functools.partial(_swa_kernel, t=t),
        grid=(B, S // t),
        in_specs=[
            pl.BlockSpec(memory_space=pltpu.SMEM),
            pl.BlockSpec((1, t, B_WIDTH), lambda b, i: (b, i, qb)),
            pl.BlockSpec((1, t, LANES), lambda b, i: (b, i, kb)),
            pl.BlockSpec((1, WINDOW, LANES), lambda b, i: (b, jnp.maximum(i * per - 1, 0), kb)),
            pl.BlockSpec((1, 1, 1, LANES, t), lambda b, i: (b, vgroup, i, 0, 0)),
            pl.BlockSpec((1, 1, 1, LANES, t), lambda b, i: (b, vgroup, jnp.maximum(i - 1, 0), 0, 0)),
            pl.BlockSpec((1, t, B_WIDTH), lambda b, i: (b, i, gb)),
        ],
        out_specs=pl.BlockSpec((1, t, B_WIDTH), lambda b, i: (b, i, 0)),
        out_shape=jax.ShapeDtypeStruct((B, S, B_WIDTH), BF16),
        scratch_shapes=[pltpu.VMEM((B_HEADS * t, LANES), BF16)],
        compiler_params=_cparams(("parallel", "parallel")),
        name="swa_attention",
    )(sinks.astype(F32), u, u, u, vt, vt, u)


def _stick_kernel(tri_ref, q_ref, k_ref, vt_ref, g_ref, o_ref, qs_scr, run_scr, acc_scr, *, t, pairs):
    i = pl.program_id(2)
    heads = 2 * pairs
    width = heads * t
    lane = lax.broadcasted_iota(jnp.int32, (t, LANES), 1)
    for p in range(pairs):
        q = q_ref[0, :, p * LANES:(p + 1) * LANES]
        zero = jnp.zeros_like(q)
        qs_scr[(2 * p) * t:(2 * p + 1) * t] = jnp.where(lane < HEAD_DIM, q, zero)
        qs_scr[(2 * p + 1) * t:(2 * p + 2) * t] = jnp.where(lane >= HEAD_DIM, q, zero)
    tri = tri_ref[...]
    acc_scr[...] = jnp.zeros((LANES, width), F32)

    def prepare(j, diagonal):
        off = pl.multiple_of(j * t, t)
        z = jnp.concatenate([
            lax.dot_general(k_ref[0, pl.ds(off, t), p * LANES:(p + 1) * LANES],
                            qs_scr[2 * p * t:(2 * p + 2) * t],
                            (((1,), (1,)), ((), ())), preferred_element_type=F32)
            for p in range(pairs)], axis=1)
        sp = jnp.maximum(z, 0.0) + jnp.log(1.0 + jnp.exp2(-jnp.abs(z))) * LOG2E
        strict = None
        if diagonal:
            key = lax.broadcasted_iota(jnp.int32, (t, width), 0)
            qry = lax.broadcasted_iota(jnp.int32, (t, width), 1) & (t - 1)
            strict = key < qry
            sp = jnp.where(strict, sp, 0.0)
        csum = jnp.dot(tri, sp.astype(BF16), preferred_element_type=F32)
        return z, csum, strict

    def apply(j, z, csum, strict, run):
        w = jnp.exp2(z - csum - run)
        if strict is not None:
            w = jnp.where(strict, w, 0.0)
        w = w.astype(BF16)
        for p in range(pairs):
            cols = slice(2 * p * t, (2 * p + 2) * t)
            acc_scr[:, cols] += jnp.dot(vt_ref[0, p, j], w[:, cols], preferred_element_type=F32)
        return run + csum[0:1, :]

    prev = jnp.maximum(i - 1, 0)
    zd, cd, strict = prepare(i, True)
    zp, cp, _ = prepare(prev, False)
    run = apply(i, zd, cd, strict, jnp.zeros((1, width), F32))
    run = apply(prev, zp, cp, None, run + jnp.where(i > 0, 0.0, -NEG_BIG))
    run_scr[...] = run

    def cond(c):
        jj, go = c
        return jnp.logical_and(jj < i - 1, go)

    def body(c):
        jj, _ = c
        j = i - 2 - jj
        z, csum, _ = prepare(j, False)
        run = apply(j, z, csum, None, run_scr[...])
        run_scr[...] = run
        return jj + 1, jnp.min(run) < EXP2_UNDERFLOW

    lax.while_loop(cond, body, (jnp.int32(0), jnp.min(run) < EXP2_UNDERFLOW))
    first = lax.broadcasted_iota(jnp.int32, (LANES, t), 0) < HEAD_DIM
    for p in range(pairs):
        a = 2 * p * t
        ot = jnp.where(first, acc_scr[:, a:a + t], acc_scr[:, a + t:a + 2 * t])
        gate = g_ref[0, :, p * LANES:(p + 1) * LANES].astype(F32)
        o_ref[0, :, p * LANES:(p + 1) * LANES] = (ot.T * gate).astype(BF16)


def _stick_attention(u, vt, pairs):
    B, S, _ = u.shape
    t = KV_BLOCK
    width = pairs * LANES
    groups = C_WIDTH // width
    row = jnp.arange(t)
    tri = (row[None, :] >= row[:, None]).astype(BF16)
    return pl.pallas_call(
        functools.partial(_stick_kernel, t=t, pairs=pairs),
        grid=(B, groups, S // t),
        in_specs=[
            pl.BlockSpec((t, t), lambda b, h, i: (0, 0)),
            pl.BlockSpec((1, t, width), lambda b, h, i: (b, i, h)),
            pl.BlockSpec((1, S, width), lambda b, h, i: (b, 0, groups + h)),
            pl.BlockSpec((1, pairs, S // t, LANES, t), lambda b, h, i: (b, h, 0, 0, 0)),
            pl.BlockSpec((1, t, width), lambda b, h, i: (b, i, 2 * groups + h)),
        ],
        out_specs=pl.BlockSpec((1, t, width), lambda b, h, i: (b, i, h)),
        out_shape=jax.ShapeDtypeStruct((B, S, C_WIDTH), BF16),
        scratch_shapes=[pltpu.VMEM((2 * pairs * t, LANES), BF16),
                        pltpu.VMEM((1, 2 * pairs * t), F32),
                        pltpu.VMEM((LANES, 2 * pairs * t), F32)],
        compiler_params=_cparams(("parallel", "parallel", "arbitrary")),
        name="stick_attention",
    )(tri, u, u, vt, u)


def _tile(n, pref):
    t = min(pref, n)
    assert n % t == 0
    return t


def _even_layer(x, c, positions, layer, norm_g, w_mod, b_mod, w_in, a_q_gain, a_k_gain,
                lq1, lk1, lq2, lk2, a_subln_g, b_q_gain, b_k_gain, b_sinks, w_out):
    B, S, D = x.shape
    mod = _modulation(c, w_mod, b_mod)
    ones = lambda n: jnp.ones((n,), F32)
    gain_cols = jnp.concatenate([
        jnp.tile(a_q_gain * (QK_SCALE * LOG2E), A_QK // HEAD_DIM),
        jnp.tile(a_k_gain, A_QK // HEAD_DIM),
        ones(A_WIDTH),
        jnp.tile(b_q_gain * (QK_SCALE * LOG2E), B_HEADS),
        ones(B_WIDTH),
        jnp.tile(b_k_gain, B_KV_HEADS),
    ]).astype(F32).reshape(1, EVEN_IN)
    u, vat = _inproj_even(x, mod, norm_g, w_in, positions, gain_cols, _tile(S, 512))
    lambda_init = 0.8 - 0.6 * math.exp(-0.3 * layer)
    lam_rows = jnp.stack([lq1, lk1, lq2, lk2]).astype(F32)
    logit_bound = (HEAD_DIM * QK_SCALE * LOG2E * LOGIT_BOUND_MARGIN
                   * jnp.max(jnp.abs(a_q_gain)) * jnp.max(jnp.abs(a_k_gain)))
    ya = _diff_attention(u, vat, logit_bound, lam_rows, a_subln_g.astype(F32), lambda_init,
                         _tile(S, 1024))
    yb = _swa_attention(u, vat, b_sinks)
    return _outproj([ya, yb], w_out, x, mod, _tile(S, 512))


def _odd_layer(x, c, norm_g, w_mod, b_mod, w_in, w_out):
    B, S, D = x.shape
    mod = _modulation(c, w_mod, b_mod)
    u, vt = _inproj_odd(x, mod, norm_g, w_in, _tile(S, 512))
    y = _stick_attention(u, vt, pairs=2)
    return _outproj([y], w_out, x, mod, _tile(S, 512))


def kernel(x, c, positions, even_norm_g, even_w_mod, even_b_mod, even_w_in, a_q_gain, a_k_gain,
           a_lambda_q1, a_lambda_k1, a_lambda_q2, a_lambda_k2, a_subln_g, b_q_gain, b_k_gain,
           b_sinks, even_w_out, odd_norm_g, odd_w_mod, odd_b_mod, odd_w_in, odd_w_out):
    depth = even_norm_g.shape[0] + odd_norm_g.shape[0]
    for layer in range(depth):
        j = layer // 2
        if layer % 2 == 0:
            x = _even_layer(x, c, positions, layer, even_norm_g[j], even_w_mod[j], even_b_mod[j],
                            even_w_in[j], a_q_gain[j], a_k_gain[j], a_lambda_q1[j], a_lambda_k1[j],
                            a_lambda_q2[j], a_lambda_k2[j], a_subln_g[j], b_q_gain[j], b_k_gain[j],
                            b_sinks[j], even_w_out[j])
        else:
            x = _odd_layer(x, c, odd_norm_g[j], odd_w_mod[j], odd_b_mod[j], odd_w_in[j], odd_w_out[j])
    return x
```

```python
import functools
import math

import jax
import jax.numpy as jnp
from jax import lax
from jax.experimental import pallas as pl
from jax.experimental.pallas import tpu as pltpu

F32 = jnp.float32
BF16 = jnp.bfloat16

HEAD_DIM = 64
ROPE_THETA = 10000.0
EPS = 1e-6
WINDOW = 128
LANES = 128
QK_SCALE = HEAD_DIM ** -0.5

A_HEADS = 4
A_QK = A_HEADS * 2 * HEAD_DIM
A_WIDTH = A_HEADS * 2 * HEAD_DIM
B_HEADS = 8
B_KV_HEADS = 2
B_WIDTH = B_HEADS * HEAD_DIM
B_KV = B_KV_HEADS * HEAD_DIM
C_HEADS = 16
C_WIDTH = C_HEADS * HEAD_DIM

W_QA = 0
W_KA = W_QA + A_QK
W_VA = W_KA + A_QK
W_GA = W_VA + A_WIDTH
W_QB = W_GA + A_WIDTH
W_KB = W_QB + B_WIDTH
W_VB = W_KB + B_KV
W_GB = W_VB + B_KV
W_END = W_GB + B_WIDTH

OFF_QA = 0
OFF_KA = OFF_QA + A_QK
OFF_GA = OFF_KA + A_QK
OFF_QB = OFF_GA + A_WIDTH
OFF_GB = OFF_QB + B_WIDTH
OFF_KB = OFF_GB + B_WIDTH
EVEN_IN = OFF_KB + B_KV

EVEN_CHUNKS = (
    (OFF_QA, 256, "qk"), (OFF_QA + 256, 256, "qk"),
    (OFF_KA, 256, "qk"), (OFF_KA + 256, 256, "qk"),
    (OFF_GA, 512, "silu"),
    (OFF_QB, 256, "qk"), (OFF_QB + 256, 256, "qk"),
    (OFF_GB, 512, "silu"), (OFF_KB, 128, "qk"),
)
LOG2E = 1.4426950408889634
KV_BLOCK = 256
ODD_IN = 3 * C_WIDTH
ODD_CHUNKS = tuple(
    (kind_i * C_WIDTH + half * 512, 512, kind)
    for kind_i, kind in enumerate(("scale", "plain", "silu"))
    for half in range(2)
)

NEG_BIG = -1e30
EXP2_UNDERFLOW = 151.0
LOGIT_BOUND = 64.0
LOGIT_BOUND_MARGIN = 1.01
VMEM_LIMIT = 48 * 1024 * 1024


def _cparams(sem):
    return pltpu.CompilerParams(dimension_semantics=sem, vmem_limit_bytes=VMEM_LIMIT)


def _mod_kernel(c_ref, w_ref, b_ref, o_ref):
    c = c_ref[...]
    sc = c * jax.nn.sigmoid(c)
    o_ref[0] = jnp.dot(sc, w_ref[...], preferred_element_type=F32,
                       precision=lax.Precision.HIGHEST) + b_ref[0]


def _modulation(c, w_mod, b_mod):
    B, D = c.shape
    rows = 8
    cp = jnp.pad(c, ((0, rows - B), (0, 0)))
    out = pl.pallas_call(
        _mod_kernel,
        grid=(3,),
        in_specs=[
            pl.BlockSpec((rows, D), lambda j: (0, 0)),
            pl.BlockSpec((D, D), lambda j: (0, j)),
            pl.BlockSpec((1, 1, D), lambda j: (j, 0, 0)),
        ],
        out_specs=pl.BlockSpec((1, rows, D), lambda j: (j, 0, 0)),
        out_shape=jax.ShapeDtypeStruct((3, rows, D), F32),
        compiler_params=_cparams(("arbitrary",)),
        name="modulation",
    )(cp, w_mod, b_mod.reshape(3, 1, D))
    return out[:, :B].reshape(3, B, 1, D)


def _modulated_rows(x_ref, mod_ref, ng_ref):
    x = x_ref[0]
    ms = jnp.mean(x * x, axis=-1, keepdims=True)
    y = x * lax.rsqrt(ms + EPS) * ng_ref[...]
    return y * (1.0 + mod_ref[1, 0]) + mod_ref[0, 0]


def _store_transposed_values(wvt_ref, h_scr, vt_ref, tm):
    vt = lax.dot_general(wvt_ref[...], h_scr[...], (((1,), (1,)), ((), ())),
                         preferred_element_type=F32)
    for g in range(vt.shape[0] // LANES):
        for cb in range(tm // KV_BLOCK):
            vt_ref[0, g, cb] = vt[g * LANES:(g + 1) * LANES,
                                  cb * KV_BLOCK:(cb + 1) * KV_BLOCK].astype(BF16)


def _inproj_even_kernel(x_ref, mod_ref, ng_ref, w_ref, wvt_ref, pos_ref, inv_ref, gain_ref, p_ref,
                        o_ref, vt_ref, h_scr, *, tm):
    h_scr[...] = _modulated_rows(x_ref, mod_ref, ng_ref).astype(BF16)
    _store_transposed_values(wvt_ref, h_scr, vt_ref, tm)
    ang = pos_ref[0].astype(F32) * inv_ref[...]
    cos = jnp.cos(ang)
    sin = jnp.sin(ang)
    lane = lax.broadcasted_iota(jnp.int32, (tm, LANES), 1)
    first = (lane & (HEAD_DIM // 2)) == 0
    sin_s = jnp.where(first, -sin, sin)
    for start, width, kind in EVEN_CHUNKS:
        u = jnp.dot(h_scr[...], w_ref[:, start:start + width], preferred_element_type=F32)
        if kind == "qk":
            ms = jnp.dot((u * u).astype(BF16), p_ref[:width, :width], preferred_element_type=F32)
            un = u * lax.rsqrt(ms + EPS) * gain_ref[:, start:start + width]
            for s in range(width // LANES):
                xs = un[:, s * LANES:(s + 1) * LANES]
                rot = jnp.where(first, pltpu.roll(xs, LANES - HEAD_DIM // 2, 1),
                                pltpu.roll(xs, HEAD_DIM // 2, 1))
                c0 = start + s * LANES
                o_ref[0, :, c0:c0 + LANES] = (xs * cos + rot * sin_s).astype(BF16)
        elif kind == "silu":
            o_ref[0, :, start:start + width] = (u * jax.nn.sigmoid(u)).astype(BF16)
        else:
            o_ref[0, :, start:start + width] = u.astype(BF16)


def _inproj_odd_kernel(x_ref, mod_ref, ng_ref, w_ref, wvt_ref, o_ref, vt_ref, h_scr, *, tm):
    h_scr[...] = _modulated_rows(x_ref, mod_ref, ng_ref).astype(BF16)
    _store_transposed_values(wvt_ref, h_scr, vt_ref, tm)
    for start, width, kind in ODD_CHUNKS:
        u = jnp.dot(h_scr[...], w_ref[:, start:start + width], preferred_element_type=F32)
        if kind == "scale":
            u = u * (QK_SCALE * LOG2E)
        elif kind == "silu":
            u = u * jax.nn.sigmoid(u)
        o_ref[0, :, start:start + width] = u.astype(BF16)


def _inproj_even(x, mod, norm_g, w_in, positions, gain_cols, tm):
    B, S, D = x.shape
    w_rows = jnp.concatenate([w_in[:, :W_VA], w_in[:, W_GA:W_KB], w_in[:, W_GB:],
                              w_in[:, W_KB:W_VB]], axis=1).astype(BF16)
    w_vat = jnp.concatenate([w_in[:, W_VA:W_GA], w_in[:, W_VB:W_GB]], axis=1).T.astype(BF16)
    n_groups = (A_WIDTH + B_KV) // LANES
    half = HEAD_DIM // 2
    inv = ROPE_THETA ** (-jnp.arange(half, dtype=F32) / half)
    inv_cols = jnp.tile(inv, LANES // half).reshape(1, LANES)
    blk = jnp.arange(256) // HEAD_DIM
    avg = jnp.where(blk[:, None] == blk[None, :], 1.0 / HEAD_DIM, 0.0).astype(BF16)
    return pl.pallas_call(
        functools.partial(_inproj_even_kernel, tm=tm),
        grid=(B, S // tm),
        in_specs=[
            pl.BlockSpec((1, tm, D), lambda b, i: (b, i, 0)),
            pl.BlockSpec((3, 1, 1, D), lambda b, i: (0, b, 0, 0)),
            pl.BlockSpec((1, D), lambda b, i: (0, 0)),
            pl.BlockSpec((D, EVEN_IN), lambda b, i: (0, 0)),
            pl.BlockSpec((A_WIDTH + B_KV, D), lambda b, i: (0, 0)),
            pl.BlockSpec((1, tm, 1), lambda b, i: (b, i, 0)),
            pl.BlockSpec((1, LANES), lambda b, i: (0, 0)),
            pl.BlockSpec((1, EVEN_IN), lambda b, i: (0, 0)),
            pl.BlockSpec((256, 256), lambda b, i: (0, 0)),
        ],
        out_specs=[
            pl.BlockSpec((1, tm, EVEN_IN), lambda b, i: (b, i, 0)),
            pl.BlockSpec((1, n_groups, tm // KV_BLOCK, LANES, KV_BLOCK), lambda b, i: (b, 0, i, 0, 0)),
        ],
        out_shape=[
            jax.ShapeDtypeStruct((B, S, EVEN_IN), BF16),
            jax.ShapeDtypeStruct((B, n_groups, S // KV_BLOCK, LANES, KV_BLOCK), BF16),
        ],
        scratch_shapes=[pltpu.VMEM((tm, D), BF16)],
        compiler_params=_cparams(("parallel", "parallel")),
        name="inproj_even",
    )(x, mod, norm_g.reshape(1, D), w_rows, w_vat, positions.reshape(B, S, 1),
      inv_cols, gain_cols, avg)


def _inproj_odd(x, mod, norm_g, w_in, tm):
    B, S, D = x.shape
    w_rows = jnp.concatenate([w_in[:, :2 * C_WIDTH], w_in[:, 3 * C_WIDTH:]], axis=1).astype(BF16)
    w_vt = w_in[:, 2 * C_WIDTH:3 * C_WIDTH].T.astype(BF16)
    n_groups = C_WIDTH // LANES
    return pl.pallas_call(
        functools.partial(_inproj_odd_kernel, tm=tm),
        grid=(B, S // tm),
        in_specs=[
            pl.BlockSpec((1, tm, D), lambda b, i: (b, i, 0)),
            pl.BlockSpec((3, 1, 1, D), lambda b, i: (0, b, 0, 0)),
            pl.BlockSpec((1, D), lambda b, i: (0, 0)),
            pl.BlockSpec((D, ODD_IN), lambda b, i: (0, 0)),
            pl.BlockSpec((C_WIDTH, D), lambda b, i: (0, 0)),
        ],
        out_specs=[
            pl.BlockSpec((1, tm, ODD_IN), lambda b, i: (b, i, 0)),
            pl.BlockSpec((1, n_groups, tm // KV_BLOCK, LANES, KV_BLOCK), lambda b, i: (b, 0, i, 0, 0)),
        ],
        out_shape=[
            jax.ShapeDtypeStruct((B, S, ODD_IN), BF16),
            jax.ShapeDtypeStruct((B, n_groups, S // KV_BLOCK, LANES, KV_BLOCK), BF16),
        ],
        scratch_shapes=[pltpu.VMEM((tm, D), BF16)],
        compiler_params=_cparams(("parallel", "parallel")),
        name="inproj_odd",
    )(x, mod, norm_g.reshape(1, D), w_rows, w_vt)


def _outproj_kernel(*refs, n_in):
    y_refs = refs[:n_in]
    w_ref, x_ref, mod_ref, o_ref = refs[n_in:]
    acc = None
    off = 0
    for y_ref in y_refs:
        wd = y_ref.shape[-1]
        part = jnp.dot(y_ref[0], w_ref[off:off + wd, :], preferred_element_type=F32)
        acc = part if acc is None else acc + part
        off += wd
    o_ref[0] = x_ref[0] + mod_ref[2, 0] * acc


def _outproj(ys, w_out, x, mod, tm):
    B, S, D = x.shape
    K = w_out.shape[0]
    n_in = len(ys)
    in_specs = [pl.BlockSpec((1, tm, y.shape[-1]), lambda b, i: (b, i, 0)) for y in ys]
    in_specs += [
        pl.BlockSpec((K, D), lambda b, i: (0, 0)),
        pl.BlockSpec((1, tm, D), lambda b, i: (b, i, 0)),
        pl.BlockSpec((3, 1, 1, D), lambda b, i: (0, b, 0, 0)),
    ]
    return pl.pallas_call(
        functools.partial(_outproj_kernel, n_in=n_in),
        grid=(B, S // tm),
        in_specs=in_specs,
        out_specs=pl.BlockSpec((1, tm, D), lambda b, i: (b, i, 0)),
        out_shape=jax.ShapeDtypeStruct((B, S, D), F32),
        compiler_params=_cparams(("parallel", "parallel")),
        name="outproj",
    )(*ys, w_out.astype(BF16), x, mod)


def _diff_attn_kernel(bound_ref, lam_ref, sub_ref, q_ref, k_ref, vt_ref, g_ref, o_ref,
                      qs_scr, s0_scr, s1_scr, smax0_scr, smax1_scr, m_scr, l_scr, acc_scr, *,
                      tq, tk, lambda_init):
    i = pl.program_id(2)
    q = q_ref[0]
    lane = lax.broadcasted_iota(jnp.int32, (tq, LANES), 1)
    zero = jnp.zeros_like(q)
    qs_scr[:tq] = jnp.where(lane < HEAD_DIM, q, zero)
    qs_scr[tq:] = jnp.where(lane >= HEAD_DIM, q, zero)

    slots = ((s0_scr, smax0_scr), (s1_scr, smax1_scr))

    def spans(diag):
        if diag is None:
            return ((0, 2 * tq),)
        return tuple((c * tq + diag * tk, (c + 1) * tq) for c in range(2))

    def scores(j, slot, bounded, diag=None):
        s_ref, smax_ref = slots[slot]
        k = k_ref[0, pl.ds(pl.multiple_of(j * tk, tk), tk), :]
        for a, b in spans(diag):
            s = lax.dot_general(k, qs_scr[a:b], (((1,), (1,)), ((), ())),
                                preferred_element_type=F32)
            s_ref[:, a:b] = s
            if not bounded and diag is None:
                smax_ref[:, a:b] = jnp.max(s, axis=0, keepdims=True)

    def softmax_pv(j, slot, bounded, diag=None):
        s_ref, smax_ref = slots[slot]
        for a, b in spans(diag):
            s = s_ref[:, a:b]
            if diag is not None:
                key = lax.broadcasted_iota(jnp.int32, (tk, b - a), 0) + diag * tk
                qry = (lax.broadcasted_iota(jnp.int32, (tk, b - a), 1) + a) & (tq - 1)
                s = jnp.where(key <= qry, s, NEG_BIG)
            if bounded:
                p = jnp.exp2(s)
                l_scr[:, a:b] += jnp.sum(p, axis=0, keepdims=True)
                acc_scr[:, a:b] += jnp.dot(vt_ref[0, 0, j], p.astype(BF16),
                                           preferred_element_type=F32)
                continue
            smax = smax_ref[:, a:b] if diag is None else jnp.max(s, axis=0, keepdims=True)
            m = m_scr[:, a:b]
            m_new = jnp.maximum(m, smax)
            alpha = jnp.exp2(m - m_new)
            p = jnp.exp2(s - m_new)
            l_scr[:, a:b] = alpha * l_scr[:, a:b] + jnp.sum(p, axis=0, keepdims=True)
            m_scr[:, a:b] = m_new
            pv = jnp.dot(vt_ref[0, 0, j], p.astype(BF16), preferred_element_type=F32)
            acc_scr[:, a:b] = alpha * acc_scr[:, a:b] + pv

    def sweep(bounded):
        m_scr[...] = jnp.full((1, 2 * tq), NEG_BIG, F32)
        l_scr[...] = jnp.zeros((1, 2 * tq), F32)
        acc_scr[...] = jnp.zeros((LANES, 2 * tq), F32)
        n_diag = tq // tk
        n_full = n_diag * i
        scores(0, 0, bounded)

        def body(jj, carry):
            j0 = n_diag * jj
            for d in range(n_diag):
                scores(j0 + d + 1, (d + 1) % 2, bounded)
                softmax_pv(j0 + d, d % 2, bounded)
            return carry

        lax.fori_loop(0, i, body, 0)
        for d in range(n_diag):
            if d + 1 < n_diag:
                scores(n_full + d + 1, (d + 1) % 2, bounded, diag=d + 1)
            softmax_pv(n_full + d, d % 2, bounded, diag=d)

    is_bounded = bound_ref[0] <= LOGIT_BOUND
    pl.when(is_bounded)(lambda: sweep(True))
    pl.when(jnp.logical_not(is_bounded))(lambda: sweep(False))
    l = l_scr[...]
    acc = acc_scr[...]

    lam_rows = lam_ref[...]
    lam = (jnp.exp(jnp.sum(lam_rows[0:1] * lam_rows[1:2], axis=1, keepdims=True))
           - jnp.exp(jnp.sum(lam_rows[2:3] * lam_rows[3:4], axis=1, keepdims=True))
           + lambda_init)
    ot = acc[:, :tq] / l[:, :tq] - lam * (acc[:, tq:] / l[:, tq:])
    ms = jnp.mean(ot * ot, axis=0, keepdims=True)
    yt = ot * lax.rsqrt(ms + EPS) * (sub_ref[...] * (1.0 - lambda_init))
    o_ref[0] = (yt.T * g_ref[0].astype(F32)).astype(BF16)


def _diff_attention(u, vat, logit_bound, lam_rows, subln_g, lambda_init, tq):
    B, S, _ = u.shape
    tk = KV_BLOCK
    assert tq % (2 * tk) == 0 and S % tq == 0
    qb, kb, gb = OFF_QA // LANES, OFF_KA // LANES, OFF_GA // LANES
    stat = pltpu.VMEM((1, 2 * tq), F32)
    score = pltpu.VMEM((tk, 2 * tq), F32)
    return pl.pallas_call(
        functools.partial(_diff_attn_kernel, tq=tq, tk=tk, lambda_init=lambda_init),
        grid=(B, A_HEADS, S // tq),
        in_specs=[
            pl.BlockSpec(memory_space=pltpu.SMEM),
            pl.BlockSpec((4, HEAD_DIM), lambda b, h, i: (0, 0)),
            pl.BlockSpec((LANES, 1), lambda b, h, i: (0, 0)),
            pl.BlockSpec((1, tq, LANES), lambda b, h, i: (b, i, qb + h)),
            pl.BlockSpec((1, S, LANES), lambda b, h, i: (b, 0, kb + h)),
            pl.BlockSpec((1, 1, S // tk, LANES, tk), lambda b, h, i: (b, h, 0, 0, 0)),
            pl.BlockSpec((1, tq, LANES), lambda b, h, i: (b, i, gb + h)),
        ],
        out_specs=pl.BlockSpec((1, tq, LANES), lambda b, h, i: (b, i, h)),
        out_shape=jax.ShapeDtypeStruct((B, S, A_WIDTH), BF16),
        scratch_shapes=[pltpu.VMEM((2 * tq, LANES), BF16), score, score, stat, stat, stat, stat,
                        pltpu.VMEM((LANES, 2 * tq), F32)],
        compiler_params=_cparams(("parallel", "parallel", "arbitrary")),
        name="diff_attention",
    )(logit_bound.reshape(1).astype(F32), lam_rows, subln_g.reshape(LANES, 1), u, u, vat, u)


def _swa_kernel(sink_ref, q_ref, kc_ref, kp_ref, vtc_ref, vtp_ref, g_ref, o_ref, qs_scr, *, t):
    i = pl.program_id(1)
    group = B_HEADS // B_KV_HEADS
    width = B_HEADS * t
    lane = lax.broadcasted_iota(jnp.int32, (t, LANES), 1)
    for h in range(B_HEADS):
        kv = h // group
        qc = q_ref[0, :, (h // 2) * LANES:(h // 2 + 1) * LANES].astype(F32)
        if h % 2 != kv:
            qc = pltpu.roll(qc, HEAD_DIM, 1)
        keep = (lane >= kv * HEAD_DIM) & (lane < (kv + 1) * HEAD_DIM)
        qs_scr[h * t:(h + 1) * t] = jnp.where(keep, qc, 0.0).astype(BF16)
    kk = jnp.concatenate([kp_ref[0], kc_ref[0]], axis=0)
    vvt = jnp.concatenate([vtp_ref[0, 0, 0][:, KV_BLOCK - WINDOW:], vtc_ref[0, 0, 0]], axis=1)
    s = lax.dot_general(kk, qs_scr[...], (((1,), (1,)), ((), ())),
                        preferred_element_type=F32)
    key = lax.broadcasted_iota(jnp.int32, (WINDOW + t, width), 0)
    seg = lax.broadcasted_iota(jnp.int32, (WINDOW + t, width), 1)
    qry = seg & (t - 1)
    mask = (key > qry) & (key <= qry + WINDOW) & jnp.logical_or(i > 0, key >= WINDOW)
    s = jnp.where(mask, s, NEG_BIG)
    head = lax.broadcasted_iota(jnp.int32, (1, width), 1) // t
    sink = jnp.zeros((1, width), F32)
    for h in range(B_HEADS):
        sink = jnp.where(head == h, sink_ref[h] * LOG2E, sink)
    m = jnp.maximum(jnp.max(s, axis=0, keepdims=True), sink)
    e = jnp.exp2(s - m)
    den = jnp.sum(e, axis=0, keepdims=True) + jnp.exp2(sink - m)
    pv = jnp.dot(vvt, e.astype(BF16), preferred_element_type=F32) / den
    for c in range(B_HEADS // 2):
        rows = []
        for h in (2 * c, 2 * c + 1):
            kv = h // group
            rows.append(pv[kv * HEAD_DIM:(kv + 1) * HEAD_DIM, h * t:(h + 1) * t])
        ot = jnp.concatenate(rows, axis=0)
        gate = g_ref[0, :, c * LANES:(c + 1) * LANES].astype(F32)
        o_ref[0, :, c * LANES:(c + 1) * LANES] = (ot.T * gate).astype(BF16)


def _swa_attention(u, vt, sinks):
    B, S, _ = u.shape
    t = KV_BLOCK
    per = t // WINDOW
    qb, gb, kb = OFF_QB // B_WIDTH, OFF_GB // B_WIDTH, OFF_KB // LANES
    assert OFF_QB % B_WIDTH == 0 and OFF_GB % B_WIDTH == 0
    vgroup = A_WIDTH // LANES
    return pl.pallas_call(
        functools.partial(_swa_kernel, t=t),
        grid=(B, S // t),
        in_specs=[
            pl.BlockSpec(memory_space=pltpu.SMEM),
            pl.BlockSpec((1, t, B_WIDTH), lambda b, i: (b, i, qb)),
            pl.BlockSpec((1, t, LANES), lambda b, i: (b, i, kb)),
            pl.BlockSpec((1, WINDOW, LANES), lambda b, i: (b, jnp.maximum(i * per - 1, 0), kb)),
            pl.BlockSpec((1, 1, 1, LANES, t), lambda b, i: (b, vgroup, i, 0, 0)),
            pl.BlockSpec((1, 1, 1, LANES, t), lambda b, i: (b, vgroup, jnp.maximum(i - 1, 0), 0, 0)),
            pl.BlockSpec((1, t, B_WIDTH), lambda b, i: (b, i, gb)),
        ],
        out_specs=pl.BlockSpec((1, t, B_WIDTH), lambda b, i: (b, i, 0)),
        out_shape=jax.ShapeDtypeStruct((B, S, B_WIDTH), BF16),
        scratch_shapes=[pltpu.VMEM((B_HEADS * t, LANES), BF16)],
        compiler_params=_cparams(("parallel", "parallel")),
        name="swa_attention",
    )(sinks.astype(F32), u, u, u, vt, vt, u)


def _stick_kernel(tri_ref, q_ref, k_ref, vt_ref, g_ref, o_ref, qs_scr, run_scr, acc_scr, *, t, pairs):
    i = pl.program_id(2)
    heads = 2 * pairs
    width = heads * t
    lane = lax.broadcasted_iota(jnp.int32, (t, LANES), 1)
    for p in range(pairs):
        q = q_ref[0, :, p * LANES:(p + 1) * LANES]
        zero = jnp.zeros_like(q)
        qs_scr[(2 * p) * t:(2 * p + 1) * t] = jnp.where(lane < HEAD_DIM, q, zero)
        qs_scr[(2 * p + 1) * t:(2 * p + 2) * t] = jnp.where(lane >= HEAD_DIM, q, zero)
    tri = tri_ref[...]
    acc_scr[...] = jnp.zeros((LANES, width), F32)

    def prepare(j, diagonal):
        off = pl.multiple_of(j * t, t)
        z = jnp.concatenate([
            lax.dot_general(k_ref[0, pl.ds(off, t), p * LANES:(p + 1) * LANES],
                            qs_scr[2 * p * t:(2 * p + 2) * t],
                            (((1,), (1,)), ((), ())), preferred_element_type=F32)
            for p in range(pairs)], axis=1)
        if diagonal:
            key = lax.broadcasted_iota(jnp.int32, (t, width), 0)
            qry = lax.broadcasted_iota(jnp.int32, (t, width), 1) & (t - 1)
            z = jnp.where(key < qry, z, NEG_BIG)
        sp = jnp.maximum(z, 0.0) + jnp.log(1.0 + jnp.exp2(-jnp.abs(z))) * LOG2E
        csum = jnp.dot(tri, sp.astype(BF16), preferred_element_type=F32)
        return z, csum

    def apply(j, z, csum, run):
        w = jnp.exp2(z - csum - run).astype(BF16)
        for p in range(pairs):
            cols = slice(2 * p * t, (2 * p + 2) * t)
            acc_scr[:, cols] += jnp.dot(vt_ref[0, p, j], w[:, cols], preferred_element_type=F32)
        return run + csum[0:1, :]

    prev = jnp.maximum(i - 1, 0)
    zd, cd = prepare(i, True)
    zp, cp = prepare(prev, False)
    run = apply(i, zd, cd, jnp.zeros((1, width), F32))
    run = apply(prev, zp, cp, run + jnp.where(i > 0, 0.0, -NEG_BIG))
    run_scr[...] = run

    def cond(c):
        jj, go = c
        return jnp.logical_and(jj < i - 1, go)

    def body(c):
        jj, _ = c
        j = i - 2 - jj
        z, csum = prepare(j, False)
        run = apply(j, z, csum, run_scr[...])
        run_scr[...] = run
        return jj + 1, jnp.min(run) < EXP2_UNDERFLOW

    lax.while_loop(cond, body, (jnp.int32(0), jnp.min(run) < EXP2_UNDERFLOW))
    first = lax.broadcasted_iota(jnp.int32, (LANES, t), 0) < HEAD_DIM
    for p in range(pairs):
        a = 2 * p * t
        ot = jnp.where(first, acc_scr[:, a:a + t], acc_scr[:, a + t:a + 2 * t])
        gate = g_ref[0, :, p * LANES:(p + 1) * LANES].astype(F32)
        o_ref[0, :, p * LANES:(p + 1) * LANES] = (ot.T * gate).astype(BF16)


def _stick_attention(u, vt, pairs):
    B, S, _ = u.shape
    t = KV_BLOCK
    width = pairs * LANES
    groups = C_WIDTH // width
    row = jnp.arange(t)
    tri = (row[None, :] >= row[:, None]).astype(BF16)
    return pl.pallas_call(
        functools.partial(_stick_kernel, t=t, pairs=pairs),
        grid=(B, groups, S // t),
        in_specs=[
            pl.BlockSpec((t, t), lambda b, h, i: (0, 0)),
            pl.BlockSpec((1, t, width), lambda b, h, i: (b, i, h)),
            pl.BlockSpec((1, S, width), lambda b, h, i: (b, 0, groups + h)),
            pl.BlockSpec((1, pairs, S // t, LANES, t), lambda b, h, i: (b, h, 0, 0, 0)),
            pl.BlockSpec((1, t, width), lambda b, h, i: (b, i, 2 * groups + h)),
        ],
        out_specs=pl.BlockSpec((1, t, width), lambda b, h, i: (b, i, h)),
        out_shape=jax.ShapeDtypeStruct((B, S, C_WIDTH), BF16),
        scratch_shapes=[pltpu.VMEM((2 * pairs * t, LANES), BF16),
                        pltpu.VMEM((1, 2 * pairs * t), F32),
                        pltpu.VMEM((LANES, 2 * pairs * t), F32)],
        compiler_params=_cparams(("parallel", "parallel", "arbitrary")),
        name="stick_attention",
    )(tri, u, u, vt, u)


def _tile(n, pref):
    t = min(pref, n)
    assert n % t == 0
    return t


def _even_layer(x, c, positions, layer, norm_g, w_mod, b_mod, w_in, a_q_gain, a_k_gain,
                lq1, lk1, lq2, lk2, a_subln_g, b_q_gain, b_k_gain, b_sinks, w_out):
    B, S, D = x.shape
    mod = _modulation(c, w_mod, b_mod)
    ones = lambda n: jnp.ones((n,), F32)
    gain_cols = jnp.concatenate([
        jnp.tile(a_q_gain * (QK_SCALE * LOG2E), A_QK // HEAD_DIM),
        jnp.tile(a_k_gain, A_QK // HEAD_DIM),
        ones(A_WIDTH),
        jnp.tile(b_q_gain * (QK_SCALE * LOG2E), B_HEADS),
        ones(B_WIDTH),
        jnp.tile(b_k_gain, B_KV_HEADS),
    ]).astype(F32).reshape(1, EVEN_IN)
    u, vat = _inproj_even(x, mod, norm_g, w_in, positions, gain_cols, _tile(S, 512))
    lambda_init = 0.8 - 0.6 * math.exp(-0.3 * layer)
    lam_rows = jnp.stack([lq1, lk1, lq2, lk2]).astype(F32)
    logit_bound = (HEAD_DIM * QK_SCALE * LOG2E * LOGIT_BOUND_MARGIN
                   * jnp.max(jnp.abs(a_q_gain)) * jnp.max(jnp.abs(a_k_gain)))
    ya = _diff_attention(u, vat, logit_bound, lam_rows, a_subln_g.astype(F32), lambda_init,
                         _tile(S, 1024))
    yb = _swa_attention(u, vat, b_sinks)
    return _outproj([ya, yb], w_out, x, mod, _tile(S, 512))


def _odd_layer(x, c, norm_g, w_mod, b_mod, w_in, w_out):
    B, S, D = x.shape
    mod = _modulation(c, w_mod, b_mod)
    u, vt = _inproj_odd(x, mod, norm_g, w_in, _tile(S, 512))
    y = _stick_attention(u, vt, pairs=2)
    return _outproj([y], w_out, x, mod, _tile(S, 512))


def kernel(x, c, positions, even_norm_g, even_w_mod, even_b_mod, even_w_in, a_q_gain, a_k_gain,
           a_lambda_q1, a_lambda_k1, a_lambda_q2, a_lambda_k2, a_subln_g, b_q_gain, b_k_gain,
           b_sinks, even_w_out, odd_norm_g, odd_w_mod, odd_b_mod, odd_w_in, odd_w_out):
    depth = even_norm_g.shape[0] + odd_norm_g.shape[0]
    for layer in range(depth):
        j = layer // 2
        if layer % 2 == 0:
            x = _even_layer(x, c, positions, layer, even_norm_g[j], even_w_mod[j], even_b_mod[j],
                            even_w_in[j], a_q_gain[j], a_k_gain[j], a_lambda_q1[j], a_lambda_k1[j],
                            a_lambda_q2[j], a_lambda_k2[j], a_subln_g[j], b_q_gain[j], b_k_gain[j],
                            b_sinks[j], even_w_out[j])
        else:
            x = _odd_layer(x, c, odd_norm_g[j], odd_w_mod[j], odd_b_mod[j], odd_w_in[j], odd_w_out[j])
    return x
```

```python
import functools
import math

import jax
import jax.numpy as jnp
from jax import lax
from jax.experimental import pallas as pl
from jax.experimental.pallas import tpu as pltpu

F32 = jnp.float32
BF16 = jnp.bfloat16

HEAD_DIM = 64
ROPE_THETA = 10000.0
EPS = 1e-6
WINDOW = 128
LANES = 128
QK_SCALE = HEAD_DIM ** -0.5

A_HEADS = 4
A_QK = A_HEADS * 2 * HEAD_DIM
A_WIDTH = A_HEADS * 2 * HEAD_DIM
B_HEADS = 8
B_KV_HEADS = 2
B_WIDTH = B_HEADS * HEAD_DIM
B_KV = B_KV_HEADS * HEAD_DIM
C_HEADS = 16
C_WIDTH = C_HEADS * HEAD_DIM

W_QA = 0
W_KA = W_QA + A_QK
W_VA = W_KA + A_QK
W_GA = W_VA + A_WIDTH
W_QB = W_GA + A_WIDTH
W_KB = W_QB + B_WIDTH
W_VB = W_KB + B_KV
W_GB = W_VB + B_KV
W_END = W_GB + B_WIDTH

OFF_QA = 0
OFF_KA = OFF_QA + A_QK
OFF_GA = OFF_KA + A_QK
OFF_QB = OFF_GA + A_WIDTH
OFF_GB = OFF_QB + B_WIDTH
OFF_KB = OFF_GB + B_WIDTH
EVEN_IN = OFF_KB + B_KV

EVEN_CHUNKS = (
    (W_GA, OFF_GA, 512, "silu"), (W_GB, OFF_GB, 512, "silu"),
    (W_QA, OFF_QA, 256, "qk"), (W_QA + 256, OFF_QA + 256, 256, "qk"),
    (W_KA, OFF_KA, 256, "qk"), (W_KA + 256, OFF_KA + 256, 256, "qk"),
    (W_QB, OFF_QB, 256, "qk"), (W_QB + 256, OFF_QB + 256, 256, "qk"),
    (W_KB, OFF_KB, 128, "qk"),
)
LOG2E = 1.4426950408889634
KV_BLOCK = 256
ODD_IN = 3 * C_WIDTH
ODD_CHUNKS = tuple(
    (w_i * C_WIDTH + half * 512, o_i * C_WIDTH + half * 512, 512, kind)
    for o_i, (w_i, kind) in enumerate(((0, "scale"), (1, "plain"), (3, "silu")))
    for half in range(2)
)

NEG_BIG = -1e30
EXP2_UNDERFLOW = 151.0
LOGIT_BOUND = 64.0
LOGIT_BOUND_MARGIN = 1.01
VMEM_LIMIT = 48 * 1024 * 1024


def _cparams(sem):
    return pltpu.CompilerParams(dimension_semantics=sem, vmem_limit_bytes=VMEM_LIMIT)


def _mod_kernel(c_ref, w_ref, b_ref, o_ref):
    c = c_ref[...]
    sc = c * jax.nn.sigmoid(c)
    o_ref[0] = jnp.dot(sc, w_ref[...], preferred_element_type=F32,
                       precision=lax.Precision.HIGHEST) + b_ref[0]


def _modulation(c, w_mod, b_mod):
    B, D = c.shape
    rows = 8
    cp = jnp.pad(c, ((0, rows - B), (0, 0)))
    out = pl.pallas_call(
        _mod_kernel,
        grid=(3,),
        in_specs=[
            pl.BlockSpec((rows, D), lambda j: (0, 0)),
            pl.BlockSpec((D, D), lambda j: (0, j)),
            pl.BlockSpec((1, 1, D), lambda j: (j, 0, 0)),
        ],
        out_specs=pl.BlockSpec((1, rows, D), lambda j: (j, 0, 0)),
        out_shape=jax.ShapeDtypeStruct((3, rows, D), F32),
        compiler_params=_cparams(("arbitrary",)),
        name="modulation",
    )(cp, w_mod, b_mod.reshape(3, 1, D))
    return out[:, :B].reshape(3, B, 1, D)


def _modulated_rows(x_ref, mod_ref, ng_ref):
    x = x_ref[0]
    ms = jnp.mean(x * x, axis=-1, keepdims=True)
    y = x * lax.rsqrt(ms + EPS) * ng_ref[...]
    return y * (1.0 + mod_ref[1, 0]) + mod_ref[0, 0]


def _store_transposed_values(wvt_ref, h_scr, vt_ref, tm):
    vt = lax.dot_general(wvt_ref[...], h_scr[...], (((1,), (1,)), ((), ())),
                         preferred_element_type=F32)
    for g in range(vt.shape[0] // LANES):
        for cb in range(tm // KV_BLOCK):
            vt_ref[0, g, cb] = vt[g * LANES:(g + 1) * LANES,
                                  cb * KV_BLOCK:(cb + 1) * KV_BLOCK].astype(BF16)


def _inproj_even_kernel(x_ref, mod_ref, ng_ref, w_ref, wvt_ref, pos_ref, inv_ref, gain_ref, p_ref,
                        o_ref, vt_ref, h_scr, *, tm):
    h_scr[...] = _modulated_rows(x_ref, mod_ref, ng_ref).astype(BF16)
    _store_transposed_values(wvt_ref, h_scr, vt_ref, tm)
    ang = pos_ref[0].astype(F32) * inv_ref[...]
    cos = jnp.cos(ang)
    sin = jnp.sin(ang)
    lane = lax.broadcasted_iota(jnp.int32, (tm, LANES), 1)
    first = (lane & (HEAD_DIM // 2)) == 0
    sin_s = jnp.where(first, -sin, sin)
    for wcol, start, width, kind in EVEN_CHUNKS:
        u = jnp.dot(h_scr[...], w_ref[:, wcol:wcol + width], preferred_element_type=F32)
        if kind == "qk":
            ms = jnp.dot((u * u).astype(BF16), p_ref[:width, :width], preferred_element_type=F32)
            un = u * lax.rsqrt(ms + EPS) * gain_ref[:, start:start + width]
            for s in range(width // LANES):
                xs = un[:, s * LANES:(s + 1) * LANES]
                rot = jnp.where(first, pltpu.roll(xs, LANES - HEAD_DIM // 2, 1),
                                pltpu.roll(xs, HEAD_DIM // 2, 1))
                c0 = start + s * LANES
                o_ref[0, :, c0:c0 + LANES] = (xs * cos + rot * sin_s).astype(BF16)
        elif kind == "silu":
            o_ref[0, :, start:start + width] = (u * jax.nn.sigmoid(u)).astype(BF16)
        else:
            o_ref[0, :, start:start + width] = u.astype(BF16)


def _inproj_odd_kernel(x_ref, mod_ref, ng_ref, w_ref, wvt_ref, o_ref, vt_ref, h_scr, *, tm):
    h_scr[...] = _modulated_rows(x_ref, mod_ref, ng_ref).astype(BF16)
    _store_transposed_values(wvt_ref, h_scr, vt_ref, tm)
    for wcol, start, width, kind in ODD_CHUNKS:
        u = jnp.dot(h_scr[...], w_ref[:, wcol:wcol + width], preferred_element_type=F32)
        if kind == "scale":
            u = u * (QK_SCALE * LOG2E)
        elif kind == "silu":
            u = u * jax.nn.sigmoid(u)
        o_ref[0, :, start:start + width] = u.astype(BF16)


def _inproj_even(x, mod, norm_g, w_in, positions, gain_cols, tm):
    B, S, D = x.shape
    w_rows = w_in.astype(BF16)
    w_vat = jnp.concatenate([w_in[:, W_VA:W_GA], w_in[:, W_VB:W_GB]], axis=1).T.astype(BF16)
    n_groups = (A_WIDTH + B_KV) // LANES
    half = HEAD_DIM // 2
    inv = ROPE_THETA ** (-jnp.arange(half, dtype=F32) / half)
    inv_cols = jnp.tile(inv, LANES // half).reshape(1, LANES)
    blk = jnp.arange(256) // HEAD_DIM
    avg = jnp.where(blk[:, None] == blk[None, :], 1.0 / HEAD_DIM, 0.0).astype(BF16)
    return pl.pallas_call(
        functools.partial(_inproj_even_kernel, tm=tm),
        grid=(B, S // tm),
        in_specs=[
            pl.BlockSpec((1, tm, D), lambda b, i: (b, i, 0)),
            pl.BlockSpec((3, 1, 1, D), lambda b, i: (0, b, 0, 0)),
            pl.BlockSpec((1, D), lambda b, i: (0, 0)),
            pl.BlockSpec((D, W_END), lambda b, i: (0, 0)),
            pl.BlockSpec((A_WIDTH + B_KV, D), lambda b, i: (0, 0)),
            pl.BlockSpec((1, tm, 1), lambda b, i: (b, i, 0)),
            pl.BlockSpec((1, LANES), lambda b, i: (0, 0)),
            pl.BlockSpec((1, EVEN_IN), lambda b, i: (0, 0)),
            pl.BlockSpec((256, 256), lambda b, i: (0, 0)),
        ],
        out_specs=[
            pl.BlockSpec((1, tm, EVEN_IN), lambda b, i: (b, i, 0)),
            pl.BlockSpec((1, n_groups, tm // KV_BLOCK, LANES, KV_BLOCK), lambda b, i: (b, 0, i, 0, 0)),
        ],
        out_shape=[
            jax.ShapeDtypeStruct((B, S, EVEN_IN), BF16),
            jax.ShapeDtypeStruct((B, n_groups, S // KV_BLOCK, LANES, KV_BLOCK), BF16),
        ],
        scratch_shapes=[pltpu.VMEM((tm, D), BF16)],
        compiler_params=_cparams(("parallel", "parallel")),
        name="inproj_even",
    )(x, mod, norm_g.reshape(1, D), w_rows, w_vat, positions.reshape(B, S, 1),
      inv_cols, gain_cols, avg)


def _inproj_odd(x, mod, norm_g, w_in, tm):
    B, S, D = x.shape
    w_rows = w_in.astype(BF16)
    w_vt = w_in[:, 2 * C_WIDTH:3 * C_WIDTH].T.astype(BF16)
    n_groups = C_WIDTH // LANES
    return pl.pallas_call(
        functools.partial(_inproj_odd_kernel, tm=tm),
        grid=(B, S // tm),
        in_specs=[
            pl.BlockSpec((1, tm, D), lambda b, i: (b, i, 0)),
            pl.BlockSpec((3, 1, 1, D), lambda b, i: (0, b, 0, 0)),
            pl.BlockSpec((1, D), lambda b, i: (0, 0)),
            pl.BlockSpec((D, 4 * C_WIDTH), lambda b, i: (0, 0)),
            pl.BlockSpec((C_WIDTH, D), lambda b, i: (0, 0)),
        ],
        out_specs=[
            pl.BlockSpec((1, tm, ODD_IN), lambda b, i: (b, i, 0)),
            pl.BlockSpec((1, n_groups, tm // KV_BLOCK, LANES, KV_BLOCK), lambda b, i: (b, 0, i, 0, 0)),
        ],
        out_shape=[
            jax.ShapeDtypeStruct((B, S, ODD_IN), BF16),
            jax.ShapeDtypeStruct((B, n_groups, S // KV_BLOCK, LANES, KV_BLOCK), BF16),
        ],
        scratch_shapes=[pltpu.VMEM((tm, D), BF16)],
        compiler_params=_cparams(("parallel", "parallel")),
        name="inproj_odd",
    )(x, mod, norm_g.reshape(1, D), w_rows, w_vt)


def _outproj_kernel(*refs, n_in):
    y_refs = refs[:n_in]
    w_ref, x_ref, mod_ref, o_ref = refs[n_in:]
    acc = None
    off = 0
    for y_ref in y_refs:
        wd = y_ref.shape[-1]
        part = jnp.dot(y_ref[0], w_ref[off:off + wd, :], preferred_element_type=F32)
        acc = part if acc is None else acc + part
        off += wd
    o_ref[0] = x_ref[0] + mod_ref[2, 0] * acc


def _outproj(ys, w_out, x, mod, tm):
    B, S, D = x.shape
    K = w_out.shape[0]
    n_in = len(ys)
    in_specs = [pl.BlockSpec((1, tm, y.shape[-1]), lambda b, i: (b, i, 0)) for y in ys]
    in_specs += [
        pl.BlockSpec((K, D), lambda b, i: (0, 0)),
        pl.BlockSpec((1, tm, D), lambda b, i: (b, i, 0)),
        pl.BlockSpec((3, 1, 1, D), lambda b, i: (0, b, 0, 0)),
    ]
    return pl.pallas_call(
        functools.partial(_outproj_kernel, n_in=n_in),
        grid=(B, S // tm),
        in_specs=in_specs,
        out_specs=pl.BlockSpec((1, tm, D), lambda b, i: (b, i, 0)),
        out_shape=jax.ShapeDtypeStruct((B, S, D), F32),
        compiler_params=_cparams(("parallel", "parallel")),
        name="outproj",
    )(*ys, w_out.astype(BF16), x, mod)


def _diff_attn_kernel(bound_ref, lam_ref, sub_ref, q_ref, k_ref, vt_ref, g_ref, o_ref,
                      qs_scr, s0_scr, s1_scr, smax0_scr, smax1_scr, m_scr, l_scr, acc_scr, *,
                      tq, tk, lambda_init):
    i = pl.program_id(2)
    q = q_ref[0]
    lane = lax.broadcasted_iota(jnp.int32, (tq, LANES), 1)
    zero = jnp.zeros_like(q)
    qs_scr[:tq] = jnp.where(lane < HEAD_DIM, q, zero)
    qs_scr[tq:] = jnp.where(lane >= HEAD_DIM, q, zero)

    slots = ((s0_scr, smax0_scr), (s1_scr, smax1_scr))

    def spans(diag):
        if diag is None:
            return ((0, 2 * tq),)
        return tuple((c * tq + diag * tk, (c + 1) * tq) for c in range(2))

    def scores(j, slot, bounded, diag=None):
        s_ref, smax_ref = slots[slot]
        k = k_ref[0, pl.ds(pl.multiple_of(j * tk, tk), tk), :]
        for a, b in spans(diag):
            s = lax.dot_general(k, qs_scr[a:b], (((1,), (1,)), ((), ())),
                                preferred_element_type=F32)
            s_ref[:, a:b] = s
            if not bounded and diag is None:
                smax_ref[:, a:b] = jnp.max(s, axis=0, keepdims=True)

    def softmax_pv(j, slot, bounded, diag=None):
        s_ref, smax_ref = slots[slot]
        for a, b in spans(diag):
            s = s_ref[:, a:b]
            if diag is not None:
                key = lax.broadcasted_iota(jnp.int32, (tk, b - a), 0) + diag * tk
                qry = (lax.broadcasted_iota(jnp.int32, (tk, b - a), 1) + a) & (tq - 1)
                s = jnp.where(key <= qry, s, NEG_BIG)
            if bounded:
                p = jnp.exp2(s)
                l_scr[:, a:b] += jnp.sum(p, axis=0, keepdims=True)
                acc_scr[:, a:b] += jnp.dot(vt_ref[0, 0, j], p.astype(BF16),
                                           preferred_element_type=F32)
                continue
            smax = smax_ref[:, a:b] if diag is None else jnp.max(s, axis=0, keepdims=True)
            m = m_scr[:, a:b]
            m_new = jnp.maximum(m, smax)
            alpha = jnp.exp2(m - m_new)
            p = jnp.exp2(s - m_new)
            l_scr[:, a:b] = alpha * l_scr[:, a:b] + jnp.sum(p, axis=0, keepdims=True)
            m_scr[:, a:b] = m_new
            pv = jnp.dot(vt_ref[0, 0, j], p.astype(BF16), preferred_element_type=F32)
            acc_scr[:, a:b] = alpha * acc_scr[:, a:b] + pv

    def sweep(bounded):
        m_scr[...] = jnp.full((1, 2 * tq), NEG_BIG, F32)
        l_scr[...] = jnp.zeros((1, 2 * tq), F32)
        acc_scr[...] = jnp.zeros((LANES, 2 * tq), F32)
        n_diag = tq // tk
        n_full = n_diag * i
        scores(0, 0, bounded)

        def body(jj, carry):
            j0 = n_diag * jj
            for d in range(n_diag):
                scores(j0 + d + 1, (d + 1) % 2, bounded)
                softmax_pv(j0 + d, d % 2, bounded)
            return carry

        lax.fori_loop(0, i, body, 0)
        for d in range(n_diag):
            if d + 1 < n_diag:
                scores(n_full + d + 1, (d + 1) % 2, bounded, diag=d + 1)
            softmax_pv(n_full + d, d % 2, bounded, diag=d)

    is_bounded = bound_ref[0] <= LOGIT_BOUND
    pl.when(is_bounded)(lambda: sweep(True))
    pl.when(jnp.logical_not(is_bounded))(lambda: sweep(False))
    l = l_scr[...]
    acc = acc_scr[...]

    lam_rows = lam_ref[...]
    lam = (jnp.exp(jnp.sum(lam_rows[0:1] * lam_rows[1:2], axis=1, keepdims=True))
           - jnp.exp(jnp.sum(lam_rows[2:3] * lam_rows[3:4], axis=1, keepdims=True))
           + lambda_init)
    ot = acc[:, :tq] / l[:, :tq] - lam * (acc[:, tq:] / l[:, tq:])
    ms = jnp.mean(ot * ot, axis=0, keepdims=True)
    yt = ot * lax.rsqrt(ms + EPS) * (sub_ref[...] * (1.0 - lambda_init))
    o_ref[0] = (yt.T * g_ref[0].astype(F32)).astype(BF16)


def _diff_attention(u, vat, logit_bound, lam_rows, subln_g, lambda_init, tq):
    B, S, _ = u.shape
    tk = KV_BLOCK
    assert tq % (2 * tk) == 0 and S % tq == 0
    qb, kb, gb = OFF_QA // LANES, OFF_KA // LANES, OFF_GA // LANES
    stat = pltpu.VMEM((1, 2 * tq), F32)
    score = pltpu.VMEM((tk, 2 * tq), F32)
    return pl.pallas_call(
        functools.partial(_diff_attn_kernel, tq=tq, tk=tk, lambda_init=lambda_init),
        grid=(B, A_HEADS, S // tq),
        in_specs=[
            pl.BlockSpec(memory_space=pltpu.SMEM),
            pl.BlockSpec((4, HEAD_DIM), lambda b, h, i: (0, 0)),
            pl.BlockSpec((LANES, 1), lambda b, h, i: (0, 0)),
            pl.BlockSpec((1, tq, LANES), lambda b, h, i: (b, i, qb + h)),
            pl.BlockSpec((1, S, LANES), lambda b, h, i: (b, 0, kb + h)),
            pl.BlockSpec((1, 1, S // tk, LANES, tk), lambda b, h, i: (b, h, 0, 0, 0)),
            pl.BlockSpec((1, tq, LANES), lambda b, h, i: (b, i, gb + h)),
        ],
        out_specs=pl.BlockSpec((1, tq, LANES), lambda b, h, i: (b, i, h)),
        out_shape=jax.ShapeDtypeStruct((B, S, A_WIDTH), BF16),
        scratch_shapes=[pltpu.VMEM((2 * tq, LANES), BF16), score, score, stat, stat, stat, stat,
                        pltpu.VMEM((LANES, 2 * tq), F32)],
        compiler_params=_cparams(("parallel", "parallel", "arbitrary")),
        name="diff_attention",
    )(logit_bound.reshape(1).astype(F32), lam_rows, subln_g.reshape(LANES, 1), u, u, vat, u)


def _swa_kernel(sink_ref, q_ref, kc_ref, kp_ref, vtc_ref, vtp_ref, g_ref, o_ref, qs_scr, *, t):
    i = pl.program_id(1)
    group = B_HEADS // B_KV_HEADS
    width = B_HEADS * t
    lane = lax.broadcasted_iota(jnp.int32, (t, LANES), 1)
    for h in range(B_HEADS):
        kv = h // group
        qc = q_ref[0, :, (h // 2) * LANES:(h // 2 + 1) * LANES].astype(F32)
        if h % 2 != kv:
            qc = pltpu.roll(qc, HEAD_DIM, 1)
        keep = (lane >= kv * HEAD_DIM) & (lane < (kv + 1) * HEAD_DIM)
        qs_scr[h * t:(h + 1) * t] = jnp.where(keep, qc, 0.0).astype(BF16)
    kk = jnp.concatenate([kp_ref[0], kc_ref[0]], axis=0)
    vvt = jnp.concatenate([vtp_ref[0, 0, 0][:, KV_BLOCK - WINDOW:], vtc_ref[0, 0, 0]], axis=1)
    s = lax.dot_general(kk, qs_scr[...], (((1,), (1,)), ((), ())),
                        preferred_element_type=F32)
    key = lax.broadcasted_iota(jnp.int32, (WINDOW + t, width), 0)
    seg = lax.broadcasted_iota(jnp.int32, (WINDOW + t, width), 1)
    qry = seg & (t - 1)
    mask = (key > qry) & (key <= qry + WINDOW) & jnp.logical_or(i > 0, key >= WINDOW)
    s = jnp.where(mask, s, NEG_BIG)
    head = lax.broadcasted_iota(jnp.int32, (1, width), 1) // t
    sink = jnp.zeros((1, width), F32)
    for h in range(B_HEADS):
        sink = jnp.where(head == h, sink_ref[h] * LOG2E, sink)
    m = jnp.maximum(jnp.max(s, axis=0, keepdims=True), sink)
    e = jnp.exp2(s - m)
    den = jnp.sum(e, axis=0, keepdims=True) + jnp.exp2(sink - m)
    pv = jnp.dot(vvt, e.astype(BF16), preferred_element_type=F32) / den
    for c in range(B_HEADS // 2):
        rows = []
        for h in (2 * c, 2 * c + 1):
            kv = h // group
            rows.append(pv[kv * HEAD_DIM:(kv + 1) * HEAD_DIM, h * t:(h + 1) * t])
        ot = jnp.concatenate(rows, axis=0)
        gate = g_ref[0, :, c * LANES:(c + 1) * LANES].astype(F32)
        o_ref[0, :, c * LANES:(c + 1) * LANES] = (ot.T * gate).astype(BF16)


def _swa_attention(u, vt, sinks):
    B, S, _ = u.shape
    t = KV_BLOCK
    per = t // WINDOW
    qb, gb, kb = OFF_QB // B_WIDTH, OFF_GB // B_WIDTH, OFF_KB // LANES
    assert OFF_QB % B_WIDTH == 0 and OFF_GB % B_WIDTH == 0
    vgroup = A_WIDTH // LANES
    return pl.pallas_call(
        functools.partial(_swa_kernel, t=t),
        grid=(B, S // t),
        in_specs=[
            pl.BlockSpec(memory_space=pltpu.SMEM),
            pl.BlockSpec((1, t, B_WIDTH), lambda b, i: (b, i, qb)),
            pl.BlockSpec((1, t, LANES), lambda b, i: (b, i, kb)),
            pl.BlockSpec((1, WINDOW, LANES), lambda b, i: (b, jnp.maximum(i * per - 1, 0), kb)),
            pl.BlockSpec((1, 1, 1, LANES, t), lambda b, i: (b, vgroup, i, 0, 0)),
            pl.BlockSpec((1, 1, 1, LANES, t), lambda b, i: (b, vgroup, jnp.maximum(i - 1, 0), 0, 0)),
            pl.BlockSpec((1, t, B_WIDTH), lambda b, i: (b, i, gb)),
        ],
        out_specs=pl.BlockSpec((1, t, B_WIDTH), lambda b, i: (b, i, 0)),
        out_shape=jax.ShapeDtypeStruct((B, S, B_WIDTH), BF16),
        scratch_shapes=[pltpu.VMEM((B_HEADS * t, LANES), BF16)],
        compiler_params=_cparams(("parallel", "parallel")),
        name="swa_attention",
    )(sinks.astype(F32), u, u, u, vt, vt, u)


def _stick_kernel(tri_ref, q_ref, k_ref, vt_ref, g_ref, o_ref, qs_scr, run_scr, acc_scr, *, t, pairs):
    i = pl.program_id(2)
    heads = 2 * pairs
    width = heads * t
    lane = lax.broadcasted_iota(jnp.int32, (t, LANES), 1)
    for p in range(pairs):
        q = q_ref[0, :, p * LANES:(p + 1) * LANES]
        zero = jnp.zeros_like(q)
        qs_scr[(2 * p) * t:(2 * p + 1) * t] = jnp.where(lane < HEAD_DIM, q, zero)
        qs_scr[(2 * p + 1) * t:(2 * p + 2) * t] = jnp.where(lane >= HEAD_DIM, q, zero)
    tri = tri_ref[...]
    acc_scr[...] = jnp.zeros((LANES, width), F32)

    def prepare(j, diagonal):
        off = pl.multiple_of(j * t, t)
        z = jnp.concatenate([
            lax.dot_general(k_ref[0, pl.ds(off, t), p * LANES:(p + 1) * LANES],
                            qs_scr[2 * p * t:(2 * p + 2) * t],
                            (((1,), (1,)), ((), ())), preferred_element_type=F32)
            for p in range(pairs)], axis=1)
        if diagonal:
            key = lax.broadcasted_iota(jnp.int32, (t, width), 0)
            qry = lax.broadcasted_iota(jnp.int32, (t, width), 1) & (t - 1)
            z = jnp.where(key < qry, z, NEG_BIG)
        sp = jnp.maximum(z, 0.0) + jnp.log(1.0 + jnp.exp2(-jnp.abs(z))) * LOG2E
        csum = jnp.dot(tri, sp.astype(BF16), preferred_element_type=F32)
        return z, csum

    def apply(j, z, csum, run):
        w = jnp.exp2(z - csum - run).astype(BF16)
        for p in range(pairs):
            cols = slice(2 * p * t, (2 * p + 2) * t)
            acc_scr[:, cols] += jnp.dot(vt_ref[0, p, j], w[:, cols], preferred_element_type=F32)
        return run + csum[0:1, :]

    prev = jnp.maximum(i - 1, 0)
    zd, cd = prepare(i, True)
    zp, cp = prepare(prev, False)
    run = apply(i, zd, cd, jnp.zeros((1, width), F32))
    run = apply(prev, zp, cp, run + jnp.where(i > 0, 0.0, -NEG_BIG))
    run_scr[...] = run

    def cond(c):
        jj, go = c
        return jnp.logical_and(jj < i - 1, go)

    def body(c):
        jj, _ = c
        j = i - 2 - jj
        z, csum = prepare(j, False)
        run = apply(j, z, csum, run_scr[...])
        run_scr[...] = run
        return jj + 1, jnp.min(run) < EXP2_UNDERFLOW

    lax.while_loop(cond, body, (jnp.int32(0), jnp.min(run) < EXP2_UNDERFLOW))
    first = lax.broadcasted_iota(jnp.int32, (LANES, t), 0) < HEAD_DIM
    for p in range(pairs):
        a = 2 * p * t
        ot = jnp.where(first, acc_scr[:, a:a + t], acc_scr[:, a + t:a + 2 * t])
        gate = g_ref[0, :, p * LANES:(p + 1) * LANES].astype(F32)
        o_ref[0, :, p * LANES:(p + 1) * LANES] = (ot.T * gate).astype(BF16)


def _stick_attention(u, vt, pairs):
    B, S, _ = u.shape
    t = KV_BLOCK
    width = pairs * LANES
    groups = C_WIDTH // width
    row = jnp.arange(t)
    tri = (row[None, :] >= row[:, None]).astype(BF16)
    return pl.pallas_call(
        functools.partial(_stick_kernel, t=t, pairs=pairs),
        grid=(B, groups, S // t),
        in_specs=[
            pl.BlockSpec((t, t), lambda b, h, i: (0, 0)),
            pl.BlockSpec((1, t, width), lambda b, h, i: (b, i, h)),
            pl.BlockSpec((1, S, width), lambda b, h, i: (b, 0, groups + h)),
            pl.BlockSpec((1, pairs, S // t, LANES, t), lambda b, h, i: (b, h, 0, 0, 0)),
            pl.BlockSpec((1, t, width), lambda b, h, i: (b, i, 2 * groups + h)),
        ],
        out_specs=pl.BlockSpec((1, t, width), lambda b, h, i: (b, i, h)),
        out_shape=jax.ShapeDtypeStruct((B, S, C_WIDTH), BF16),
        scratch_shapes=[pltpu.VMEM((2 * pairs * t, LANES), BF16),
                        pltpu.VMEM((1, 2 * pairs * t), F32),
                        pltpu.VMEM((LANES, 2 * pairs * t), F32)],
        compiler_params=_cparams(("parallel", "parallel", "arbitrary")),
        name="stick_attention",
    )(tri, u, u, vt, u)


def _tile(n, pref):
    t = min(pref, n)
    assert n % t == 0
    return t


def _even_layer(x, c, positions, layer, norm_g, w_mod, b_mod, w_in, a_q_gain, a_k_gain,
                lq1, lk1, lq2, lk2, a_subln_g, b_q_gain, b_k_gain, b_sinks, w_out):
    B, S, D = x.shape
    mod = _modulation(c, w_mod, b_mod)
    ones = lambda n: jnp.ones((n,), F32)
    gain_cols = jnp.concatenate([
        jnp.tile(a_q_gain * (QK_SCALE * LOG2E), A_QK // HEAD_DIM),
        jnp.tile(a_k_gain, A_QK // HEAD_DIM),
        ones(A_WIDTH),
        jnp.tile(b_q_gain * (QK_SCALE * LOG2E), B_HEADS),
        ones(B_WIDTH),
        jnp.tile(b_k_gain, B_KV_HEADS),
    ]).astype(F32).reshape(1, EVEN_IN)
    u, vat = _inproj_even(x, mod, norm_g, w_in, positions, gain_cols, _tile(S, 512))
    lambda_init = 0.8 - 0.6 * math.exp(-0.3 * layer)
    lam_rows = jnp.stack([lq1, lk1, lq2, lk2]).astype(F32)
    logit_bound = (HEAD_DIM * QK_SCALE * LOG2E * LOGIT_BOUND_MARGIN
                   * jnp.max(jnp.abs(a_q_gain)) * jnp.max(jnp.abs(a_k_gain)))
    ya = _diff_attention(u, vat, logit_bound, lam_rows, a_subln_g.astype(F32), lambda_init,
                         _tile(S, 1024))
    yb = _swa_attention(u, vat, b_sinks)
    return _outproj([ya, yb], w_out, x, mod, _tile(S, 512))


def _odd_layer(x, c, norm_g, w_mod, b_mod, w_in, w_out):
    B, S, D = x.shape
    mod = _modulation(c, w_mod, b_mod)
    u, vt = _inproj_odd(x, mod, norm_g, w_in, _tile(S, 512))
    y = _stick_attention(u, vt, pairs=2)
    return _outproj([y], w_out, x, mod, _tile(S, 512))


def kernel(x, c, positions, even_norm_g, even_w_mod, even_b_mod, even_w_in, a_q_gain, a_k_gain,
           a_lambda_q1, a_lambda_k1, a_lambda_q2, a_lambda_k2, a_subln_g, b_q_gain, b_k_gain,
           b_sinks, even_w_out, odd_norm_g, odd_w_mod, odd_b_mod, odd_w_in, odd_w_out):
    depth = even_norm_g.shape[0] + odd_norm_g.shape[0]
    for layer in range(depth):
        j = layer // 2
        if layer % 2 == 0:
            x = _even_layer(x, c, positions, layer, even_norm_g[j], even_w_mod[j], even_b_mod[j],
                            even_w_in[j], a_q_gain[j], a_k_gain[j], a_lambda_q1[j], a_lambda_k1[j],
                            a_lambda_q2[j], a_lambda_k2[j], a_subln_g[j], b_q_gain[j], b_k_gain[j],
                            b_sinks[j], even_w_out[j])
        else:
            x = _odd_layer(x, c, odd_norm_g[j], odd_w_mod[j], odd_b_mod[j], odd_w_in[j], odd_w_out[j])
    return x
```

```python
import functools
import math

import jax
import jax.numpy as jnp
from jax import lax
from jax.experimental import pallas as pl
from jax.experimental.pallas import tpu as pltpu

F32 = jnp.float32
BF16 = jnp.bfloat16

HEAD_DIM = 64
ROPE_THETA = 10000.0
EPS = 1e-6
WINDOW = 128
LANES = 128
QK_SCALE = HEAD_DIM ** -0.5

A_HEADS = 4
A_QK = A_HEADS * 2 * HEAD_DIM
A_WIDTH = A_HEADS * 2 * HEAD_DIM
B_HEADS = 8
B_KV_HEADS = 2
B_WIDTH = B_HEADS * HEAD_DIM
B_KV = B_KV_HEADS * HEAD_DIM
C_HEADS = 16
C_WIDTH = C_HEADS * HEAD_DIM

W_QA = 0
W_KA = W_QA + A_QK
W_VA = W_KA + A_QK
W_GA = W_VA + A_WIDTH
W_QB = W_GA + A_WIDTH
W_KB = W_QB + B_WIDTH
W_VB = W_KB + B_KV
W_GB = W_VB + B_KV
W_END = W_GB + B_WIDTH

OFF_QA = 0
OFF_KA = OFF_QA + A_QK
OFF_GA = OFF_KA + A_QK
OFF_QB = OFF_GA + A_WIDTH
OFF_GB = OFF_QB + B_WIDTH
OFF_KB = OFF_GB + B_WIDTH
EVEN_IN = OFF_KB + B_KV

EVEN_CHUNKS = (
    (W_GA, OFF_GA, 512, "silu"), (W_GB, OFF_GB, 512, "silu"),
    (W_QA, OFF_QA, 256, "qk"), (W_QA + 256, OFF_QA + 256, 256, "qk"),
    (W_KA, OFF_KA, 256, "qk"), (W_KA + 256, OFF_KA + 256, 256, "qk"),
    (W_QB, OFF_QB, 256, "qk"), (W_QB + 256, OFF_QB + 256, 256, "qk"),
    (W_KB, OFF_KB, 128, "qk"),
)
LOG2E = 1.4426950408889634
KV_BLOCK = 256
ODD_IN = 3 * C_WIDTH
ODD_CHUNKS = tuple(
    (w_i * C_WIDTH + half * 512, o_i * C_WIDTH + half * 512, 512, kind)
    for o_i, (w_i, kind) in enumerate(((0, "scale"), (1, "plain"), (3, "silu")))
    for half in range(2)
)

NEG_BIG = -1e30
EXP2_UNDERFLOW = 151.0
LOGIT_BOUND = 64.0
LOGIT_BOUND_MARGIN = 1.01
VMEM_LIMIT = 48 * 1024 * 1024


def _cparams(sem):
    return pltpu.CompilerParams(dimension_semantics=sem, vmem_limit_bytes=VMEM_LIMIT)


def _mod_kernel(c_ref, w_ref, b_ref, o_ref):
    c = c_ref[...]
    sc = c * jax.nn.sigmoid(c)
    o_ref[0] = jnp.dot(sc, w_ref[...], preferred_element_type=F32,
                       precision=lax.Precision.HIGHEST) + b_ref[0]


def _modulation(c, w_mod, b_mod):
    B, D = c.shape
    rows = 8
    cp = jnp.pad(c, ((0, rows - B), (0, 0)))
    out = pl.pallas_call(
        _mod_kernel,
        grid=(3,),
        in_specs=[
            pl.BlockSpec((rows, D), lambda j: (0, 0)),
            pl.BlockSpec((D, D), lambda j: (0, j)),
            pl.BlockSpec((1, 1, D), lambda j: (j, 0, 0)),
        ],
        out_specs=pl.BlockSpec((1, rows, D), lambda j: (j, 0, 0)),
        out_shape=jax.ShapeDtypeStruct((3, rows, D), F32),
        compiler_params=_cparams(("arbitrary",)),
        name="modulation",
    )(cp, w_mod, b_mod.reshape(3, 1, D))
    return out[:, :B].reshape(3, B, 1, D)


def _project_out(y_refs, w_ref):
    acc = None
    off = 0
    for y_ref in y_refs:
        wd = y_ref.shape[-1]
        part = jnp.dot(y_ref[0], w_ref[off:off + wd, :], preferred_element_type=F32)
        acc = part if acc is None else acc + part
        off += wd
    return acc


def _modulated_rows(x, mod_ref, ng_ref):
    ms = jnp.mean(x * x, axis=-1, keepdims=True)
    y = x * lax.rsqrt(ms + EPS) * ng_ref[...]
    return y * (1.0 + mod_ref[1, 0]) + mod_ref[0, 0]


def _store_transposed_values(wvt_ref, h_scr, vt_ref, tm):
    vt = lax.dot_general(wvt_ref[...], h_scr[...], (((1,), (1,)), ((), ())),
                         preferred_element_type=F32)
    for g in range(vt.shape[0] // LANES):
        for cb in range(tm // KV_BLOCK):
            vt_ref[0, g, cb] = vt[g * LANES:(g + 1) * LANES,
                                  cb * KV_BLOCK:(cb + 1) * KV_BLOCK].astype(BF16)


def _inproj_even_kernel(x_ref, mod_ref, ng_ref, w_ref, wvt_ref, pos_ref, inv_ref, gain_ref, p_ref,
                        o_ref, vt_ref, h_scr, *, tm):
    h_scr[...] = _modulated_rows(x_ref[0], mod_ref, ng_ref).astype(BF16)
    _store_transposed_values(wvt_ref, h_scr, vt_ref, tm)
    ang = pos_ref[0].astype(F32) * inv_ref[...]
    cos = jnp.cos(ang)
    sin = jnp.sin(ang)
    lane = lax.broadcasted_iota(jnp.int32, (tm, LANES), 1)
    first = (lane & (HEAD_DIM // 2)) == 0
    sin_s = jnp.where(first, -sin, sin)
    for wcol, start, width, kind in EVEN_CHUNKS:
        u = jnp.dot(h_scr[...], w_ref[:, wcol:wcol + width], preferred_element_type=F32)
        if kind == "qk":
            ms = jnp.dot((u * u).astype(BF16), p_ref[:width, :width], preferred_element_type=F32)
            un = u * lax.rsqrt(ms + EPS) * gain_ref[:, start:start + width]
            for s in range(width // LANES):
                xs = un[:, s * LANES:(s + 1) * LANES]
                rot = jnp.where(first, pltpu.roll(xs, LANES - HEAD_DIM // 2, 1),
                                pltpu.roll(xs, HEAD_DIM // 2, 1))
                c0 = start + s * LANES
                o_ref[0, :, c0:c0 + LANES] = (xs * cos + rot * sin_s).astype(BF16)
        elif kind == "silu":
            o_ref[0, :, start:start + width] = (u * jax.nn.sigmoid(u)).astype(BF16)
        else:
            o_ref[0, :, start:start + width] = u.astype(BF16)


def _inproj_odd_kernel(*refs, tm, n_y):
    if n_y:
        y_refs, (wout_ref, modp_ref) = refs[:n_y], refs[n_y:n_y + 2]
        x_ref, mod_ref, ng_ref, w_ref, wvt_ref, x1_ref, o_ref, vt_ref, h_scr = refs[n_y + 2:]
        x = x_ref[0] + modp_ref[2, 0] * _project_out(y_refs, wout_ref)
        x1_ref[0] = x
    else:
        x_ref, mod_ref, ng_ref, w_ref, wvt_ref, o_ref, vt_ref, h_scr = refs
        x = x_ref[0]
    h_scr[...] = _modulated_rows(x, mod_ref, ng_ref).astype(BF16)
    _store_transposed_values(wvt_ref, h_scr, vt_ref, tm)
    for wcol, start, width, kind in ODD_CHUNKS:
        u = jnp.dot(h_scr[...], w_ref[:, wcol:wcol + width], preferred_element_type=F32)
        if kind == "scale":
            u = u * (QK_SCALE * LOG2E)
        elif kind == "silu":
            u = u * jax.nn.sigmoid(u)
        o_ref[0, :, start:start + width] = u.astype(BF16)


def _inproj_even(x, mod, norm_g, w_in, positions, gain_cols, tm):
    B, S, D = x.shape
    w_rows = w_in.astype(BF16)
    w_vat = jnp.concatenate([w_in[:, W_VA:W_GA], w_in[:, W_VB:W_GB]], axis=1).T.astype(BF16)
    n_groups = (A_WIDTH + B_KV) // LANES
    half = HEAD_DIM // 2
    inv = ROPE_THETA ** (-jnp.arange(half, dtype=F32) / half)
    inv_cols = jnp.tile(inv, LANES // half).reshape(1, LANES)
    blk = jnp.arange(256) // HEAD_DIM
    avg = jnp.where(blk[:, None] == blk[None, :], 1.0 / HEAD_DIM, 0.0).astype(BF16)
    return pl.pallas_call(
        functools.partial(_inproj_even_kernel, tm=tm),
        grid=(B, S // tm),
        in_specs=[
            pl.BlockSpec((1, tm, D), lambda b, i: (b, i, 0)),
            pl.BlockSpec((3, 1, 1, D), lambda b, i: (0, b, 0, 0)),
            pl.BlockSpec((1, D), lambda b, i: (0, 0)),
            pl.BlockSpec((D, W_END), lambda b, i: (0, 0)),
            pl.BlockSpec((A_WIDTH + B_KV, D), lambda b, i: (0, 0)),
            pl.BlockSpec((1, tm, 1), lambda b, i: (b, i, 0)),
            pl.BlockSpec((1, LANES), lambda b, i: (0, 0)),
            pl.BlockSpec((1, EVEN_IN), lambda b, i: (0, 0)),
            pl.BlockSpec((256, 256), lambda b, i: (0, 0)),
        ],
        out_specs=[
            pl.BlockSpec((1, tm, EVEN_IN), lambda b, i: (b, i, 0)),
            pl.BlockSpec((1, n_groups, tm // KV_BLOCK, LANES, KV_BLOCK), lambda b, i: (b, 0, i, 0, 0)),
        ],
        out_shape=[
            jax.ShapeDtypeStruct((B, S, EVEN_IN), BF16),
            jax.ShapeDtypeStruct((B, n_groups, S // KV_BLOCK, LANES, KV_BLOCK), BF16),
        ],
        scratch_shapes=[pltpu.VMEM((tm, D), BF16)],
        compiler_params=_cparams(("parallel", "parallel")),
        name="inproj_even",
    )(x, mod, norm_g.reshape(1, D), w_rows, w_vat, positions.reshape(B, S, 1),
      inv_cols, gain_cols, avg)


def _inproj_odd(x, mod, norm_g, w_in, tm, pending=None):
    B, S, D = x.shape
    w_rows = w_in.astype(BF16)
    w_vt = w_in[:, 2 * C_WIDTH:3 * C_WIDTH].T.astype(BF16)
    n_groups = C_WIDTH // LANES
    row_tile = lambda width: pl.BlockSpec((1, tm, width), lambda b, i: (b, i, 0))
    mod_spec = pl.BlockSpec((3, 1, 1, D), lambda b, i: (0, b, 0, 0))
    const = lambda shape: pl.BlockSpec(shape, lambda b, i: (0,) * len(shape),
                                       pipeline_mode=pl.Buffered(1))
    in_specs = [row_tile(D), mod_spec, const((1, D)), const((D, 4 * C_WIDTH)), const((C_WIDTH, D))]
    args = [x, mod, norm_g.reshape(1, D), w_rows, w_vt]
    out_specs = [
        row_tile(ODD_IN),
        pl.BlockSpec((1, n_groups, tm // KV_BLOCK, LANES, KV_BLOCK), lambda b, i: (b, 0, i, 0, 0)),
    ]
    out_shape = [
        jax.ShapeDtypeStruct((B, S, ODD_IN), BF16),
        jax.ShapeDtypeStruct((B, n_groups, S // KV_BLOCK, LANES, KV_BLOCK), BF16),
    ]
    n_y = 0
    if pending is not None:
        ys, w_out, mod_prev = pending
        n_y = len(ys)
        in_specs = [row_tile(y.shape[-1]) for y in ys] + [const(w_out.shape), mod_spec] + in_specs
        args = list(ys) + [w_out.astype(BF16), mod_prev] + args
        out_specs = [row_tile(D)] + out_specs
        out_shape = [jax.ShapeDtypeStruct((B, S, D), F32)] + out_shape
    outs = pl.pallas_call(
        functools.partial(_inproj_odd_kernel, tm=tm, n_y=n_y),
        grid=(B, S // tm),
        in_specs=in_specs,
        out_specs=out_specs,
        out_shape=out_shape,
        scratch_shapes=[pltpu.VMEM((tm, D), BF16)],
        compiler_params=_cparams(("parallel", "parallel")),
        name="inproj_odd",
    )(*args)
    return (outs[0], outs[1], outs[2]) if n_y else (x, outs[0], outs[1])


def _outproj_kernel(*refs, n_in):
    y_refs = refs[:n_in]
    w_ref, x_ref, mod_ref, o_ref = refs[n_in:]
    o_ref[0] = x_ref[0] + mod_ref[2, 0] * _project_out(y_refs, w_ref)


def _outproj(ys, w_out, x, mod, tm):
    B, S, D = x.shape
    K = w_out.shape[0]
    n_in = len(ys)
    in_specs = [pl.BlockSpec((1, tm, y.shape[-1]), lambda b, i: (b, i, 0)) for y in ys]
    in_specs += [
        pl.BlockSpec((K, D), lambda b, i: (0, 0)),
        pl.BlockSpec((1, tm, D), lambda b, i: (b, i, 0)),
        pl.BlockSpec((3, 1, 1, D), lambda b, i: (0, b, 0, 0)),
    ]
    return pl.pallas_call(
        functools.partial(_outproj_kernel, n_in=n_in),
        grid=(B, S // tm),
        in_specs=in_specs,
        out_specs=pl.BlockSpec((1, tm, D), lambda b, i: (b, i, 0)),
        out_shape=jax.ShapeDtypeStruct((B, S, D), F32),
        compiler_params=_cparams(("parallel", "parallel")),
        name="outproj",
    )(*ys, w_out.astype(BF16), x, mod)


def _diff_attn_kernel(bound_ref, lam_ref, sub_ref, q_ref, k_ref, vt_ref, g_ref, o_ref,
                      qs_scr, s0_scr, s1_scr, smax0_scr, smax1_scr, m_scr, l_scr, acc_scr, *,
                      tq, tk, lambda_init):
    i = pl.program_id(2)
    q = q_ref[0]
    lane = lax.broadcasted_iota(jnp.int32, (tq, LANES), 1)
    zero = jnp.zeros_like(q)
    qs_scr[:tq] = jnp.where(lane < HEAD_DIM, q, zero)
    qs_scr[tq:] = jnp.where(lane >= HEAD_DIM, q, zero)

    slots = ((s0_scr, smax0_scr), (s1_scr, smax1_scr))

    def spans(diag):
        if diag is None:
            return ((0, 2 * tq),)
        return tuple((c * tq + diag * tk, (c + 1) * tq) for c in range(2))

    def scores(j, slot, bounded, diag=None):
        s_ref, smax_ref = slots[slot]
        k = k_ref[0, pl.ds(pl.multiple_of(j * tk, tk), tk), :]
        for a, b in spans(diag):
            s = lax.dot_general(k, qs_scr[a:b], (((1,), (1,)), ((), ())),
                                preferred_element_type=F32)
            s_ref[:, a:b] = s
            if not bounded and diag is None:
                smax_ref[:, a:b] = jnp.max(s, axis=0, keepdims=True)

    def softmax_pv(j, slot, bounded, diag=None):
        s_ref, smax_ref = slots[slot]
        for a, b in spans(diag):
            s = s_ref[:, a:b]
            if diag is not None:
                key = lax.broadcasted_iota(jnp.int32, (tk, b - a), 0) + diag * tk
                qry = (lax.broadcasted_iota(jnp.int32, (tk, b - a), 1) + a) & (tq - 1)
                s = jnp.where(key <= qry, s, NEG_BIG)
            if bounded:
                p = jnp.exp2(s)
                l_scr[:, a:b] += jnp.sum(p, axis=0, keepdims=True)
                acc_scr[:, a:b] += jnp.dot(vt_ref[0, 0, j], p.astype(BF16),
                                           preferred_element_type=F32)
                continue
            smax = smax_ref[:, a:b] if diag is None else jnp.max(s, axis=0, keepdims=True)
            m = m_scr[:, a:b]
            m_new = jnp.maximum(m, smax)
            alpha = jnp.exp2(m - m_new)
            p = jnp.exp2(s - m_new)
            l_scr[:, a:b] = alpha * l_scr[:, a:b] + jnp.sum(p, axis=0, keepdims=True)
            m_scr[:, a:b] = m_new
            pv = jnp.dot(vt_ref[0, 0, j], p.astype(BF16), preferred_element_type=F32)
            acc_scr[:, a:b] = alpha * acc_scr[:, a:b] + pv

    def sweep(bounded):
        m_scr[...] = jnp.full((1, 2 * tq), NEG_BIG, F32)
        l_scr[...] = jnp.zeros((1, 2 * tq), F32)
        acc_scr[...] = jnp.zeros((LANES, 2 * tq), F32)
        n_diag = tq // tk
        n_full = n_diag * i
        scores(0, 0, bounded)

        def body(jj, carry):
            j0 = n_diag * jj
            for d in range(n_diag):
                scores(j0 + d + 1, (d + 1) % 2, bounded)
                softmax_pv(j0 + d, d % 2, bounded)
            return carry

        lax.fori_loop(0, i, body, 0)
        for d in range(n_diag):
            if d + 1 < n_diag:
                scores(n_full + d + 1, (d + 1) % 2, bounded, diag=d + 1)
            softmax_pv(n_full + d, d % 2, bounded, diag=d)

    is_bounded = bound_ref[0] <= LOGIT_BOUND
    pl.when(is_bounded)(lambda: sweep(True))
    pl.when(jnp.logical_not(is_bounded))(lambda: sweep(False))
    l = l_scr[...]
    acc = acc_scr[...]

    lam_rows = lam_ref[...]
    lam = (jnp.exp(jnp.sum(lam_rows[0:1] * lam_rows[1:2], axis=1, keepdims=True))
           - jnp.exp(jnp.sum(lam_rows[2:3] * lam_rows[3:4], axis=1, keepdims=True))
           + lambda_init)
    ot = acc[:, :tq] / l[:, :tq] - lam * (acc[:, tq:] / l[:, tq:])
    ms = jnp.mean(ot * ot, axis=0, keepdims=True)
    yt = ot * lax.rsqrt(ms + EPS) * (sub_ref[...] * (1.0 - lambda_init))
    o_ref[0] = (yt.T * g_ref[0].astype(F32)).astype(BF16)


def _diff_attention(u, vat, logit_bound, lam_rows, subln_g, lambda_init, tq):
    B, S, _ = u.shape
    tk = KV_BLOCK
    assert tq % (2 * tk) == 0 and S % tq == 0
    qb, kb, gb = OFF_QA // LANES, OFF_KA // LANES, OFF_GA // LANES
    stat = pltpu.VMEM((1, 2 * tq), F32)
    score = pltpu.VMEM((tk, 2 * tq), F32)
    return pl.pallas_call(
        functools.partial(_diff_attn_kernel, tq=tq, tk=tk, lambda_init=lambda_init),
        grid=(B, A_HEADS, S // tq),
        in_specs=[
            pl.BlockSpec(memory_space=pltpu.SMEM),
            pl.BlockSpec((4, HEAD_DIM), lambda b, h, i: (0, 0)),
            pl.BlockSpec((LANES, 1), lambda b, h, i: (0, 0)),
            pl.BlockSpec((1, tq, LANES), lambda b, h, i: (b, i, qb + h)),
            pl.BlockSpec((1, S, LANES), lambda b, h, i: (b, 0, kb + h)),
            pl.BlockSpec((1, 1, S // tk, LANES, tk), lambda b, h, i: (b, h, 0, 0, 0)),
            pl.BlockSpec((1, tq, LANES), lambda b, h, i: (b, i, gb + h)),
        ],
        out_specs=pl.BlockSpec((1, tq, LANES), lambda b, h, i: (b, i, h)),
        out_shape=jax.ShapeDtypeStruct((B, S, A_WIDTH), BF16),
        scratch_shapes=[pltpu.VMEM((2 * tq, LANES), BF16), score, score, stat, stat, stat, stat,
                        pltpu.VMEM((LANES, 2 * tq), F32)],
        compiler_params=_cparams(("parallel", "parallel", "arbitrary")),
        name="diff_attention",
    )(logit_bound.reshape(1).astype(F32), lam_rows, subln_g.reshape(LANES, 1), u, u, vat, u)


def _swa_kernel(sink_ref, q_ref, kc_ref, kp_ref, vtc_ref, vtp_ref, g_ref, o_ref, qs_scr, *, t):
    i = pl.program_id(1)
    group = B_HEADS // B_KV_HEADS
    width = B_HEADS * t
    lane = lax.broadcasted_iota(jnp.int32, (t, LANES), 1)
    for h in range(B_HEADS):
        kv = h // group
        qc = q_ref[0, :, (h // 2) * LANES:(h // 2 + 1) * LANES].astype(F32)
        if h % 2 != kv:
            qc = pltpu.roll(qc, HEAD_DIM, 1)
        keep = (lane >= kv * HEAD_DIM) & (lane < (kv + 1) * HEAD_DIM)
        qs_scr[h * t:(h + 1) * t] = jnp.where(keep, qc, 0.0).astype(BF16)
    kk = jnp.concatenate([kp_ref[0], kc_ref[0]], axis=0)
    vvt = jnp.concatenate([vtp_ref[0, 0, 0][:, KV_BLOCK - WINDOW:], vtc_ref[0, 0, 0]], axis=1)
    s = lax.dot_general(kk, qs_scr[...], (((1,), (1,)), ((), ())),
                        preferred_element_type=F32)
    key = lax.broadcasted_iota(jnp.int32, (WINDOW + t, width), 0)
    seg = lax.broadcasted_iota(jnp.int32, (WINDOW + t, width), 1)
    qry = seg & (t - 1)
    mask = (key > qry) & (key <= qry + WINDOW) & jnp.logical_or(i > 0, key >= WINDOW)
    s = jnp.where(mask, s, NEG_BIG)
    head = lax.broadcasted_iota(jnp.int32, (1, width), 1) // t
    sink = jnp.zeros((1, width), F32)
    for h in range(B_HEADS):
        sink = jnp.where(head == h, sink_ref[h] * LOG2E, sink)
    m = jnp.maximum(jnp.max(s, axis=0, keepdims=True), sink)
    e = jnp.exp2(s - m)
    den = jnp.sum(e, axis=0, keepdims=True) + jnp.exp2(sink - m)
    pv = jnp.dot(vvt, e.astype(BF16), preferred_element_type=F32) / den
    for c in range(B_HEADS // 2):
        rows = []
        for h in (2 * c, 2 * c + 1):
            kv = h // group
            rows.append(pv[kv * HEAD_DIM:(kv + 1) * HEAD_DIM, h * t:(h + 1) * t])
        ot = jnp.concatenate(rows, axis=0)
        gate = g_ref[0, :, c * LANES:(c + 1) * LANES].astype(F32)
        o_ref[0, :, c * LANES:(c + 1) * LANES] = (ot.T * gate).astype(BF16)


def _swa_attention(u, vt, sinks):
    B, S, _ = u.shape
    t = KV_BLOCK
    per = t // WINDOW
    qb, gb, kb = OFF_QB // B_WIDTH, OFF_GB // B_WIDTH, OFF_KB // LANES
    assert OFF_QB % B_WIDTH == 0 and OFF_GB % B_WIDTH == 0
    vgroup = A_WIDTH // LANES
    return pl.pallas_call(
        functools.partial(_swa_kernel, t=t),
        grid=(B, S // t),
        in_specs=[
            pl.BlockSpec(memory_space=pltpu.SMEM),
            pl.BlockSpec((1, t, B_WIDTH), lambda b, i: (b, i, qb)),
            pl.BlockSpec((1, t, LANES), lambda b, i: (b, i, kb)),
            pl.BlockSpec((1, WINDOW, LANES), lambda b, i: (b, jnp.maximum(i * per - 1, 0), kb)),
            pl.BlockSpec((1, 1, 1, LANES, t), lambda b, i: (b, vgroup, i, 0, 0)),
            pl.BlockSpec((1, 1, 1, LANES, t), lambda b, i: (b, vgroup, jnp.maximum(i - 1, 0), 0, 0)),
            pl.BlockSpec((1, t, B_WIDTH), lambda b, i: (b, i, gb)),
        ],
        out_specs=pl.BlockSpec((1, t, B_WIDTH), lambda b, i: (b, i, 0)),
        out_shape=jax.ShapeDtypeStruct((B, S, B_WIDTH), BF16),
        scratch_shapes=[pltpu.VMEM((B_HEADS * t, LANES), BF16)],
        compiler_params=_cparams(("parallel", "parallel")),
        name="swa_attention",
    )(sinks.astype(F32), u, u, u, vt, vt, u)


def _stick_kernel(tri_ref, q_ref, k_ref, vt_ref, g_ref, o_ref, qs_scr, run_scr, acc_scr, *, t, pairs):
    i = pl.program_id(2)
    heads = 2 * pairs
    width = heads * t
    lane = lax.broadcasted_iota(jnp.int32, (t, LANES), 1)
    for p in range(pairs):
        q = q_ref[0, :, p * LANES:(p + 1) * LANES]
        zero = jnp.zeros_like(q)
        qs_scr[(2 * p) * t:(2 * p + 1) * t] = jnp.where(lane < HEAD_DIM, q, zero)
        qs_scr[(2 * p + 1) * t:(2 * p + 2) * t] = jnp.where(lane >= HEAD_DIM, q, zero)
    tri = tri_ref[...]
    acc_scr[...] = jnp.zeros((LANES, width), F32)

    def prepare(j, diagonal):
        off = pl.multiple_of(j * t, t)
        z = jnp.concatenate([
            lax.dot_general(k_ref[0, pl.ds(off, t), p * LANES:(p + 1) * LANES],
                            qs_scr[2 * p * t:(2 * p + 2) * t],
                            (((1,), (1,)), ((), ())), preferred_element_type=F32)
            for p in range(pairs)], axis=1)
        if diagonal:
            key = lax.broadcasted_iota(jnp.int32, (t, width), 0)
            qry = lax.broadcasted_iota(jnp.int32, (t, width), 1) & (t - 1)
            z = jnp.where(key < qry, z, NEG_BIG)
        sp = jnp.maximum(z, 0.0) + jnp.log(1.0 + jnp.exp2(-jnp.abs(z))) * LOG2E
        csum = jnp.dot(tri, sp.astype(BF16), preferred_element_type=F32)
        return z, csum

    def apply(j, z, csum, run):
        w = jnp.exp2(z - csum - run).astype(BF16)
        for p in range(pairs):
            cols = slice(2 * p * t, (2 * p + 2) * t)
            acc_scr[:, cols] += jnp.dot(vt_ref[0, p, j], w[:, cols], preferred_element_type=F32)
        return run + csum[0:1, :]

    prev = jnp.maximum(i - 1, 0)
    zd, cd = prepare(i, True)
    zp, cp = prepare(prev, False)
    run = apply(i, zd, cd, jnp.zeros((1, width), F32))
    run = apply(prev, zp, cp, run + jnp.where(i > 0, 0.0, -NEG_BIG))
    run_scr[...] = run

    def cond(c):
        jj, go = c
        return jnp.logical_and(jj < i - 1, go)

    def body(c):
        jj, _ = c
        j = i - 2 - jj
        z, csum = prepare(j, False)
        run = apply(j, z, csum, run_scr[...])
        run_scr[...] = run
        return jj + 1, jnp.min(run) < EXP2_UNDERFLOW

    lax.while_loop(cond, body, (jnp.int32(0), jnp.min(run) < EXP2_UNDERFLOW))
    first = lax.broadcasted_iota(jnp.int32, (LANES, t), 0) < HEAD_DIM
    for p in range(pairs):
        a = 2 * p * t
        ot = jnp.where(first, acc_scr[:, a:a + t], acc_scr[:, a + t:a + 2 * t])
        gate = g_ref[0, :, p * LANES:(p + 1) * LANES].astype(F32)
        o_ref[0, :, p * LANES:(p + 1) * LANES] = (ot.T * gate).astype(BF16)


def _stick_attention(u, vt, pairs):
    B, S, _ = u.shape
    t = KV_BLOCK
    width = pairs * LANES
    groups = C_WIDTH // width
    row = jnp.arange(t)
    tri = (row[None, :] >= row[:, None]).astype(BF16)
    return pl.pallas_call(
        functools.partial(_stick_kernel, t=t, pairs=pairs),
        grid=(B, groups, S // t),
        in_specs=[
            pl.BlockSpec((t, t), lambda b, h, i: (0, 0)),
            pl.BlockSpec((1, t, width), lambda b, h, i: (b, i, h)),
            pl.BlockSpec((1, S, width), lambda b, h, i: (b, 0, groups + h)),
            pl.BlockSpec((1, pairs, S // t, LANES, t), lambda b, h, i: (b, h, 0, 0, 0)),
            pl.BlockSpec((1, t, width), lambda b, h, i: (b, i, 2 * groups + h)),
        ],
        out_specs=pl.BlockSpec((1, t, width), lambda b, h, i: (b, i, h)),
        out_shape=jax.ShapeDtypeStruct((B, S, C_WIDTH), BF16),
        scratch_shapes=[pltpu.VMEM((2 * pairs * t, LANES), BF16),
                        pltpu.VMEM((1, 2 * pairs * t), F32),
                        pltpu.VMEM((LANES, 2 * pairs * t), F32)],
        compiler_params=_cparams(("parallel", "parallel", "arbitrary")),
        name="stick_attention",
    )(tri, u, u, vt, u)


def _tile(n, pref):
    t = min(pref, n)
    assert n % t == 0
    return t


def _even_layer(x, c, positions, layer, norm_g, w_mod, b_mod, w_in, a_q_gain, a_k_gain,
                lq1, lk1, lq2, lk2, a_subln_g, b_q_gain, b_k_gain, b_sinks, w_out):
    B, S, D = x.shape
    mod = _modulation(c, w_mod, b_mod)
    ones = lambda n: jnp.ones((n,), F32)
    gain_cols = jnp.concatenate([
        jnp.tile(a_q_gain * (QK_SCALE * LOG2E), A_QK // HEAD_DIM),
        jnp.tile(a_k_gain, A_QK // HEAD_DIM),
        ones(A_WIDTH),
        jnp.tile(b_q_gain * (QK_SCALE * LOG2E), B_HEADS),
        ones(B_WIDTH),
        jnp.tile(b_k_gain, B_KV_HEADS),
    ]).astype(F32).reshape(1, EVEN_IN)
    u, vat = _inproj_even(x, mod, norm_g, w_in, positions, gain_cols, _tile(S, 512))
    lambda_init = 0.8 - 0.6 * math.exp(-0.3 * layer)
    lam_rows = jnp.stack([lq1, lk1, lq2, lk2]).astype(F32)
    logit_bound = (HEAD_DIM * QK_SCALE * LOG2E * LOGIT_BOUND_MARGIN
                   * jnp.max(jnp.abs(a_q_gain)) * jnp.max(jnp.abs(a_k_gain)))
    ya = _diff_attention(u, vat, logit_bound, lam_rows, a_subln_g.astype(F32), lambda_init,
                         _tile(S, 2048))
    yb = _swa_attention(u, vat, b_sinks)
    return x, ([ya, yb], w_out, mod)


def _odd_layer(x, pending, c, norm_g, w_mod, b_mod, w_in, w_out):
    B, S, D = x.shape
    mod = _modulation(c, w_mod, b_mod)
    x, u, vt = _inproj_odd(x, mod, norm_g, w_in, _tile(S, 512), pending)
    y = _stick_attention(u, vt, pairs=2)
    return x, ([y], w_out, mod)


def kernel(x, c, positions, even_norm_g, even_w_mod, even_b_mod, even_w_in, a_q_gain, a_k_gain,
           a_lambda_q1, a_lambda_k1, a_lambda_q2, a_lambda_k2, a_subln_g, b_q_gain, b_k_gain,
           b_sinks, even_w_out, odd_norm_g, odd_w_mod, odd_b_mod, odd_w_in, odd_w_out):
    depth = even_norm_g.shape[0] + odd_norm_g.shape[0]
    tm = _tile(x.shape[1], 512)
    pending = None
    for layer in range(depth):
        j = layer // 2
        if layer % 2 == 0:
            if pending is not None:
                x = _outproj(pending[0], pending[1], x, pending[2], tm)
            x, pending = _even_layer(
                x, c, positions, layer, even_norm_g[j], even_w_mod[j], even_b_mod[j],
                even_w_in[j], a_q_gain[j], a_k_gain[j], a_lambda_q1[j], a_lambda_k1[j],
                a_lambda_q2[j], a_lambda_k2[j], a_subln_g[j], b_q_gain[j], b_k_gain[j],
                b_sinks[j], even_w_out[j])
        else:
            x, pending = _odd_layer(x, pending, c, odd_norm_g[j], odd_w_mod[j], odd_b_mod[j],
                                    odd_w_in[j], odd_w_out[j])
    return _outproj(pending[0], pending[1], x, pending[2], tm)
```

```python
import functools
import math

import jax
import jax.numpy as jnp
from jax import lax
from jax.experimental import pallas as pl
from jax.experimental.pallas import tpu as pltpu

F32 = jnp.float32
BF16 = jnp.bfloat16

HEAD_DIM = 64
ROPE_THETA = 10000.0
EPS = 1e-6
WINDOW = 128
LANES = 128
QK_SCALE = HEAD_DIM ** -0.5

A_HEADS = 4
A_QK = A_HEADS * 2 * HEAD_DIM
A_WIDTH = A_HEADS * 2 * HEAD_DIM
B_HEADS = 8
B_KV_HEADS = 2
B_WIDTH = B_HEADS * HEAD_DIM
B_KV = B_KV_HEADS * HEAD_DIM
C_HEADS = 16
C_WIDTH = C_HEADS * HEAD_DIM

W_QA = 0
W_KA = W_QA + A_QK
W_VA = W_KA + A_QK
W_GA = W_VA + A_WIDTH
W_QB = W_GA + A_WIDTH
W_KB = W_QB + B_WIDTH
W_VB = W_KB + B_KV
W_GB = W_VB + B_KV
W_END = W_GB + B_WIDTH

OFF_QA = 0
OFF_KA = OFF_QA + A_QK
OFF_GA = OFF_KA + A_QK
OFF_QB = OFF_GA + A_WIDTH
OFF_GB = OFF_QB + B_WIDTH
OFF_KB = OFF_GB + B_WIDTH
EVEN_IN = OFF_KB + B_KV

EVEN_CHUNKS = (
    (W_GA, OFF_GA, 512, "silu"), (W_GB, OFF_GB, 512, "silu"),
    (W_QA, OFF_QA, 256, "qk"), (W_QA + 256, OFF_QA + 256, 256, "qk"),
    (W_KA, OFF_KA, 256, "qk"), (W_KA + 256, OFF_KA + 256, 256, "qk"),
    (W_QB, OFF_QB, 256, "qk"), (W_QB + 256, OFF_QB + 256, 256, "qk"),
    (W_KB, OFF_KB, 128, "qk"),
)
LOG2E = 1.4426950408889634
KV_BLOCK = 256
ODD_IN = 3 * C_WIDTH
ODD_CHUNKS = tuple(
    (w_i * C_WIDTH + half * 512, o_i * C_WIDTH + half * 512, 512, kind)
    for o_i, (w_i, kind) in enumerate(((0, "scale"), (1, "plain"), (3, "silu")))
    for half in range(2)
)

NEG_BIG = -1e30
EXP2_UNDERFLOW = 151.0
LOGIT_BOUND = 64.0
LOGIT_BOUND_MARGIN = 1.01
VMEM_LIMIT = 48 * 1024 * 1024


def _cparams(sem):
    return pltpu.CompilerParams(dimension_semantics=sem, vmem_limit_bytes=VMEM_LIMIT)


def _cast_kernel(w_ref, o_ref):
    o_ref[...] = w_ref[...].astype(BF16)


def _to_bf16(w, rows=128):
    K, N = w.shape
    return pl.pallas_call(
        _cast_kernel,
        grid=(K // rows,),
        in_specs=[pl.BlockSpec((rows, N), lambda r: (r, 0))],
        out_specs=pl.BlockSpec((rows, N), lambda r: (r, 0)),
        out_shape=jax.ShapeDtypeStruct((K, N), BF16),
        compiler_params=_cparams(("parallel",)),
        name="weight_cast",
    )(w)


def _mod_kernel(c_ref, w_ref, b_ref, o_ref):
    c = c_ref[...]
    sc = c * jax.nn.sigmoid(c)
    o_ref[0] = jnp.dot(sc, w_ref[...], preferred_element_type=F32,
                       precision=lax.Precision.HIGHEST) + b_ref[0]


def _modulation(c, w_mod, b_mod):
    B, D = c.shape
    rows = 8
    cp = jnp.pad(c, ((0, rows - B), (0, 0)))
    out = pl.pallas_call(
        _mod_kernel,
        grid=(3,),
        in_specs=[
            pl.BlockSpec((rows, D), lambda j: (0, 0)),
            pl.BlockSpec((D, D), lambda j: (0, j)),
            pl.BlockSpec((1, 1, D), lambda j: (j, 0, 0)),
        ],
        out_specs=pl.BlockSpec((1, rows, D), lambda j: (j, 0, 0)),
        out_shape=jax.ShapeDtypeStruct((3, rows, D), F32),
        compiler_params=_cparams(("arbitrary",)),
        name="modulation",
    )(cp, w_mod, b_mod.reshape(3, 1, D))
    return out[:, :B].reshape(3, B, 1, D)


def _project_out(y_refs, w_ref):
    acc = None
    off = 0
    for y_ref in y_refs:
        wd = y_ref.shape[-1]
        part = jnp.dot(y_ref[0], w_ref[off:off + wd, :], preferred_element_type=F32)
        acc = part if acc is None else acc + part
        off += wd
    return acc


def _modulated_rows(x, mod_ref, ng_ref):
    ms = jnp.mean(x * x, axis=-1, keepdims=True)
    y = x * lax.rsqrt(ms + EPS) * ng_ref[...]
    return y * (1.0 + mod_ref[1, 0]) + mod_ref[0, 0]


def _store_transposed_values(wvt_ref, h_scr, vt_ref, tm):
    vt = lax.dot_general(wvt_ref[...], h_scr[...], (((1,), (1,)), ((), ())),
                         preferred_element_type=F32)
    for g in range(vt.shape[0] // LANES):
        for cb in range(tm // KV_BLOCK):
            vt_ref[0, g, cb] = vt[g * LANES:(g + 1) * LANES,
                                  cb * KV_BLOCK:(cb + 1) * KV_BLOCK].astype(BF16)


def _inproj_even_kernel(x_ref, mod_ref, ng_ref, w_ref, wvt_ref, pos_ref, inv_ref, gain_ref, p_ref,
                        o_ref, vt_ref, h_scr, *, tm):
    h_scr[...] = _modulated_rows(x_ref[0], mod_ref, ng_ref).astype(BF16)
    _store_transposed_values(wvt_ref, h_scr, vt_ref, tm)
    ang_t = pos_ref[0].astype(F32) * inv_ref[...]
    reps = LANES // ang_t.shape[0]
    cos = jnp.concatenate([jnp.cos(ang_t)] * reps, axis=0).T
    sin = jnp.concatenate([jnp.sin(ang_t)] * reps, axis=0).T
    lane = lax.broadcasted_iota(jnp.int32, (tm, LANES), 1)
    first = (lane & (HEAD_DIM // 2)) == 0
    sin_s = jnp.where(first, -sin, sin)
    for wcol, start, width, kind in EVEN_CHUNKS:
        u = jnp.dot(h_scr[...], w_ref[:, wcol:wcol + width], preferred_element_type=F32)
        if kind == "qk":
            ms = jnp.dot((u * u).astype(BF16), p_ref[:width, :width], preferred_element_type=F32)
            un = u * lax.rsqrt(ms + EPS) * gain_ref[:, start:start + width]
            for s in range(width // LANES):
                xs = un[:, s * LANES:(s + 1) * LANES]
                rot = jnp.where(first, pltpu.roll(xs, LANES - HEAD_DIM // 2, 1),
                                pltpu.roll(xs, HEAD_DIM // 2, 1))
                c0 = start + s * LANES
                o_ref[0, :, c0:c0 + LANES] = (xs * cos + rot * sin_s).astype(BF16)
        elif kind == "silu":
            o_ref[0, :, start:start + width] = (u * jax.nn.sigmoid(u)).astype(BF16)
        else:
            o_ref[0, :, start:start + width] = u.astype(BF16)


def _inproj_odd_kernel(*refs, tm, n_y):
    if n_y:
        y_refs, (wout_ref, modp_ref) = refs[:n_y], refs[n_y:n_y + 2]
        x_ref, mod_ref, ng_ref, w_ref, wvt_ref, x1_ref, o_ref, vt_ref, h_scr = refs[n_y + 2:]
        x = x_ref[0] + modp_ref[2, 0] * _project_out(y_refs, wout_ref)
        x1_ref[0] = x
    else:
        x_ref, mod_ref, ng_ref, w_ref, wvt_ref, o_ref, vt_ref, h_scr = refs
        x = x_ref[0]
    h_scr[...] = _modulated_rows(x, mod_ref, ng_ref).astype(BF16)
    _store_transposed_values(wvt_ref, h_scr, vt_ref, tm)
    for wcol, start, width, kind in ODD_CHUNKS:
        u = jnp.dot(h_scr[...], w_ref[:, wcol:wcol + width], preferred_element_type=F32)
        if kind == "scale":
            u = u * (QK_SCALE * LOG2E)
        elif kind == "silu":
            u = u * jax.nn.sigmoid(u)
        o_ref[0, :, start:start + width] = u.astype(BF16)


def _inproj_even(x, mod, norm_g, w_in, positions, gain_cols, tm):
    B, S, D = x.shape
    w_rows = _to_bf16(w_in)
    w_vat = _to_bf16(jnp.concatenate([w_in[:, W_VA:W_GA], w_in[:, W_VB:W_GB]], axis=1).T)
    n_groups = (A_WIDTH + B_KV) // LANES
    half = HEAD_DIM // 2
    inv = (ROPE_THETA ** (-jnp.arange(half, dtype=F32) / half)).reshape(half, 1)
    blk = jnp.arange(256) // HEAD_DIM
    avg = jnp.where(blk[:, None] == blk[None, :], 1.0 / HEAD_DIM, 0.0).astype(BF16)
    return pl.pallas_call(
        functools.partial(_inproj_even_kernel, tm=tm),
        grid=(B, S // tm),
        in_specs=[
            pl.BlockSpec((1, tm, D), lambda b, i: (b, i, 0)),
            pl.BlockSpec((3, 1, 1, D), lambda b, i: (0, b, 0, 0)),
            pl.BlockSpec((1, D), lambda b, i: (0, 0)),
            pl.BlockSpec((D, W_END), lambda b, i: (0, 0)),
            pl.BlockSpec((A_WIDTH + B_KV, D), lambda b, i: (0, 0)),
            pl.BlockSpec((1, 1, tm), lambda b, i: (b, 0, i)),
            pl.BlockSpec((half, 1), lambda b, i: (0, 0)),
            pl.BlockSpec((1, EVEN_IN), lambda b, i: (0, 0)),
            pl.BlockSpec((256, 256), lambda b, i: (0, 0)),
        ],
        out_specs=[
            pl.BlockSpec((1, tm, EVEN_IN), lambda b, i: (b, i, 0)),
            pl.BlockSpec((1, n_groups, tm // KV_BLOCK, LANES, KV_BLOCK), lambda b, i: (b, 0, i, 0, 0)),
        ],
        out_shape=[
            jax.ShapeDtypeStruct((B, S, EVEN_IN), BF16),
            jax.ShapeDtypeStruct((B, n_groups, S // KV_BLOCK, LANES, KV_BLOCK), BF16),
        ],
        scratch_shapes=[pltpu.VMEM((tm, D), BF16)],
        compiler_params=_cparams(("parallel", "parallel")),
        name="inproj_even",
    )(x, mod, norm_g.reshape(1, D), w_rows, w_vat, positions.reshape(B, 1, S),
      inv, gain_cols, avg)


def _inproj_odd(x, mod, norm_g, w_in, tm, pending=None):
    B, S, D = x.shape
    w_rows = _to_bf16(w_in)
    w_vt = _to_bf16(w_in[:, 2 * C_WIDTH:3 * C_WIDTH].T)
    n_groups = C_WIDTH // LANES
    row_tile = lambda width: pl.BlockSpec((1, tm, width), lambda b, i: (b, i, 0))
    mod_spec = pl.BlockSpec((3, 1, 1, D), lambda b, i: (0, b, 0, 0))
    const = lambda shape: pl.BlockSpec(shape, lambda b, i: (0,) * len(shape),
                                       pipeline_mode=pl.Buffered(1))
    in_specs = [row_tile(D), mod_spec, const((1, D)), const((D, 4 * C_WIDTH)), const((C_WIDTH, D))]
    args = [x, mod, norm_g.reshape(1, D), w_rows, w_vt]
    out_specs = [
        row_tile(ODD_IN),
        pl.BlockSpec((1, n_groups, tm // KV_BLOCK, LANES, KV_BLOCK), lambda b, i: (b, 0, i, 0, 0)),
    ]
    out_shape = [
        jax.ShapeDtypeStruct((B, S, ODD_IN), BF16),
        jax.ShapeDtypeStruct((B, n_groups, S // KV_BLOCK, LANES, KV_BLOCK), BF16),
    ]
    n_y = 0
    if pending is not None:
        ys, w_out, mod_prev = pending
        n_y = len(ys)
        in_specs = [row_tile(y.shape[-1]) for y in ys] + [const(w_out.shape), mod_spec] + in_specs
        args = list(ys) + [w_out.astype(BF16), mod_prev] + args
        out_specs = [row_tile(D)] + out_specs
        out_shape = [jax.ShapeDtypeStruct((B, S, D), F32)] + out_shape
    outs = pl.pallas_call(
        functools.partial(_inproj_odd_kernel, tm=tm, n_y=n_y),
        grid=(B, S // tm),
        in_specs=in_specs,
        out_specs=out_specs,
        out_shape=out_shape,
        scratch_shapes=[pltpu.VMEM((tm, D), BF16)],
        compiler_params=_cparams(("parallel", "parallel")),
        name="inproj_odd",
    )(*args)
    return (outs[0], outs[1], outs[2]) if n_y else (x, outs[0], outs[1])


def _outproj_kernel(*refs, n_in):
    y_refs = refs[:n_in]
    w_ref, x_ref, mod_ref, o_ref = refs[n_in:]
    o_ref[0] = x_ref[0] + mod_ref[2, 0] * _project_out(y_refs, w_ref)


def _outproj(ys, w_out, x, mod, tm):
    B, S, D = x.shape
    K = w_out.shape[0]
    n_in = len(ys)
    in_specs = [pl.BlockSpec((1, tm, y.shape[-1]), lambda b, i: (b, i, 0)) for y in ys]
    in_specs += [
        pl.BlockSpec((K, D), lambda b, i: (0, 0)),
        pl.BlockSpec((1, tm, D), lambda b, i: (b, i, 0)),
        pl.BlockSpec((3, 1, 1, D), lambda b, i: (0, b, 0, 0)),
    ]
    return pl.pallas_call(
        functools.partial(_outproj_kernel, n_in=n_in),
        grid=(B, S // tm),
        in_specs=in_specs,
        out_specs=pl.BlockSpec((1, tm, D), lambda b, i: (b, i, 0)),
        out_shape=jax.ShapeDtypeStruct((B, S, D), F32),
        compiler_params=_cparams(("parallel", "parallel")),
        name="outproj",
    )(*ys, w_out.astype(BF16), x, mod)


def _diff_attn_kernel(bound_ref, lam_ref, sub_ref, q_ref, k_ref, vt_ref, g_ref, o_ref,
                      qs_scr, s0_scr, s1_scr, smax0_scr, smax1_scr, m_scr, l_scr, acc_scr, *,
                      tq, tk, lambda_init):
    i = pl.program_id(2)
    q = q_ref[0]
    lane = lax.broadcasted_iota(jnp.int32, (tq, LANES), 1)
    zero = jnp.zeros_like(q)
    qs_scr[:tq] = jnp.where(lane < HEAD_DIM, q, zero)
    qs_scr[tq:] = jnp.where(lane >= HEAD_DIM, q, zero)

    slots = ((s0_scr, smax0_scr), (s1_scr, smax1_scr))

    def spans(diag):
        if diag is None:
            return ((0, 2 * tq),)
        return tuple((c * tq + diag * tk, (c + 1) * tq) for c in range(2))

    def scores(j, slot, bounded, diag=None):
        s_ref, smax_ref = slots[slot]
        k = k_ref[0, pl.ds(pl.multiple_of(j * tk, tk), tk), :]
        for a, b in spans(diag):
            s = lax.dot_general(k, qs_scr[a:b], (((1,), (1,)), ((), ())),
                                preferred_element_type=F32)
            s_ref[:, a:b] = s
            if not bounded and diag is None:
                smax_ref[:, a:b] = jnp.max(s, axis=0, keepdims=True)

    def softmax_pv(j, slot, bounded, diag=None):
        s_ref, smax_ref = slots[slot]
        for a, b in spans(diag):
            s = s_ref[:, a:b]
            if diag is not None:
                key = lax.broadcasted_iota(jnp.int32, (tk, b - a), 0) + diag * tk
                qry = (lax.broadcasted_iota(jnp.int32, (tk, b - a), 1) + a) & (tq - 1)
                s = jnp.where(key <= qry, s, NEG_BIG)
            if bounded:
                p = jnp.exp2(s)
                l_scr[:, a:b] += jnp.sum(p, axis=0, keepdims=True)
                acc_scr[:, a:b] += jnp.dot(vt_ref[0, 0, j], p.astype(BF16),
                                           preferred_element_type=F32)
                continue
            smax = smax_ref[:, a:b] if diag is None else jnp.max(s, axis=0, keepdims=True)
            m = m_scr[:, a:b]
            m_new = jnp.maximum(m, smax)
            alpha = jnp.exp2(m - m_new)
            p = jnp.exp2(s - m_new)
            l_scr[:, a:b] = alpha * l_scr[:, a:b] + jnp.sum(p, axis=0, keepdims=True)
            m_scr[:, a:b] = m_new
            pv = jnp.dot(vt_ref[0, 0, j], p.astype(BF16), preferred_element_type=F32)
            acc_scr[:, a:b] = alpha * acc_scr[:, a:b] + pv

    def sweep(bounded):
        m_scr[...] = jnp.full((1, 2 * tq), NEG_BIG, F32)
        l_scr[...] = jnp.zeros((1, 2 * tq), F32)
        acc_scr[...] = jnp.zeros((LANES, 2 * tq), F32)
        n_diag = tq // tk
        n_full = n_diag * i
        scores(0, 0, bounded)

        def body(jj, carry):
            j0 = n_diag * jj
            for d in range(n_diag):
                scores(j0 + d + 1, (d + 1) % 2, bounded)
                softmax_pv(j0 + d, d % 2, bounded)
            return carry

        lax.fori_loop(0, i, body, 0)
        for d in range(n_diag):
            if d + 1 < n_diag:
                scores(n_full + d + 1, (d + 1) % 2, bounded, diag=d + 1)
            softmax_pv(n_full + d, d % 2, bounded, diag=d)

    is_bounded = bound_ref[0] <= LOGIT_BOUND
    pl.when(is_bounded)(lambda: sweep(True))
    pl.when(jnp.logical_not(is_bounded))(lambda: sweep(False))
    l = l_scr[...]
    acc = acc_scr[...]

    lam_rows = lam_ref[...]
    lam = (jnp.exp(jnp.sum(lam_rows[0:1] * lam_rows[1:2], axis=1, keepdims=True))
           - jnp.exp(jnp.sum(lam_rows[2:3] * lam_rows[3:4], axis=1, keepdims=True))
           + lambda_init)
    ot = acc[:, :tq] / l[:, :tq] - lam * (acc[:, tq:] / l[:, tq:])
    ms = jnp.mean(ot * ot, axis=0, keepdims=True)
    yt = ot * lax.rsqrt(ms + EPS) * (sub_ref[...] * (1.0 - lambda_init))
    o_ref[0] = (yt.T * g_ref[0].astype(F32)).astype(BF16)


def _diff_attention(u, vat, logit_bound, lam_rows, subln_g, lambda_init, tq):
    B, S, _ = u.shape
    tk = KV_BLOCK
    assert tq % (2 * tk) == 0 and S % tq == 0
    qb, kb, gb = OFF_QA // LANES, OFF_KA // LANES, OFF_GA // LANES
    stat = pltpu.VMEM((1, 2 * tq), F32)
    score = pltpu.VMEM((tk, 2 * tq), F32)
    return pl.pallas_call(
        functools.partial(_diff_attn_kernel, tq=tq, tk=tk, lambda_init=lambda_init),
        grid=(B, A_HEADS, S // tq),
        in_specs=[
            pl.BlockSpec(memory_space=pltpu.SMEM),
            pl.BlockSpec((4, HEAD_DIM), lambda b, h, i: (0, 0)),
            pl.BlockSpec((LANES, 1), lambda b, h, i: (0, 0)),
            pl.BlockSpec((1, tq, LANES), lambda b, h, i: (b, i, qb + h)),
            pl.BlockSpec((1, S, LANES), lambda b, h, i: (b, 0, kb + h)),
            pl.BlockSpec((1, 1, S // tk, LANES, tk), lambda b, h, i: (b, h, 0, 0, 0)),
            pl.BlockSpec((1, tq, LANES), lambda b, h, i: (b, i, gb + h)),
        ],
        out_specs=pl.BlockSpec((1, tq, LANES), lambda b, h, i: (b, i, h)),
        out_shape=jax.ShapeDtypeStruct((B, S, A_WIDTH), BF16),
        scratch_shapes=[pltpu.VMEM((2 * tq, LANES), BF16), score, score, stat, stat, stat, stat,
                        pltpu.VMEM((LANES, 2 * tq), F32)],
        compiler_params=_cparams(("parallel", "parallel", "arbitrary")),
        name="diff_attention",
    )(logit_bound.reshape(1).astype(F32), lam_rows, subln_g.reshape(LANES, 1), u, u, vat, u)


def _swa_kernel(sink_ref, q_ref, kc_ref, kp_ref, vtc_ref, vtp_ref, g_ref, o_ref, qs_scr, *, t):
    i = pl.program_id(1)
    group = B_HEADS // B_KV_HEADS
    width = B_HEADS * t
    lane = lax.broadcasted_iota(jnp.int32, (t, LANES), 1)
    for h in range(B_HEADS):
        kv = h // group
        qc = q_ref[0, :, (h // 2) * LANES:(h // 2 + 1) * LANES].astype(F32)
        if h % 2 != kv:
            qc = pltpu.roll(qc, HEAD_DIM, 1)
        keep = (lane >= kv * HEAD_DIM) & (lane < (kv + 1) * HEAD_DIM)
        qs_scr[h * t:(h + 1) * t] = jnp.where(keep, qc, 0.0).astype(BF16)
    kk = jnp.concatenate([kp_ref[0], kc_ref[0]], axis=0)
    vvt = jnp.concatenate([vtp_ref[0, 0, 0][:, KV_BLOCK - WINDOW:], vtc_ref[0, 0, 0]], axis=1)
    s = lax.dot_general(kk, qs_scr[...], (((1,), (1,)), ((), ())),
                        preferred_element_type=F32)
    key = lax.broadcasted_iota(jnp.int32, (WINDOW + t, width), 0)
    seg = lax.broadcasted_iota(jnp.int32, (WINDOW + t, width), 1)
    qry = seg & (t - 1)
    mask = (key > qry) & (key <= qry + WINDOW) & jnp.logical_or(i > 0, key >= WINDOW)
    s = jnp.where(mask, s, NEG_BIG)
    head = lax.broadcasted_iota(jnp.int32, (1, width), 1) // t
    sink = jnp.zeros((1, width), F32)
    for h in range(B_HEADS):
        sink = jnp.where(head == h, sink_ref[h] * LOG2E, sink)
    m = jnp.maximum(jnp.max(s, axis=0, keepdims=True), sink)
    e = jnp.exp2(s - m)
    den = jnp.sum(e, axis=0, keepdims=True) + jnp.exp2(sink - m)
    pv = jnp.dot(vvt, e.astype(BF16), preferred_element_type=F32) / den
    for c in range(B_HEADS // 2):
        rows = []
        for h in (2 * c, 2 * c + 1):
            kv = h // group
            rows.append(pv[kv * HEAD_DIM:(kv + 1) * HEAD_DIM, h * t:(h + 1) * t])
        ot = jnp.concatenate(rows, axis=0)
        gate = g_ref[0, :, c * LANES:(c + 1) * LANES].astype(F32)
        o_ref[0, :, c * LANES:(c + 1) * LANES] = (ot.T * gate).astype(BF16)


def _swa_attention(u, vt, sinks):
    B, S, _ = u.shape
    t = KV_BLOCK
    per = t // WINDOW
    qb, gb, kb = OFF_QB // B_WIDTH, OFF_GB // B_WIDTH, OFF_KB // LANES
    assert OFF_QB % B_WIDTH == 0 and OFF_GB % B_WIDTH == 0
    vgroup = A_WIDTH // LANES
    return pl.pallas_call(
        functools.partial(_swa_kernel, t=t),
        grid=(B, S // t),
        in_specs=[
            pl.BlockSpec(memory_space=pltpu.SMEM),
            pl.BlockSpec((1, t, B_WIDTH), lambda b, i: (b, i, qb)),
            pl.BlockSpec((1, t, LANES), lambda b, i: (b, i, kb)),
            pl.BlockSpec((1, WINDOW, LANES), lambda b, i: (b, jnp.maximum(i * per - 1, 0), kb)),
            pl.BlockSpec((1, 1, 1, LANES, t), lambda b, i: (b, vgroup, i, 0, 0)),
            pl.BlockSpec((1, 1, 1, LANES, t), lambda b, i: (b, vgroup, jnp.maximum(i - 1, 0), 0, 0)),
            pl.BlockSpec((1, t, B_WIDTH), lambda b, i: (b, i, gb)),
        ],
        out_specs=pl.BlockSpec((1, t, B_WIDTH), lambda b, i: (b, i, 0)),
        out_shape=jax.ShapeDtypeStruct((B, S, B_WIDTH), BF16),
        scratch_shapes=[pltpu.VMEM((B_HEADS * t, LANES), BF16)],
        compiler_params=_cparams(("parallel", "parallel")),
        name="swa_attention",
    )(sinks.astype(F32), u, u, u, vt, vt, u)


def _stick_kernel(tri_ref, q_ref, k_ref, vt_ref, g_ref, o_ref, qs_scr, run_scr, acc_scr, *, t, pairs):
    i = pl.program_id(2)
    heads = 2 * pairs
    width = heads * t
    lane = lax.broadcasted_iota(jnp.int32, (t, LANES), 1)
    for p in range(pairs):
        q = q_ref[0, :, p * LANES:(p + 1) * LANES]
        zero = jnp.zeros_like(q)
        qs_scr[(2 * p) * t:(2 * p + 1) * t] = jnp.where(lane < HEAD_DIM, q, zero)
        qs_scr[(2 * p + 1) * t:(2 * p + 2) * t] = jnp.where(lane >= HEAD_DIM, q, zero)
    tri = tri_ref[...]
    acc_scr[...] = jnp.zeros((LANES, width), F32)

    def prepare(j, diagonal):
        off = pl.multiple_of(j * t, t)
        z = jnp.concatenate([
            lax.dot_general(k_ref[0, pl.ds(off, t), p * LANES:(p + 1) * LANES],
                            qs_scr[2 * p * t:(2 * p + 2) * t],
                            (((1,), (1,)), ((), ())), preferred_element_type=F32)
            for p in range(pairs)], axis=1)
        if diagonal:
            key = lax.broadcasted_iota(jnp.int32, (t, width), 0)
            qry = lax.broadcasted_iota(jnp.int32, (t, width), 1) & (t - 1)
            z = jnp.where(key < qry, z, NEG_BIG)
        sp = jnp.maximum(z, 0.0) + jnp.log(1.0 + jnp.exp2(-jnp.abs(z))) * LOG2E
        csum = jnp.dot(tri, sp.astype(BF16), preferred_element_type=F32)
        return z, csum

    def apply(j, z, csum, run):
        w = jnp.exp2(z - csum - run).astype(BF16)
        for p in range(pairs):
            cols = slice(2 * p * t, (2 * p + 2) * t)
            acc_scr[:, cols] += jnp.dot(vt_ref[0, p, j], w[:, cols], preferred_element_type=F32)
        return run + csum[0:1, :]

    prev = jnp.maximum(i - 1, 0)
    zd, cd = prepare(i, True)
    zp, cp = prepare(prev, False)
    run = apply(i, zd, cd, jnp.zeros((1, width), F32))
    run = apply(prev, zp, cp, run + jnp.where(i > 0, 0.0, -NEG_BIG))
    run_scr[...] = run

    def cond(c):
        jj, go = c
        return jnp.logical_and(jj < i - 1, go)

    def body(c):
        jj, _ = c
        j = i - 2 - jj
        z, csum = prepare(j, False)
        run = apply(j, z, csum, run_scr[...])
        run_scr[...] = run
        return jj + 1, jnp.min(run) < EXP2_UNDERFLOW

    lax.while_loop(cond, body, (jnp.int32(0), jnp.min(run) < EXP2_UNDERFLOW))
    first = lax.broadcasted_iota(jnp.int32, (LANES, t), 0) < HEAD_DIM
    for p in range(pairs):
        a = 2 * p * t
        ot = jnp.where(first, acc_scr[:, a:a + t], acc_scr[:, a + t:a + 2 * t])
        gate = g_ref[0, :, p * LANES:(p + 1) * LANES].astype(F32)
        o_ref[0, :, p * LANES:(p + 1) * LANES] = (ot.T * gate).astype(BF16)


def _stick_attention(u, vt, pairs):
    B, S, _ = u.shape
    t = KV_BLOCK
    width = pairs * LANES
    groups = C_WIDTH // width
    row = jnp.arange(t)
    tri = (row[None, :] >= row[:, None]).astype(BF16)
    return pl.pallas_call(
        functools.partial(_stick_kernel, t=t, pairs=pairs),
        grid=(B, groups, S // t),
        in_specs=[
            pl.BlockSpec((t, t), lambda b, h, i: (0, 0)),
            pl.BlockSpec((1, t, width), lambda b, h, i: (b, i, h)),
            pl.BlockSpec((1, S, width), lambda b, h, i: (b, 0, groups + h)),
            pl.BlockSpec((1, pairs, S // t, LANES, t), lambda b, h, i: (b, h, 0, 0, 0)),
            pl.BlockSpec((1, t, width), lambda b, h, i: (b, i, 2 * groups + h)),
        ],
        out_specs=pl.BlockSpec((1, t, width), lambda b, h, i: (b, i, h)),
        out_shape=jax.ShapeDtypeStruct((B, S, C_WIDTH), BF16),
        scratch_shapes=[pltpu.VMEM((2 * pairs * t, LANES), BF16),
                        pltpu.VMEM((1, 2 * pairs * t), F32),
                        pltpu.VMEM((LANES, 2 * pairs * t), F32)],
        compiler_params=_cparams(("parallel", "parallel", "arbitrary")),
        name="stick_attention",
    )(tri, u, u, vt, u)


def _tile(n, pref):
    t = min(pref, n)
    assert n % t == 0
    return t


def _even_layer(x, c, positions, layer, norm_g, w_mod, b_mod, w_in, a_q_gain, a_k_gain,
                lq1, lk1, lq2, lk2, a_subln_g, b_q_gain, b_k_gain, b_sinks, w_out):
    B, S, D = x.shape
    mod = _modulation(c, w_mod, b_mod)
    ones = lambda n: jnp.ones((n,), F32)
    gain_cols = jnp.concatenate([
        jnp.tile(a_q_gain * (QK_SCALE * LOG2E), A_QK // HEAD_DIM),
        jnp.tile(a_k_gain, A_QK // HEAD_DIM),
        ones(A_WIDTH),
        jnp.tile(b_q_gain * (QK_SCALE * LOG2E), B_HEADS),
        ones(B_WIDTH),
        jnp.tile(b_k_gain, B_KV_HEADS),
    ]).astype(F32).reshape(1, EVEN_IN)
    u, vat = _inproj_even(x, mod, norm_g, w_in, positions, gain_cols, _tile(S, 512))
    lambda_init = 0.8 - 0.6 * math.exp(-0.3 * layer)
    lam_rows = jnp.stack([lq1, lk1, lq2, lk2]).astype(F32)
    logit_bound = (HEAD_DIM * QK_SCALE * LOG2E * LOGIT_BOUND_MARGIN
                   * jnp.max(jnp.abs(a_q_gain)) * jnp.max(jnp.abs(a_k_gain)))
    ya = _diff_attention(u, vat, logit_bound, lam_rows, a_subln_g.astype(F32), lambda_init,
                         _tile(S, 2048))
    yb = _swa_attention(u, vat, b_sinks)
    return x, ([ya, yb], w_out, mod)


def _odd_layer(x, pending, c, norm_g, w_mod, b_mod, w_in, w_out):
    B, S, D = x.shape
    mod = _modulation(c, w_mod, b_mod)
    x, u, vt = _inproj_odd(x, mod, norm_g, w_in, _tile(S, 512), pending)
    y = _stick_attention(u, vt, pairs=2)
    return x, ([y], w_out, mod)


def kernel(x, c, positions, even_norm_g, even_w_mod, even_b_mod, even_w_in, a_q_gain, a_k_gain,
           a_lambda_q1, a_lambda_k1, a_lambda_q2, a_lambda_k2, a_subln_g, b_q_gain, b_k_gain,
           b_sinks, even_w_out, odd_norm_g, odd_w_mod, odd_b_mod, odd_w_in, odd_w_out):
    depth = even_norm_g.shape[0] + odd_norm_g.shape[0]
    tm = _tile(x.shape[1], 512)
    pending = None
    for layer in range(depth):
        j = layer // 2
        if layer % 2 == 0:
            if pending is not None:
                x = _outproj(pending[0], pending[1], x, pending[2], tm)
            x, pending = _even_layer(
                x, c, positions, layer, even_norm_g[j], even_w_mod[j], even_b_mod[j],
                even_w_in[j], a_q_gain[j], a_k_gain[j], a_lambda_q1[j], a_lambda_k1[j],
                a_lambda_q2[j], a_lambda_k2[j], a_subln_g[j], b_q_gain[j], b_k_gain[j],
                b_sinks[j], even_w_out[j])
        else:
            x, pending = _odd_layer(x, pending, c, odd_norm_g[j], odd_w_mod[j], odd_b_mod[j],
                                    odd_w_in[j], odd_w_out[j])
    return _outproj(pending[0], pending[1], x, pending[2], tm)
```

```python
import functools
import math

import jax
import jax.numpy as jnp
from jax import lax
from jax.experimental import pallas as pl
from jax.experimental.pallas import tpu as pltpu

F32 = jnp.float32
BF16 = jnp.bfloat16

HEAD_DIM = 64
ROPE_THETA = 10000.0
EPS = 1e-6
WINDOW = 128
LANES = 128
QK_SCALE = HEAD_DIM ** -0.5

A_HEADS = 4
A_QK = A_HEADS * 2 * HEAD_DIM
A_WIDTH = A_HEADS * 2 * HEAD_DIM
B_HEADS = 8
B_KV_HEADS = 2
B_WIDTH = B_HEADS * HEAD_DIM
B_KV = B_KV_HEADS * HEAD_DIM
C_HEADS = 16
C_WIDTH = C_HEADS * HEAD_DIM

W_QA = 0
W_KA = W_QA + A_QK
W_VA = W_KA + A_QK
W_GA = W_VA + A_WIDTH
W_QB = W_GA + A_WIDTH
W_KB = W_QB + B_WIDTH
W_VB = W_KB + B_KV
W_GB = W_VB + B_KV
W_END = W_GB + B_WIDTH

OFF_QA = 0
OFF_KA = OFF_QA + A_QK
OFF_GA = OFF_KA + A_QK
OFF_QB = OFF_GA + A_WIDTH
OFF_GB = OFF_QB + B_WIDTH
OFF_KB = OFF_GB + B_WIDTH
EVEN_IN = OFF_KB + B_KV

EVEN_CHUNKS = (
    (W_GA, OFF_GA, 512, "silu"), (W_GB, OFF_GB, 512, "silu"),
    (W_QA, OFF_QA, 256, "qk"), (W_QA + 256, OFF_QA + 256, 256, "qk"),
    (W_KA, OFF_KA, 256, "qk"), (W_KA + 256, OFF_KA + 256, 256, "qk"),
    (W_QB, OFF_QB, 256, "qk"), (W_QB + 256, OFF_QB + 256, 256, "qk"),
    (W_KB, OFF_KB, 128, "qk"),
)
LOG2E = 1.4426950408889634
KV_BLOCK = 256
ODD_IN = 3 * C_WIDTH
ODD_CHUNKS = tuple(
    (w_i * C_WIDTH + half * 512, o_i * C_WIDTH + half * 512, 512, kind)
    for o_i, (w_i, kind) in enumerate(((0, "scale"), (1, "plain"), (3, "silu")))
    for half in range(2)
)

NEG_BIG = -1e30
EXP2_UNDERFLOW = 151.0
LOGIT_BOUND = 64.0
LOGIT_BOUND_MARGIN = 1.01
VMEM_LIMIT = 48 * 1024 * 1024


def _cparams(sem):
    return pltpu.CompilerParams(dimension_semantics=sem, vmem_limit_bytes=VMEM_LIMIT)


def _cast_kernel(w_ref, o_ref):
    o_ref[...] = w_ref[...].astype(BF16)


def _to_bf16(w, rows=128):
    K, N = w.shape
    return pl.pallas_call(
        _cast_kernel,
        grid=(K // rows,),
        in_specs=[pl.BlockSpec((rows, N), lambda r: (r, 0))],
        out_specs=pl.BlockSpec((rows, N), lambda r: (r, 0)),
        out_shape=jax.ShapeDtypeStruct((K, N), BF16),
        compiler_params=_cparams(("parallel",)),
        name="weight_cast",
    )(w)


def _mod_kernel(c_ref, w_ref, b_ref, o_ref):
    c = c_ref[...]
    sc = c * jax.nn.sigmoid(c)
    o_ref[0] = jnp.dot(sc, w_ref[...], preferred_element_type=F32,
                       precision=lax.Precision.HIGHEST) + b_ref[0]


def _modulation(c, w_mod, b_mod):
    B, D = c.shape
    rows = 8
    cp = jnp.pad(c, ((0, rows - B), (0, 0)))
    out = pl.pallas_call(
        _mod_kernel,
        grid=(3,),
        in_specs=[
            pl.BlockSpec((rows, D), lambda j: (0, 0)),
            pl.BlockSpec((D, D), lambda j: (0, j)),
            pl.BlockSpec((1, 1, D), lambda j: (j, 0, 0)),
        ],
        out_specs=pl.BlockSpec((1, rows, D), lambda j: (j, 0, 0)),
        out_shape=jax.ShapeDtypeStruct((3, rows, D), F32),
        compiler_params=_cparams(("arbitrary",)),
        name="modulation",
    )(cp, w_mod, b_mod.reshape(3, 1, D))
    return out[:, :B].reshape(3, B, 1, D)


def _project_out(y_refs, w_ref):
    acc = None
    off = 0
    for y_ref in y_refs:
        wd = y_ref.shape[-1]
        part = jnp.dot(y_ref[0], w_ref[off:off + wd, :], preferred_element_type=F32)
        acc = part if acc is None else acc + part
        off += wd
    return acc


def _modulated_rows(x, mod_ref, ng_ref):
    ms = jnp.mean(x * x, axis=-1, keepdims=True)
    y = x * lax.rsqrt(ms + EPS) * ng_ref[...]
    return y * (1.0 + mod_ref[1, 0]) + mod_ref[0, 0]


def _store_transposed_values(wvt_ref, h_scr, vt_ref, tm):
    vt = lax.dot_general(wvt_ref[...], h_scr[...], (((1,), (1,)), ((), ())),
                         preferred_element_type=F32)
    for g in range(vt.shape[0] // LANES):
        for cb in range(tm // KV_BLOCK):
            vt_ref[0, g, cb] = vt[g * LANES:(g + 1) * LANES,
                                  cb * KV_BLOCK:(cb + 1) * KV_BLOCK].astype(BF16)


def _inproj_even_kernel(x_ref, mod_ref, ng_ref, w_ref, wvt_ref, pos_ref, inv_ref, gain_ref, p_ref,
                        o_ref, vt_ref, h_scr, *, tm):
    h_scr[...] = _modulated_rows(x_ref[0], mod_ref, ng_ref).astype(BF16)
    _store_transposed_values(wvt_ref, h_scr, vt_ref, tm)
    ang_t = pos_ref[0].astype(F32) * inv_ref[...]
    reps = LANES // ang_t.shape[0]
    cos = jnp.concatenate([jnp.cos(ang_t)] * reps, axis=0).T
    sin = jnp.concatenate([jnp.sin(ang_t)] * reps, axis=0).T
    lane = lax.broadcasted_iota(jnp.int32, (tm, LANES), 1)
    first = (lane & (HEAD_DIM // 2)) == 0
    sin_s = jnp.where(first, -sin, sin)
    for wcol, start, width, kind in EVEN_CHUNKS:
        u = jnp.dot(h_scr[...], w_ref[:, wcol:wcol + width], preferred_element_type=F32)
        if kind == "qk":
            ms = jnp.dot((u * u).astype(BF16), p_ref[:width, :width], preferred_element_type=F32)
            un = u * lax.rsqrt(ms + EPS) * gain_ref[:, start:start + width]
            for s in range(width // LANES):
                xs = un[:, s * LANES:(s + 1) * LANES]
                rot = jnp.where(first, pltpu.roll(xs, LANES - HEAD_DIM // 2, 1),
                                pltpu.roll(xs, HEAD_DIM // 2, 1))
                c0 = start + s * LANES
                o_ref[0, :, c0:c0 + LANES] = (xs * cos + rot * sin_s).astype(BF16)
        elif kind == "silu":
            o_ref[0, :, start:start + width] = (u * jax.nn.sigmoid(u)).astype(BF16)
        else:
            o_ref[0, :, start:start + width] = u.astype(BF16)


def _inproj_odd_kernel(*refs, tm, n_y):
    if n_y:
        y_refs, (wout_ref, modp_ref) = refs[:n_y], refs[n_y:n_y + 2]
        x_ref, mod_ref, ng_ref, w_ref, wvt_ref, x1_ref, o_ref, vt_ref, h_scr = refs[n_y + 2:]
        x = x_ref[0] + modp_ref[2, 0] * _project_out(y_refs, wout_ref)
        x1_ref[0] = x
    else:
        x_ref, mod_ref, ng_ref, w_ref, wvt_ref, o_ref, vt_ref, h_scr = refs
        x = x_ref[0]
    h_scr[...] = _modulated_rows(x, mod_ref, ng_ref).astype(BF16)
    _store_transposed_values(wvt_ref, h_scr, vt_ref, tm)
    for wcol, start, width, kind in ODD_CHUNKS:
        u = jnp.dot(h_scr[...], w_ref[:, wcol:wcol + width], preferred_element_type=F32)
        if kind == "scale":
            u = u * (QK_SCALE * LOG2E)
        elif kind == "silu":
            u = u * jax.nn.sigmoid(u)
        o_ref[0, :, start:start + width] = u.astype(BF16)


def _inproj_even(x, mod, norm_g, w_in, positions, gain_cols, tm):
    B, S, D = x.shape
    w_rows = _to_bf16(w_in)
    w_vat = _to_bf16(jnp.concatenate([w_in[:, W_VA:W_GA], w_in[:, W_VB:W_GB]], axis=1).T)
    n_groups = (A_WIDTH + B_KV) // LANES
    half = HEAD_DIM // 2
    inv = (ROPE_THETA ** (-jnp.arange(half, dtype=F32) / half)).reshape(half, 1)
    blk = jnp.arange(256) // HEAD_DIM
    avg = jnp.where(blk[:, None] == blk[None, :], 1.0 / HEAD_DIM, 0.0).astype(BF16)
    return pl.pallas_call(
        functools.partial(_inproj_even_kernel, tm=tm),
        grid=(B, S // tm),
        in_specs=[
            pl.BlockSpec((1, tm, D), lambda b, i: (b, i, 0)),
            pl.BlockSpec((3, 1, 1, D), lambda b, i: (0, b, 0, 0)),
            pl.BlockSpec((1, D), lambda b, i: (0, 0)),
            pl.BlockSpec((D, W_END), lambda b, i: (0, 0)),
            pl.BlockSpec((A_WIDTH + B_KV, D), lambda b, i: (0, 0)),
            pl.BlockSpec((1, 1, tm), lambda b, i: (b, 0, i)),
            pl.BlockSpec((half, 1), lambda b, i: (0, 0)),
            pl.BlockSpec((1, EVEN_IN), lambda b, i: (0, 0)),
            pl.BlockSpec((256, 256), lambda b, i: (0, 0)),
        ],
        out_specs=[
            pl.BlockSpec((1, tm, EVEN_IN), lambda b, i: (b, i, 0)),
            pl.BlockSpec((1, n_groups, tm // KV_BLOCK, LANES, KV_BLOCK), lambda b, i: (b, 0, i, 0, 0)),
        ],
        out_shape=[
            jax.ShapeDtypeStruct((B, S, EVEN_IN), BF16),
            jax.ShapeDtypeStruct((B, n_groups, S // KV_BLOCK, LANES, KV_BLOCK), BF16),
        ],
        scratch_shapes=[pltpu.VMEM((tm, D), BF16)],
        compiler_params=_cparams(("parallel", "parallel")),
        name="inproj_even",
    )(x, mod, norm_g.reshape(1, D), w_rows, w_vat, positions.reshape(B, 1, S),
      inv, gain_cols, avg)


def _inproj_odd(x, mod, norm_g, w_in, tm, pending=None):
    B, S, D = x.shape
    w_rows = _to_bf16(w_in)
    w_vt = _to_bf16(w_in[:, 2 * C_WIDTH:3 * C_WIDTH].T)
    n_groups = C_WIDTH // LANES
    row_tile = lambda width: pl.BlockSpec((1, tm, width), lambda b, i: (b, i, 0))
    mod_spec = pl.BlockSpec((3, 1, 1, D), lambda b, i: (0, b, 0, 0))
    const = lambda shape: pl.BlockSpec(shape, lambda b, i: (0,) * len(shape),
                                       pipeline_mode=pl.Buffered(1))
    in_specs = [row_tile(D), mod_spec, const((1, D)), const((D, 4 * C_WIDTH)), const((C_WIDTH, D))]
    args = [x, mod, norm_g.reshape(1, D), w_rows, w_vt]
    out_specs = [
        row_tile(ODD_IN),
        pl.BlockSpec((1, n_groups, tm // KV_BLOCK, LANES, KV_BLOCK), lambda b, i: (b, 0, i, 0, 0)),
    ]
    out_shape = [
        jax.ShapeDtypeStruct((B, S, ODD_IN), BF16),
        jax.ShapeDtypeStruct((B, n_groups, S // KV_BLOCK, LANES, KV_BLOCK), BF16),
    ]
    n_y = 0
    if pending is not None:
        ys, w_out, mod_prev = pending
        n_y = len(ys)
        in_specs = [row_tile(y.shape[-1]) for y in ys] + [const(w_out.shape), mod_spec] + in_specs
        args = list(ys) + [w_out.astype(BF16), mod_prev] + args
        out_specs = [row_tile(D)] + out_specs
        out_shape = [jax.ShapeDtypeStruct((B, S, D), F32)] + out_shape
    outs = pl.pallas_call(
        functools.partial(_inproj_odd_kernel, tm=tm, n_y=n_y),
        grid=(B, S // tm),
        in_specs=in_specs,
        out_specs=out_specs,
        out_shape=out_shape,
        scratch_shapes=[pltpu.VMEM((tm, D), BF16)],
        compiler_params=_cparams(("parallel", "parallel")),
        name="inproj_odd",
    )(*args)
    return (outs[0], outs[1], outs[2]) if n_y else (x, outs[0], outs[1])


def _outproj_kernel(*refs, n_in):
    y_refs = refs[:n_in]
    w_ref, x_ref, mod_ref, o_ref = refs[n_in:]
    o_ref[0] = x_ref[0] + mod_ref[2, 0] * _project_out(y_refs, w_ref)


def _outproj(ys, w_out, x, mod, tm):
    B, S, D = x.shape
    K = w_out.shape[0]
    n_in = len(ys)
    in_specs = [pl.BlockSpec((1, tm, y.shape[-1]), lambda b, i: (b, i, 0)) for y in ys]
    in_specs += [
        pl.BlockSpec((K, D), lambda b, i: (0, 0)),
        pl.BlockSpec((1, tm, D), lambda b, i: (b, i, 0)),
        pl.BlockSpec((3, 1, 1, D), lambda b, i: (0, b, 0, 0)),
    ]
    return pl.pallas_call(
        functools.partial(_outproj_kernel, n_in=n_in),
        grid=(B, S // tm),
        in_specs=in_specs,
        out_specs=pl.BlockSpec((1, tm, D), lambda b, i: (b, i, 0)),
        out_shape=jax.ShapeDtypeStruct((B, S, D), F32),
        compiler_params=_cparams(("parallel", "parallel")),
        name="outproj",
    )(*ys, w_out.astype(BF16), x, mod)


def _diff_attn_kernel(bound_ref, lam_ref, sub_ref, q_ref, k_ref, vt_ref, g_ref, o_ref,
                      qs_scr, s0_scr, s1_scr, smax0_scr, smax1_scr, m_scr, l_scr, acc_scr, *,
                      tq, tk, lambda_init):
    i = pl.program_id(2)
    q = q_ref[0]
    lane = lax.broadcasted_iota(jnp.int32, (tq, LANES), 1)
    zero = jnp.zeros_like(q)
    qs_scr[:tq] = jnp.where(lane < HEAD_DIM, q, zero)
    qs_scr[tq:] = jnp.where(lane >= HEAD_DIM, q, zero)

    slots = ((s0_scr, smax0_scr), (s1_scr, smax1_scr))

    def spans(diag):
        if diag is None:
            return ((0, 2 * tq),)
        return tuple((c * tq + diag * tk, (c + 1) * tq) for c in range(2))

    def scores(j, slot, bounded, diag=None):
        s_ref, smax_ref = slots[slot]
        k = k_ref[0, pl.ds(pl.multiple_of(j * tk, tk), tk), :]
        for a, b in spans(diag):
            s = lax.dot_general(k, qs_scr[a:b], (((1,), (1,)), ((), ())),
                                preferred_element_type=F32)
            s_ref[:, a:b] = s
            if not bounded and diag is None:
                smax_ref[:, a:b] = jnp.max(s, axis=0, keepdims=True)

    def softmax_pv(j, slot, bounded, diag=None):
        s_ref, smax_ref = slots[slot]
        for a, b in spans(diag):
            s = s_ref[:, a:b]
            if diag is not None:
                key = lax.broadcasted_iota(jnp.int32, (tk, b - a), 0) + diag * tk
                qry = (lax.broadcasted_iota(jnp.int32, (tk, b - a), 1) + a) & (tq - 1)
                s = jnp.where(key <= qry, s, NEG_BIG)
            if bounded:
                p = jnp.exp2(s)
                l_scr[:, a:b] += jnp.sum(p, axis=0, keepdims=True)
                acc_scr[:, a:b] += jnp.dot(vt_ref[0, 0, j], p.astype(BF16),
                                           preferred_element_type=F32)
                continue
            smax = smax_ref[:, a:b] if diag is None else jnp.max(s, axis=0, keepdims=True)
            m = m_scr[:, a:b]
            m_new = jnp.maximum(m, smax)
            alpha = jnp.exp2(m - m_new)
            p = jnp.exp2(s - m_new)
            l_scr[:, a:b] = alpha * l_scr[:, a:b] + jnp.sum(p, axis=0, keepdims=True)
            m_scr[:, a:b] = m_new
            pv = jnp.dot(vt_ref[0, 0, j], p.astype(BF16), preferred_element_type=F32)
            acc_scr[:, a:b] = alpha * acc_scr[:, a:b] + pv

    def sweep(bounded):
        m_scr[...] = jnp.full((1, 2 * tq), NEG_BIG, F32)
        l_scr[...] = jnp.zeros((1, 2 * tq), F32)
        acc_scr[...] = jnp.zeros((LANES, 2 * tq), F32)
        n_diag = tq // tk
        n_full = n_diag * i
        scores(0, 0, bounded)

        def body(jj, carry):
            j0 = n_diag * jj
            for d in range(n_diag):
                scores(j0 + d + 1, (d + 1) % 2, bounded)
                softmax_pv(j0 + d, d % 2, bounded)
            return carry

        lax.fori_loop(0, i, body, 0)
        for d in range(n_diag):
            if d + 1 < n_diag:
                scores(n_full + d + 1, (d + 1) % 2, bounded, diag=d + 1)
            softmax_pv(n_full + d, d % 2, bounded, diag=d)

    is_bounded = bound_ref[0] <= LOGIT_BOUND
    pl.when(is_bounded)(lambda: sweep(True))
    pl.when(jnp.logical_not(is_bounded))(lambda: sweep(False))
    l = l_scr[...]
    acc = acc_scr[...]

    lam_rows = lam_ref[...]
    lam = (jnp.exp(jnp.sum(lam_rows[0:1] * lam_rows[1:2], axis=1, keepdims=True))
           - jnp.exp(jnp.sum(lam_rows[2:3] * lam_rows[3:4], axis=1, keepdims=True))
           + lambda_init)
    ot = acc[:, :tq] / l[:, :tq] - lam * (acc[:, tq:] / l[:, tq:])
    ms = jnp.mean(ot * ot, axis=0, keepdims=True)
    yt = ot * lax.rsqrt(ms + EPS) * (sub_ref[...] * (1.0 - lambda_init))
    o_ref[0] = (yt.T * g_ref[0].astype(F32)).astype(BF16)


def _diff_attention(u, vat, logit_bound, lam_rows, subln_g, lambda_init, tq):
    B, S, _ = u.shape
    tk = KV_BLOCK
    assert tq % (2 * tk) == 0 and S % tq == 0
    qb, kb, gb = OFF_QA // LANES, OFF_KA // LANES, OFF_GA // LANES
    stat = pltpu.VMEM((1, 2 * tq), F32)
    score = pltpu.VMEM((tk, 2 * tq), F32)
    return pl.pallas_call(
        functools.partial(_diff_attn_kernel, tq=tq, tk=tk, lambda_init=lambda_init),
        grid=(B, A_HEADS, S // tq),
        in_specs=[
            pl.BlockSpec(memory_space=pltpu.SMEM),
            pl.BlockSpec((4, HEAD_DIM), lambda b, h, i: (0, 0)),
            pl.BlockSpec((LANES, 1), lambda b, h, i: (0, 0)),
            pl.BlockSpec((1, tq, LANES), lambda b, h, i: (b, i, qb + h)),
            pl.BlockSpec((1, S, LANES), lambda b, h, i: (b, 0, kb + h)),
            pl.BlockSpec((1, 1, S // tk, LANES, tk), lambda b, h, i: (b, h, 0, 0, 0)),
            pl.BlockSpec((1, tq, LANES), lambda b, h, i: (b, i, gb + h)),
        ],
        out_specs=pl.BlockSpec((1, tq, LANES), lambda b, h, i: (b, i, h)),
        out_shape=jax.ShapeDtypeStruct((B, S, A_WIDTH), BF16),
        scratch_shapes=[pltpu.VMEM((2 * tq, LANES), BF16), score, score, stat, stat, stat, stat,
                        pltpu.VMEM((LANES, 2 * tq), F32)],
        compiler_params=_cparams(("parallel", "parallel", "arbitrary")),
        name="diff_attention",
    )(logit_bound.reshape(1).astype(F32), lam_rows, subln_g.reshape(LANES, 1), u, u, vat, u)


def _swa_kernel(sink_ref, q_ref, kc_ref, kp_ref, vtc_ref, vtp_ref, g_ref, o_ref, qs_scr, *, t):
    i = pl.program_id(1)
    group = B_HEADS // B_KV_HEADS
    width = B_HEADS * t
    lane = lax.broadcasted_iota(jnp.int32, (t, LANES), 1)
    for h in range(B_HEADS):
        kv = h // group
        qc = q_ref[0, :, (h // 2) * LANES:(h // 2 + 1) * LANES].astype(F32)
        if h % 2 != kv:
            qc = pltpu.roll(qc, HEAD_DIM, 1)
        keep = (lane >= kv * HEAD_DIM) & (lane < (kv + 1) * HEAD_DIM)
        qs_scr[h * t:(h + 1) * t] = jnp.where(keep, qc, 0.0).astype(BF16)
    kk = jnp.concatenate([kp_ref[0], kc_ref[0]], axis=0)
    vvt = jnp.concatenate([vtp_ref[0, 0, 0][:, KV_BLOCK - WINDOW:], vtc_ref[0, 0, 0]], axis=1)
    s = lax.dot_general(kk, qs_scr[...], (((1,), (1,)), ((), ())),
                        preferred_element_type=F32)
    key = lax.broadcasted_iota(jnp.int32, (WINDOW + t, width), 0)
    seg = lax.broadcasted_iota(jnp.int32, (WINDOW + t, width), 1)
    qry = seg & (t - 1)
    mask = (key > qry) & (key <= qry + WINDOW) & jnp.logical_or(i > 0, key >= WINDOW)
    s = jnp.where(mask, s, NEG_BIG)
    head = lax.broadcasted_iota(jnp.int32, (1, width), 1) // t
    sink = jnp.zeros((1, width), F32)
    for h in range(B_HEADS):
        sink = jnp.where(head == h, sink_ref[h] * LOG2E, sink)
    m = jnp.maximum(jnp.max(s, axis=0, keepdims=True), sink)
    e = jnp.exp2(s - m)
    den = jnp.sum(e, axis=0, keepdims=True) + jnp.exp2(sink - m)
    pv = jnp.dot(vvt, e.astype(BF16), preferred_element_type=F32) / den
    for c in range(B_HEADS // 2):
        rows = []
        for h in (2 * c, 2 * c + 1):
            kv = h // group
            rows.append(pv[kv * HEAD_DIM:(kv + 1) * HEAD_DIM, h * t:(h + 1) * t])
        ot = jnp.concatenate(rows, axis=0)
        gate = g_ref[0, :, c * LANES:(c + 1) * LANES].astype(F32)
        o_ref[0, :, c * LANES:(c + 1) * LANES] = (ot.T * gate).astype(BF16)


def _swa_attention(u, vt, sinks):
    B, S, _ = u.shape
    t = KV_BLOCK
    per = t // WINDOW
    qb, gb, kb = OFF_QB // B_WIDTH, OFF_GB // B_WIDTH, OFF_KB // LANES
    assert OFF_QB % B_WIDTH == 0 and OFF_GB % B_WIDTH == 0
    vgroup = A_WIDTH // LANES
    return pl.pallas_call(
        functools.partial(_swa_kernel, t=t),
        grid=(B, S // t),
        in_specs=[
            pl.BlockSpec(memory_space=pltpu.SMEM),
            pl.BlockSpec((1, t, B_WIDTH), lambda b, i: (b, i, qb)),
            pl.BlockSpec((1, t, LANES), lambda b, i: (b, i, kb)),
            pl.BlockSpec((1, WINDOW, LANES), lambda b, i: (b, jnp.maximum(i * per - 1, 0), kb)),
            pl.BlockSpec((1, 1, 1, LANES, t), lambda b, i: (b, vgroup, i, 0, 0)),
            pl.BlockSpec((1, 1, 1, LANES, t), lambda b, i: (b, vgroup, jnp.maximum(i - 1, 0), 0, 0)),
            pl.BlockSpec((1, t, B_WIDTH), lambda b, i: (b, i, gb)),
        ],
        out_specs=pl.BlockSpec((1, t, B_WIDTH), lambda b, i: (b, i, 0)),
        out_shape=jax.ShapeDtypeStruct((B, S, B_WIDTH), BF16),
        scratch_shapes=[pltpu.VMEM((B_HEADS * t, LANES), BF16)],
        compiler_params=_cparams(("parallel", "parallel")),
        name="swa_attention",
    )(sinks.astype(F32), u, u, u, vt, vt, u)


def _stick_kernel(tri_ref, q_ref, k_ref, vt_ref, g_ref, o_ref, qs_scr, run_scr, acc_scr, *, t, pairs):
    i = pl.program_id(2)
    heads = 2 * pairs
    width = heads * t
    lane = lax.broadcasted_iota(jnp.int32, (t, LANES), 1)
    for p in range(pairs):
        q = q_ref[0, :, p * LANES:(p + 1) * LANES]
        zero = jnp.zeros_like(q)
        qs_scr[(2 * p) * t:(2 * p + 1) * t] = jnp.where(lane < HEAD_DIM, q, zero)
        qs_scr[(2 * p + 1) * t:(2 * p + 2) * t] = jnp.where(lane >= HEAD_DIM, q, zero)
    tri = tri_ref[...]
    acc_scr[...] = jnp.zeros((LANES, width), F32)

    def prepare(j, diagonal):
        off = pl.multiple_of(j * t, t)
        z = jnp.concatenate([
            lax.dot_general(k_ref[0, pl.ds(off, t), p * LANES:(p + 1) * LANES],
                            qs_scr[2 * p * t:(2 * p + 2) * t],
                            (((1,), (1,)), ((), ())), preferred_element_type=F32)
            for p in range(pairs)], axis=1)
        if diagonal:
            key = lax.broadcasted_iota(jnp.int32, (t, width), 0)
            qry = lax.broadcasted_iota(jnp.int32, (t, width), 1) & (t - 1)
            z = jnp.where(key < qry, z, NEG_BIG)
        sp = jnp.maximum(z, 0.0) + jnp.log(1.0 + jnp.exp2(-jnp.abs(z))) * LOG2E
        csum = jnp.dot(tri, sp.astype(BF16), preferred_element_type=F32)
        return z, csum

    def apply(j, z, csum, run):
        w = jnp.exp2(z - csum - run).astype(BF16)
        for p in range(pairs):
            cols = slice(2 * p * t, (2 * p + 2) * t)
            acc_scr[:, cols] += jnp.dot(vt_ref[0, p, j], w[:, cols], preferred_element_type=F32)
        return run + csum[0:1, :]

    prev = jnp.maximum(i - 1, 0)
    zd, cd = prepare(i, True)
    zp, cp = prepare(prev, False)
    run = apply(i, zd, cd, jnp.zeros((1, width), F32))
    run = apply(prev, zp, cp, run + jnp.where(i > 0, 0.0, -NEG_BIG))
    run_scr[...] = run

    def cond(c):
        jj, go = c
        return jnp.logical_and(jj < i - 1, go)

    def body(c):
        jj, _ = c
        j = i - 2 - jj
        z, csum = prepare(j, False)
        run = apply(j, z, csum, run_scr[...])
        run_scr[...] = run
        return jj + 1, jnp.min(run) < EXP2_UNDERFLOW

    lax.while_loop(cond, body, (jnp.int32(0), jnp.min(run) < EXP2_UNDERFLOW))
    first = lax.broadcasted_iota(jnp.int32, (LANES, t), 0) < HEAD_DIM
    for p in range(pairs):
        a = 2 * p * t
        ot = jnp.where(first, acc_scr[:, a:a + t], acc_scr[:, a + t:a + 2 * t])
        gate = g_ref[0, :, p * LANES:(p + 1) * LANES].astype(F32)
        o_ref[0, :, p * LANES:(p + 1) * LANES] = (ot.T * gate).astype(BF16)


def _stick_attention(u, vt, pairs):
    B, S, _ = u.shape
    t = KV_BLOCK
    width = pairs * LANES
    groups = C_WIDTH // width
    row = jnp.arange(t)
    tri = (row[None, :] >= row[:, None]).astype(BF16)
    return pl.pallas_call(
        functools.partial(_stick_kernel, t=t, pairs=pairs),
        grid=(B, groups, S // t),
        in_specs=[
            pl.BlockSpec((t, t), lambda b, h, i: (0, 0)),
            pl.BlockSpec((1, t, width), lambda b, h, i: (b, i, h)),
            pl.BlockSpec((1, S, width), lambda b, h, i: (b, 0, groups + h),
                         pipeline_mode=pl.Buffered(1)),
            pl.BlockSpec((1, pairs, S // t, LANES, t), lambda b, h, i: (b, h, 0, 0, 0),
                         pipeline_mode=pl.Buffered(1)),
            pl.BlockSpec((1, t, width), lambda b, h, i: (b, i, 2 * groups + h)),
        ],
        out_specs=pl.BlockSpec((1, t, width), lambda b, h, i: (b, i, h)),
        out_shape=jax.ShapeDtypeStruct((B, S, C_WIDTH), BF16),
        scratch_shapes=[pltpu.VMEM((2 * pairs * t, LANES), BF16),
                        pltpu.VMEM((1, 2 * pairs * t), F32),
                        pltpu.VMEM((LANES, 2 * pairs * t), F32)],
        compiler_params=_cparams(("parallel", "parallel", "arbitrary")),
        name="stick_attention",
    )(tri, u, u, vt, u)


def _tile(n, pref):
    t = min(pref, n)
    assert n % t == 0
    return t


def _even_layer(x, c, positions, layer, norm_g, w_mod, b_mod, w_in, a_q_gain, a_k_gain,
                lq1, lk1, lq2, lk2, a_subln_g, b_q_gain, b_k_gain, b_sinks, w_out):
    B, S, D = x.shape
    mod = _modulation(c, w_mod, b_mod)
    ones = lambda n: jnp.ones((n,), F32)
    gain_cols = jnp.concatenate([
        jnp.tile(a_q_gain * (QK_SCALE * LOG2E), A_QK // HEAD_DIM),
        jnp.tile(a_k_gain, A_QK // HEAD_DIM),
        ones(A_WIDTH),
        jnp.tile(b_q_gain * (QK_SCALE * LOG2E), B_HEADS),
        ones(B_WIDTH),
        jnp.tile(b_k_gain, B_KV_HEADS),
    ]).astype(F32).reshape(1, EVEN_IN)
    u, vat = _inproj_even(x, mod, norm_g, w_in, positions, gain_cols, _tile(S, 512))
    lambda_init = 0.8 - 0.6 * math.exp(-0.3 * layer)
    lam_rows = jnp.stack([lq1, lk1, lq2, lk2]).astype(F32)
    logit_bound = (HEAD_DIM * QK_SCALE * LOG2E * LOGIT_BOUND_MARGIN
                   * jnp.max(jnp.abs(a_q_gain)) * jnp.max(jnp.abs(a_k_gain)))
    ya = _diff_attention(u, vat, logit_bound, lam_rows, a_subln_g.astype(F32), lambda_init,
                         _tile(S, 2048))
    yb = _swa_attention(u, vat, b_sinks)
    return x, ([ya, yb], w_out, mod)


def _odd_layer(x, pending, c, norm_g, w_mod, b_mod, w_in, w_out):
    B, S, D = x.shape
    mod = _modulation(c, w_mod, b_mod)
    x, u, vt = _inproj_odd(x, mod, norm_g, w_in, _tile(S, 512), pending)
    y = _stick_attention(u, vt, pairs=4)
    return x, ([y], w_out, mod)


def kernel(x, c, positions, even_norm_g, even_w_mod, even_b_mod, even_w_in, a_q_gain, a_k_gain,
           a_lambda_q1, a_lambda_k1, a_lambda_q2, a_lambda_k2, a_subln_g, b_q_gain, b_k_gain,
           b_sinks, even_w_out, odd_norm_g, odd_w_mod, odd_b_mod, odd_w_in, odd_w_out):
    depth = even_norm_g.shape[0] + odd_norm_g.shape[0]
    tm = _tile(x.shape[1], 512)
    pending = None
    for layer in range(depth):
        j = layer // 2
        if layer % 2 == 0:
            if pending is not None:
                x = _outproj(pending[0], pending[1], x, pending[2], tm)
            x, pending = _even_layer(
                x, c, positions, layer, even_norm_g[j], even_w_mod[j], even_b_mod[j],
                even_w_in[j], a_q_gain[j], a_k_gain[j], a_lambda_q1[j], a_lambda_k1[j],
                a_lambda_q2[j], a_lambda_k2[j], a_subln_g[j], b_q_gain[j], b_k_gain[j],
                b_sinks[j], even_w_out[j])
        else:
            x, pending = _odd_layer(x, pending, c, odd_norm_g[j], odd_w_mod[j], odd_b_mod[j],
                                    odd_w_in[j], odd_w_out[j])
    return _outproj(pending[0], pending[1], x, pending[2], tm)
```

```python
import functools
import math

import jax
import jax.numpy as jnp
from jax import lax
from jax.experimental import pallas as pl
from jax.experimental.pallas import tpu as pltpu

F32 = jnp.float32
BF16 = jnp.bfloat16

HEAD_DIM = 64
ROPE_THETA = 10000.0
EPS = 1e-6
WINDOW = 128
LANES = 128
QK_SCALE = HEAD_DIM ** -0.5

A_HEADS = 4
A_QK = A_HEADS * 2 * HEAD_DIM
A_WIDTH = A_HEADS * 2 * HEAD_DIM
B_HEADS = 8
B_KV_HEADS = 2
B_WIDTH = B_HEADS * HEAD_DIM
B_KV = B_KV_HEADS * HEAD_DIM
C_HEADS = 16
C_WIDTH = C_HEADS * HEAD_DIM

W_QA = 0
W_KA = W_QA + A_QK
W_VA = W_KA + A_QK
W_GA = W_VA + A_WIDTH
W_QB = W_GA + A_WIDTH
W_KB = W_QB + B_WIDTH
W_VB = W_KB + B_KV
W_GB = W_VB + B_KV
W_END = W_GB + B_WIDTH

OFF_QA = 0
OFF_KA = OFF_QA + A_QK
OFF_GA = OFF_KA + A_QK
OFF_QB = OFF_GA + A_WIDTH
OFF_GB = OFF_QB + B_WIDTH
OFF_KB = OFF_GB + B_WIDTH
EVEN_IN = OFF_KB + B_KV

EVEN_CHUNKS = (
    (W_GA, OFF_GA, 512, "silu"), (W_GB, OFF_GB, 512, "silu"),
    (W_QA, OFF_QA, 256, "qk"), (W_QA + 256, OFF_QA + 256, 256, "qk"),
    (W_KA, OFF_KA, 256, "qk"), (W_KA + 256, OFF_KA + 256, 256, "qk"),
    (W_QB, OFF_QB, 256, "qk"), (W_QB + 256, OFF_QB + 256, 256, "qk"),
    (W_KB, OFF_KB, 128, "qk"),
)
LOG2E = 1.4426950408889634
KV_BLOCK = 256
ODD_IN = 3 * C_WIDTH
ODD_CHUNKS = tuple(
    (w_i * C_WIDTH + half * 512, o_i * C_WIDTH + half * 512, 512, kind)
    for o_i, (w_i, kind) in enumerate(((0, "scale"), (1, "plain"), (3, "silu")))
    for half in range(2)
)

NEG_BIG = -1e30
EXP2_UNDERFLOW = 151.0
LOGIT_BOUND = 64.0
LOGIT_BOUND_MARGIN = 1.01
VMEM_LIMIT = 48 * 1024 * 1024


def _cparams(sem):
    return pltpu.CompilerParams(dimension_semantics=sem, vmem_limit_bytes=VMEM_LIMIT)


def _cast_kernel(w_ref, o_ref):
    o_ref[...] = w_ref[...].astype(BF16)


def _to_bf16(w, layer=None, rows=128):
    K, N = w.shape[-2:]
    if layer is None:
        in_spec = pl.BlockSpec((rows, N), lambda r: (r, 0))
    else:
        in_spec = pl.BlockSpec((None, rows, N), lambda r: (layer, r, 0))
    return pl.pallas_call(
        _cast_kernel,
        grid=(K // rows,),
        in_specs=[in_spec],
        out_specs=pl.BlockSpec((rows, N), lambda r: (r, 0)),
        out_shape=jax.ShapeDtypeStruct((K, N), BF16),
        compiler_params=_cparams(("parallel",)),
        name="weight_cast",
    )(w)


def _mod_kernel(c_ref, w_ref, b_ref, o_ref):
    c = c_ref[...]
    sc = c * jax.nn.sigmoid(c)
    o_ref[0] = jnp.dot(sc, w_ref[...], preferred_element_type=F32,
                       precision=lax.Precision.HIGHEST) + b_ref[0]


def _modulation(c, w_mod, b_mod):
    B, D = c.shape
    rows = 8
    cp = jnp.pad(c, ((0, rows - B), (0, 0)))
    w_mod, layer = w_mod
    out = pl.pallas_call(
        _mod_kernel,
        grid=(3,),
        in_specs=[
            pl.BlockSpec((rows, D), lambda j: (0, 0)),
            pl.BlockSpec((None, D, D), lambda j: (layer, 0, j)),
            pl.BlockSpec((1, 1, D), lambda j: (j, 0, 0)),
        ],
        out_specs=pl.BlockSpec((1, rows, D), lambda j: (j, 0, 0)),
        out_shape=jax.ShapeDtypeStruct((3, rows, D), F32),
        compiler_params=_cparams(("arbitrary",)),
        name="modulation",
    )(cp, w_mod, b_mod.reshape(3, 1, D))
    return out[:, :B].reshape(3, B, 1, D)


def _project_out(y_refs, w_ref):
    acc = None
    off = 0
    for y_ref in y_refs:
        wd = y_ref.shape[-1]
        part = jnp.dot(y_ref[0], w_ref[off:off + wd, :], preferred_element_type=F32)
        acc = part if acc is None else acc + part
        off += wd
    return acc


def _modulated_rows(x, mod_ref, ng_ref):
    ms = jnp.mean(x * x, axis=-1, keepdims=True)
    y = x * lax.rsqrt(ms + EPS) * ng_ref[...]
    return y * (1.0 + mod_ref[1, 0]) + mod_ref[0, 0]


def _store_transposed_values(wvt_ref, h_scr, vt_ref, tm):
    vt = lax.dot_general(wvt_ref[...], h_scr[...], (((1,), (1,)), ((), ())),
                         preferred_element_type=F32)
    for g in range(vt.shape[0] // LANES):
        for cb in range(tm // KV_BLOCK):
            vt_ref[0, g, cb] = vt[g * LANES:(g + 1) * LANES,
                                  cb * KV_BLOCK:(cb + 1) * KV_BLOCK].astype(BF16)


def _inproj_even_kernel(x_ref, mod_ref, ng_ref, w_ref, wvt_ref, pos_ref, inv_ref, gain_ref, p_ref,
                        o_ref, vt_ref, h_scr, *, tm):
    h_scr[...] = _modulated_rows(x_ref[0], mod_ref, ng_ref).astype(BF16)
    _store_transposed_values(wvt_ref, h_scr, vt_ref, tm)
    ang_t = pos_ref[0].astype(F32) * inv_ref[...]
    reps = LANES // ang_t.shape[0]
    cos = jnp.concatenate([jnp.cos(ang_t)] * reps, axis=0).T
    sin = jnp.concatenate([jnp.sin(ang_t)] * reps, axis=0).T
    lane = lax.broadcasted_iota(jnp.int32, (tm, LANES), 1)
    first = (lane & (HEAD_DIM // 2)) == 0
    sin_s = jnp.where(first, -sin, sin)
    for wcol, start, width, kind in EVEN_CHUNKS:
        u = jnp.dot(h_scr[...], w_ref[:, wcol:wcol + width], preferred_element_type=F32)
        if kind == "qk":
            ms = jnp.dot((u * u).astype(BF16), p_ref[:width, :width], preferred_element_type=F32)
            un = u * lax.rsqrt(ms + EPS) * gain_ref[:, start:start + width]
            for s in range(width // LANES):
                xs = un[:, s * LANES:(s + 1) * LANES]
                rot = jnp.where(first, pltpu.roll(xs, LANES - HEAD_DIM // 2, 1),
                                pltpu.roll(xs, HEAD_DIM // 2, 1))
                c0 = start + s * LANES
                o_ref[0, :, c0:c0 + LANES] = (xs * cos + rot * sin_s).astype(BF16)
        elif kind == "silu":
            o_ref[0, :, start:start + width] = (u * jax.nn.sigmoid(u)).astype(BF16)
        else:
            o_ref[0, :, start:start + width] = u.astype(BF16)


def _inproj_odd_kernel(*refs, tm, n_y):
    if n_y:
        y_refs, (wout_ref, modp_ref) = refs[:n_y], refs[n_y:n_y + 2]
        x_ref, mod_ref, ng_ref, w_ref, wvt_ref, x1_ref, o_ref, vt_ref, h_scr = refs[n_y + 2:]
        x = x_ref[0] + modp_ref[2, 0] * _project_out(y_refs, wout_ref)
        x1_ref[0] = x
    else:
        x_ref, mod_ref, ng_ref, w_ref, wvt_ref, o_ref, vt_ref, h_scr = refs
        x = x_ref[0]
    h_scr[...] = _modulated_rows(x, mod_ref, ng_ref).astype(BF16)
    _store_transposed_values(wvt_ref, h_scr, vt_ref, tm)
    for wcol, start, width, kind in ODD_CHUNKS:
        u = jnp.dot(h_scr[...], w_ref[:, wcol:wcol + width], preferred_element_type=F32)
        if kind == "scale":
            u = u * (QK_SCALE * LOG2E)
        elif kind == "silu":
            u = u * jax.nn.sigmoid(u)
        o_ref[0, :, start:start + width] = u.astype(BF16)


def _inproj_even(x, mod, norm_g, w_in, positions, gain_cols, tm):
    B, S, D = x.shape
    w_stack, layer = w_in
    w_rows = _to_bf16(w_stack, layer)
    w_vat = _to_bf16(jnp.concatenate([w_stack[layer, :, W_VA:W_GA],
                                      w_stack[layer, :, W_VB:W_GB]], axis=1).T)
    n_groups = (A_WIDTH + B_KV) // LANES
    half = HEAD_DIM // 2
    inv = (ROPE_THETA ** (-jnp.arange(half, dtype=F32) / half)).reshape(half, 1)
    blk = jnp.arange(256) // HEAD_DIM
    avg = jnp.where(blk[:, None] == blk[None, :], 1.0 / HEAD_DIM, 0.0).astype(BF16)
    return pl.pallas_call(
        functools.partial(_inproj_even_kernel, tm=tm),
        grid=(B, S // tm),
        in_specs=[
            pl.BlockSpec((1, tm, D), lambda b, i: (b, i, 0)),
            pl.BlockSpec((3, 1, 1, D), lambda b, i: (0, b, 0, 0)),
            pl.BlockSpec((1, D), lambda b, i: (0, 0)),
            pl.BlockSpec((D, W_END), lambda b, i: (0, 0)),
            pl.BlockSpec((A_WIDTH + B_KV, D), lambda b, i: (0, 0)),
            pl.BlockSpec((1, 1, tm), lambda b, i: (b, 0, i)),
            pl.BlockSpec((half, 1), lambda b, i: (0, 0)),
            pl.BlockSpec((1, EVEN_IN), lambda b, i: (0, 0)),
            pl.BlockSpec((256, 256), lambda b, i: (0, 0)),
        ],
        out_specs=[
            pl.BlockSpec((1, tm, EVEN_IN), lambda b, i: (b, i, 0)),
            pl.BlockSpec((1, n_groups, tm // KV_BLOCK, LANES, KV_BLOCK), lambda b, i: (b, 0, i, 0, 0)),
        ],
        out_shape=[
            jax.ShapeDtypeStruct((B, S, EVEN_IN), BF16),
            jax.ShapeDtypeStruct((B, n_groups, S // KV_BLOCK, LANES, KV_BLOCK), BF16),
        ],
        scratch_shapes=[pltpu.VMEM((tm, D), BF16)],
        compiler_params=_cparams(("parallel", "parallel")),
        name="inproj_even",
    )(x, mod, norm_g.reshape(1, D), w_rows, w_vat, positions.reshape(B, 1, S),
      inv, gain_cols, avg)


def _inproj_odd(x, mod, norm_g, w_in, tm, pending=None):
    B, S, D = x.shape
    w_stack, layer = w_in
    w_rows = _to_bf16(w_stack, layer)
    w_vt = _to_bf16(w_stack[layer, :, 2 * C_WIDTH:3 * C_WIDTH].T)
    n_groups = C_WIDTH // LANES
    row_tile = lambda width: pl.BlockSpec((1, tm, width), lambda b, i: (b, i, 0))
    mod_spec = pl.BlockSpec((3, 1, 1, D), lambda b, i: (0, b, 0, 0))
    const = lambda shape: pl.BlockSpec(shape, lambda b, i: (0,) * len(shape),
                                       pipeline_mode=pl.Buffered(1))
    in_specs = [row_tile(D), mod_spec, const((1, D)), const((D, 4 * C_WIDTH)), const((C_WIDTH, D))]
    args = [x, mod, norm_g.reshape(1, D), w_rows, w_vt]
    out_specs = [
        row_tile(ODD_IN),
        pl.BlockSpec((1, n_groups, tm // KV_BLOCK, LANES, KV_BLOCK), lambda b, i: (b, 0, i, 0, 0)),
    ]
    out_shape = [
        jax.ShapeDtypeStruct((B, S, ODD_IN), BF16),
        jax.ShapeDtypeStruct((B, n_groups, S // KV_BLOCK, LANES, KV_BLOCK), BF16),
    ]
    n_y = 0
    if pending is not None:
        ys, w_out, mod_prev = pending
        n_y = len(ys)
        in_specs = [row_tile(y.shape[-1]) for y in ys] + [const(w_out.shape), mod_spec] + in_specs
        args = list(ys) + [w_out.astype(BF16), mod_prev] + args
        out_specs = [row_tile(D)] + out_specs
        out_shape = [jax.ShapeDtypeStruct((B, S, D), F32)] + out_shape
    outs = pl.pallas_call(
        functools.partial(_inproj_odd_kernel, tm=tm, n_y=n_y),
        grid=(B, S // tm),
        in_specs=in_specs,
        out_specs=out_specs,
        out_shape=out_shape,
        scratch_shapes=[pltpu.VMEM((tm, D), BF16)],
        compiler_params=_cparams(("parallel", "parallel")),
        name="inproj_odd",
    )(*args)
    return (outs[0], outs[1], outs[2]) if n_y else (x, outs[0], outs[1])


def _outproj_kernel(*refs, n_in):
    y_refs = refs[:n_in]
    w_ref, x_ref, mod_ref, o_ref = refs[n_in:]
    o_ref[0] = x_ref[0] + mod_ref[2, 0] * _project_out(y_refs, w_ref)


def _outproj(ys, w_out, x, mod, tm):
    B, S, D = x.shape
    K = w_out.shape[0]
    n_in = len(ys)
    in_specs = [pl.BlockSpec((1, tm, y.shape[-1]), lambda b, i: (b, i, 0)) for y in ys]
    in_specs += [
        pl.BlockSpec((K, D), lambda b, i: (0, 0)),
        pl.BlockSpec((1, tm, D), lambda b, i: (b, i, 0)),
        pl.BlockSpec((3, 1, 1, D), lambda b, i: (0, b, 0, 0)),
    ]
    return pl.pallas_call(
        functools.partial(_outproj_kernel, n_in=n_in),
        grid=(B, S // tm),
        in_specs=in_specs,
        out_specs=pl.BlockSpec((1, tm, D), lambda b, i: (b, i, 0)),
        out_shape=jax.ShapeDtypeStruct((B, S, D), F32),
        compiler_params=_cparams(("parallel", "parallel")),
        name="outproj",
    )(*ys, w_out.astype(BF16), x, mod)


def _diff_attn_kernel(bound_ref, lam_ref, sub_ref, q_ref, k_ref, vt_ref, g_ref, o_ref,
                      qs_scr, s0_scr, s1_scr, smax0_scr, smax1_scr, m_scr, l_scr, acc_scr, *,
                      tq, tk, lambda_init):
    i = pl.program_id(2)
    q = q_ref[0]
    lane = lax.broadcasted_iota(jnp.int32, (tq, LANES), 1)
    zero = jnp.zeros_like(q)
    qs_scr[:tq] = jnp.where(lane < HEAD_DIM, q, zero)
    qs_scr[tq:] = jnp.where(lane >= HEAD_DIM, q, zero)

    slots = ((s0_scr, smax0_scr), (s1_scr, smax1_scr))

    def spans(diag):
        if diag is None:
            return ((0, 2 * tq),)
        return tuple((c * tq + diag * tk, (c + 1) * tq) for c in range(2))

    def scores(j, slot, bounded, diag=None):
        s_ref, smax_ref = slots[slot]
        k = k_ref[0, pl.ds(pl.multiple_of(j * tk, tk), tk), :]
        for a, b in spans(diag):
            s = lax.dot_general(k, qs_scr[a:b], (((1,), (1,)), ((), ())),
                                preferred_element_type=F32)
            s_ref[:, a:b] = s
            if not bounded and diag is None:
                smax_ref[:, a:b] = jnp.max(s, axis=0, keepdims=True)

    def softmax_pv(j, slot, bounded, diag=None):
        s_ref, smax_ref = slots[slot]
        for a, b in spans(diag):
            s = s_ref[:, a:b]
            if diag is not None:
                key = lax.broadcasted_iota(jnp.int32, (tk, b - a), 0) + diag * tk
                qry = (lax.broadcasted_iota(jnp.int32, (tk, b - a), 1) + a) & (tq - 1)
                s = jnp.where(key <= qry, s, NEG_BIG)
            if bounded:
                p = jnp.exp2(s)
                l_scr[:, a:b] += jnp.sum(p, axis=0, keepdims=True)
                acc_scr[:, a:b] += jnp.dot(vt_ref[0, 0, j], p.astype(BF16),
                                           preferred_element_type=F32)
                continue
            smax = smax_ref[:, a:b] if diag is None else jnp.max(s, axis=0, keepdims=True)
            m = m_scr[:, a:b]
            m_new = jnp.maximum(m, smax)
            alpha = jnp.exp2(m - m_new)
            p = jnp.exp2(s - m_new)
            l_scr[:, a:b] = alpha * l_scr[:, a:b] + jnp.sum(p, axis=0, keepdims=True)
            m_scr[:, a:b] = m_new
            pv = jnp.dot(vt_ref[0, 0, j], p.astype(BF16), preferred_element_type=F32)
            acc_scr[:, a:b] = alpha * acc_scr[:, a:b] + pv

    def sweep(bounded):
        m_scr[...] = jnp.full((1, 2 * tq), NEG_BIG, F32)
        l_scr[...] = jnp.zeros((1, 2 * tq), F32)
        acc_scr[...] = jnp.zeros((LANES, 2 * tq), F32)
        n_diag = tq // tk
        n_full = n_diag * i
        scores(0, 0, bounded)

        def body(jj, carry):
            j0 = n_diag * jj
            for d in range(n_diag):
                scores(j0 + d + 1, (d + 1) % 2, bounded)
                softmax_pv(j0 + d, d % 2, bounded)
            return carry

        lax.fori_loop(0, i, body, 0)
        for d in range(n_diag):
            if d + 1 < n_diag:
                scores(n_full + d + 1, (d + 1) % 2, bounded, diag=d + 1)
            softmax_pv(n_full + d, d % 2, bounded, diag=d)

    is_bounded = bound_ref[0] <= LOGIT_BOUND
    pl.when(is_bounded)(lambda: sweep(True))
    pl.when(jnp.logical_not(is_bounded))(lambda: sweep(False))
    l = l_scr[...]
    acc = acc_scr[...]

    lam_rows = lam_ref[...]
    lam = (jnp.exp(jnp.sum(lam_rows[0:1] * lam_rows[1:2], axis=1, keepdims=True))
           - jnp.exp(jnp.sum(lam_rows[2:3] * lam_rows[3:4], axis=1, keepdims=True))
           + lambda_init)
    ot = acc[:, :tq] / l[:, :tq] - lam * (acc[:, tq:] / l[:, tq:])
    ms = jnp.mean(ot * ot, axis=0, keepdims=True)
    yt = ot * lax.rsqrt(ms + EPS) * (sub_ref[...] * (1.0 - lambda_init))
    o_ref[0] = (yt.T * g_ref[0].astype(F32)).astype(BF16)


def _diff_attention(u, vat, logit_bound, lam_rows, subln_g, lambda_init, tq):
    B, S, _ = u.shape
    tk = KV_BLOCK
    assert tq % (2 * tk) == 0 and S % tq == 0
    qb, kb, gb = OFF_QA // LANES, OFF_KA // LANES, OFF_GA // LANES
    stat = pltpu.VMEM((1, 2 * tq), F32)
    score = pltpu.VMEM((tk, 2 * tq), F32)
    return pl.pallas_call(
        functools.partial(_diff_attn_kernel, tq=tq, tk=tk, lambda_init=lambda_init),
        grid=(B, A_HEADS, S // tq),
        in_specs=[
            pl.BlockSpec(memory_space=pltpu.SMEM),
            pl.BlockSpec((4, HEAD_DIM), lambda b, h, i: (0, 0)),
            pl.BlockSpec((LANES, 1), lambda b, h, i: (0, 0)),
            pl.BlockSpec((1, tq, LANES), lambda b, h, i: (b, i, qb + h)),
            pl.BlockSpec((1, S, LANES), lambda b, h, i: (b, 0, kb + h)),
            pl.BlockSpec((1, 1, S // tk, LANES, tk), lambda b, h, i: (b, h, 0, 0, 0)),
            pl.BlockSpec((1, tq, LANES), lambda b, h, i: (b, i, gb + h)),
        ],
        out_specs=pl.BlockSpec((1, tq, LANES), lambda b, h, i: (b, i, h)),
        out_shape=jax.ShapeDtypeStruct((B, S, A_WIDTH), BF16),
        scratch_shapes=[pltpu.VMEM((2 * tq, LANES), BF16), score, score, stat, stat, stat, stat,
                        pltpu.VMEM((LANES, 2 * tq), F32)],
        compiler_params=_cparams(("parallel", "parallel", "arbitrary")),
        name="diff_attention",
    )(logit_bound.reshape(1).astype(F32), lam_rows, subln_g.reshape(LANES, 1), u, u, vat, u)


def _swa_kernel(sink_ref, q_ref, kc_ref, kp_ref, vtc_ref, vtp_ref, g_ref, o_ref, qs_scr, *, t):
    i = pl.program_id(1)
    group = B_HEADS // B_KV_HEADS
    width = B_HEADS * t
    lane = lax.broadcasted_iota(jnp.int32, (t, LANES), 1)
    for h in range(B_HEADS):
        kv = h // group
        qc = q_ref[0, :, (h // 2) * LANES:(h // 2 + 1) * LANES].astype(F32)
        if h % 2 != kv:
            qc = pltpu.roll(qc, HEAD_DIM, 1)
        keep = (lane >= kv * HEAD_DIM) & (lane < (kv + 1) * HEAD_DIM)
        qs_scr[h * t:(h + 1) * t] = jnp.where(keep, qc, 0.0).astype(BF16)
    kk = jnp.concatenate([kp_ref[0], kc_ref[0]], axis=0)
    vvt = jnp.concatenate([vtp_ref[0, 0, 0][:, KV_BLOCK - WINDOW:], vtc_ref[0, 0, 0]], axis=1)
    s = lax.dot_general(kk, qs_scr[...], (((1,), (1,)), ((), ())),
                        preferred_element_type=F32)
    key = lax.broadcasted_iota(jnp.int32, (WINDOW + t, width), 0)
    seg = lax.broadcasted_iota(jnp.int32, (WINDOW + t, width), 1)
    qry = seg & (t - 1)
    mask = (key > qry) & (key <= qry + WINDOW) & jnp.logical_or(i > 0, key >= WINDOW)
    s = jnp.where(mask, s, NEG_BIG)
    head = lax.broadcasted_iota(jnp.int32, (1, width), 1) // t
    sink = jnp.zeros((1, width), F32)
    for h in range(B_HEADS):
        sink = jnp.where(head == h, sink_ref[h] * LOG2E, sink)
    m = jnp.maximum(jnp.max(s, axis=0, keepdims=True), sink)
    e = jnp.exp2(s - m)
    den = jnp.sum(e, axis=0, keepdims=True) + jnp.exp2(sink - m)
    pv = jnp.dot(vvt, e.astype(BF16), preferred_element_type=F32) / den
    for c in range(B_HEADS // 2):
        rows = []
        for h in (2 * c, 2 * c + 1):
            kv = h // group
            rows.append(pv[kv * HEAD_DIM:(kv + 1) * HEAD_DIM, h * t:(h + 1) * t])
        ot = jnp.concatenate(rows, axis=0)
        gate = g_ref[0, :, c * LANES:(c + 1) * LANES].astype(F32)
        o_ref[0, :, c * LANES:(c + 1) * LANES] = (ot.T * gate).astype(BF16)


def _swa_attention(u, vt, sinks):
    B, S, _ = u.shape
    t = KV_BLOCK
    per = t // WINDOW
    qb, gb, kb = OFF_QB // B_WIDTH, OFF_GB // B_WIDTH, OFF_KB // LANES
    assert OFF_QB % B_WIDTH == 0 and OFF_GB % B_WIDTH == 0
    vgroup = A_WIDTH // LANES
    return pl.pallas_call(
        functools.partial(_swa_kernel, t=t),
        grid=(B, S // t),
        in_specs=[
            pl.BlockSpec(memory_space=pltpu.SMEM),
            pl.BlockSpec((1, t, B_WIDTH), lambda b, i: (b, i, qb)),
            pl.BlockSpec((1, t, LANES), lambda b, i: (b, i, kb)),
            pl.BlockSpec((1, WINDOW, LANES), lambda b, i: (b, jnp.maximum(i * per - 1, 0), kb)),
            pl.BlockSpec((1, 1, 1, LANES, t), lambda b, i: (b, vgroup, i, 0, 0)),
            pl.BlockSpec((1, 1, 1, LANES, t), lambda b, i: (b, vgroup, jnp.maximum(i - 1, 0), 0, 0)),
            pl.BlockSpec((1, t, B_WIDTH), lambda b, i: (b, i, gb)),
        ],
        out_specs=pl.BlockSpec((1, t, B_WIDTH), lambda b, i: (b, i, 0)),
        out_shape=jax.ShapeDtypeStruct((B, S, B_WIDTH), BF16),
        scratch_shapes=[pltpu.VMEM((B_HEADS * t, LANES), BF16)],
        compiler_params=_cparams(("parallel", "parallel")),
        name="swa_attention",
    )(sinks.astype(F32), u, u, u, vt, vt, u)


def _stick_kernel(tri_ref, q_ref, k_ref, vt_ref, g_ref, o_ref, qs_scr, run_scr, acc_scr, *, t, pairs):
    i = pl.program_id(2)
    heads = 2 * pairs
    width = heads * t
    lane = lax.broadcasted_iota(jnp.int32, (t, LANES), 1)
    for p in range(pairs):
        q = q_ref[0, :, p * LANES:(p + 1) * LANES]
        zero = jnp.zeros_like(q)
        qs_scr[(2 * p) * t:(2 * p + 1) * t] = jnp.where(lane < HEAD_DIM, q, zero)
        qs_scr[(2 * p + 1) * t:(2 * p + 2) * t] = jnp.where(lane >= HEAD_DIM, q, zero)
    tri = tri_ref[...]
    acc_scr[...] = jnp.zeros((LANES, width), F32)

    def prepare(j, diagonal):
        off = pl.multiple_of(j * t, t)
        z = jnp.concatenate([
            lax.dot_general(k_ref[0, pl.ds(off, t), p * LANES:(p + 1) * LANES],
                            qs_scr[2 * p * t:(2 * p + 2) * t],
                            (((1,), (1,)), ((), ())), preferred_element_type=F32)
            for p in range(pairs)], axis=1)
        if diagonal:
            key = lax.broadcasted_iota(jnp.int32, (t, width), 0)
            qry = lax.broadcasted_iota(jnp.int32, (t, width), 1) & (t - 1)
            z = jnp.where(key < qry, z, NEG_BIG)
        sp = jnp.maximum(z, 0.0) + jnp.log(1.0 + jnp.exp2(-jnp.abs(z))) * LOG2E
        csum = jnp.dot(tri, sp.astype(BF16), preferred_element_type=F32)
        return z, csum

    def apply(j, z, csum, run):
        w = jnp.exp2(z - csum - run).astype(BF16)
        for p in range(pairs):
            cols = slice(2 * p * t, (2 * p + 2) * t)
            acc_scr[:, cols] += jnp.dot(vt_ref[0, p, j], w[:, cols], preferred_element_type=F32)
        return run + csum[0:1, :]

    prev = jnp.maximum(i - 1, 0)
    zd, cd = prepare(i, True)
    zp, cp = prepare(prev, False)
    run = apply(i, zd, cd, jnp.zeros((1, width), F32))
    run = apply(prev, zp, cp, run + jnp.where(i > 0, 0.0, -NEG_BIG))
    run_scr[...] = run

    def cond(c):
        jj, go = c
        return jnp.logical_and(jj < i - 1, go)

    def body(c):
        jj, _ = c
        j = i - 2 - jj
        z, csum = prepare(j, False)
        run = apply(j, z, csum, run_scr[...])
        run_scr[...] = run
        return jj + 1, jnp.min(run) < EXP2_UNDERFLOW

    lax.while_loop(cond, body, (jnp.int32(0), jnp.min(run) < EXP2_UNDERFLOW))
    first = lax.broadcasted_iota(jnp.int32, (LANES, t), 0) < HEAD_DIM
    for p in range(pairs):
        a = 2 * p * t
        ot = jnp.where(first, acc_scr[:, a:a + t], acc_scr[:, a + t:a + 2 * t])
        gate = g_ref[0, :, p * LANES:(p + 1) * LANES].astype(F32)
        o_ref[0, :, p * LANES:(p + 1) * LANES] = (ot.T * gate).astype(BF16)


def _stick_attention(u, vt, pairs):
    B, S, _ = u.shape
    t = KV_BLOCK
    width = pairs * LANES
    groups = C_WIDTH // width
    row = jnp.arange(t)
    tri = (row[None, :] >= row[:, None]).astype(BF16)
    return pl.pallas_call(
        functools.partial(_stick_kernel, t=t, pairs=pairs),
        grid=(B, groups, S // t),
        in_specs=[
            pl.BlockSpec((t, t), lambda b, h, i: (0, 0)),
            pl.BlockSpec((1, t, width), lambda b, h, i: (b, i, h)),
            pl.BlockSpec((1, S, width), lambda b, h, i: (b, 0, groups + h),
                         pipeline_mode=pl.Buffered(1)),
            pl.BlockSpec((1, pairs, S // t, LANES, t), lambda b, h, i: (b, h, 0, 0, 0),
                         pipeline_mode=pl.Buffered(1)),
            pl.BlockSpec((1, t, width), lambda b, h, i: (b, i, 2 * groups + h)),
        ],
        out_specs=pl.BlockSpec((1, t, width), lambda b, h, i: (b, i, h)),
        out_shape=jax.ShapeDtypeStruct((B, S, C_WIDTH), BF16),
        scratch_shapes=[pltpu.VMEM((2 * pairs * t, LANES), BF16),
                        pltpu.VMEM((1, 2 * pairs * t), F32),
                        pltpu.VMEM((LANES, 2 * pairs * t), F32)],
        compiler_params=_cparams(("parallel", "parallel", "arbitrary")),
        name="stick_attention",
    )(tri, u, u, vt, u)


def _tile(n, pref):
    t = min(pref, n)
    assert n % t == 0
    return t


def _even_layer(x, c, positions, layer, norm_g, w_mod, b_mod, w_in, a_q_gain, a_k_gain,
                lq1, lk1, lq2, lk2, a_subln_g, b_q_gain, b_k_gain, b_sinks, w_out):
    B, S, D = x.shape
    mod = _modulation(c, w_mod, b_mod)
    ones = lambda n: jnp.ones((n,), F32)
    gain_cols = jnp.concatenate([
        jnp.tile(a_q_gain * (QK_SCALE * LOG2E), A_QK // HEAD_DIM),
        jnp.tile(a_k_gain, A_QK // HEAD_DIM),
        ones(A_WIDTH),
        jnp.tile(b_q_gain * (QK_SCALE * LOG2E), B_HEADS),
        ones(B_WIDTH),
        jnp.tile(b_k_gain, B_KV_HEADS),
    ]).astype(F32).reshape(1, EVEN_IN)
    u, vat = _inproj_even(x, mod, norm_g, w_in, positions, gain_cols, _tile(S, 512))
    lambda_init = 0.8 - 0.6 * math.exp(-0.3 * layer)
    lam_rows = jnp.stack([lq1, lk1, lq2, lk2]).astype(F32)
    logit_bound = (HEAD_DIM * QK_SCALE * LOG2E * LOGIT_BOUND_MARGIN
                   * jnp.max(jnp.abs(a_q_gain)) * jnp.max(jnp.abs(a_k_gain)))
    ya = _diff_attention(u, vat, logit_bound, lam_rows, a_subln_g.astype(F32), lambda_init,
                         _tile(S, 2048))
    yb = _swa_attention(u, vat, b_sinks)
    return x, ([ya, yb], w_out, mod)


def _odd_layer(x, pending, c, norm_g, w_mod, b_mod, w_in, w_out):
    B, S, D = x.shape
    mod = _modulation(c, w_mod, b_mod)
    x, u, vt = _inproj_odd(x, mod, norm_g, w_in, _tile(S, 512), pending)
    y = _stick_attention(u, vt, pairs=4)
    return x, ([y], w_out, mod)


def kernel(x, c, positions, even_norm_g, even_w_mod, even_b_mod, even_w_in, a_q_gain, a_k_gain,
           a_lambda_q1, a_lambda_k1, a_lambda_q2, a_lambda_k2, a_subln_g, b_q_gain, b_k_gain,
           b_sinks, even_w_out, odd_norm_g, odd_w_mod, odd_b_mod, odd_w_in, odd_w_out):
    depth = even_norm_g.shape[0] + odd_norm_g.shape[0]
    tm = _tile(x.shape[1], 512)
    pending = None
    for layer in range(depth):
        j = layer // 2
        if layer % 2 == 0:
            if pending is not None:
                x = _outproj(pending[0], pending[1], x, pending[2], tm)
            x, pending = _even_layer(
                x, c, positions, layer, even_norm_g[j], (even_w_mod, j), even_b_mod[j],
                (even_w_in, j), a_q_gain[j], a_k_gain[j], a_lambda_q1[j], a_lambda_k1[j],
                a_lambda_q2[j], a_lambda_k2[j], a_subln_g[j], b_q_gain[j], b_k_gain[j],
                b_sinks[j], even_w_out[j])
        else:
            x, pending = _odd_layer(x, pending, c, odd_norm_g[j], (odd_w_mod, j), odd_b_mod[j],
                                    (odd_w_in, j), odd_w_out[j])
    return _outproj(pending[0], pending[1], x, pending[2], tm)
```

```python
import functools
import math

import jax
import jax.numpy as jnp
from jax import lax
from jax.experimental import pallas as pl
from jax.experimental.pallas import tpu as pltpu

F32 = jnp.float32
BF16 = jnp.bfloat16

HEAD_DIM = 64
ROPE_THETA = 10000.0
EPS = 1e-6
WINDOW = 128
LANES = 128
QK_SCALE = HEAD_DIM ** -0.5

A_HEADS = 4
A_QK = A_HEADS * 2 * HEAD_DIM
A_WIDTH = A_HEADS * 2 * HEAD_DIM
B_HEADS = 8
B_KV_HEADS = 2
B_WIDTH = B_HEADS * HEAD_DIM
B_KV = B_KV_HEADS * HEAD_DIM
C_HEADS = 16
C_WIDTH = C_HEADS * HEAD_DIM

W_QA = 0
W_KA = W_QA + A_QK
W_VA = W_KA + A_QK
W_GA = W_VA + A_WIDTH
W_QB = W_GA + A_WIDTH
W_KB = W_QB + B_WIDTH
W_VB = W_KB + B_KV
W_GB = W_VB + B_KV
W_END = W_GB + B_WIDTH

OFF_QA = 0
OFF_KA = OFF_QA + A_QK
OFF_GA = OFF_KA + A_QK
OFF_QB = OFF_GA + A_WIDTH
OFF_GB = OFF_QB + B_WIDTH
OFF_KB = OFF_GB + B_WIDTH
EVEN_IN = OFF_KB + B_KV

EVEN_CHUNKS = (
    (W_GA, OFF_GA, 512, "silu"), (W_GB, OFF_GB, 512, "silu"),
    (W_QA, OFF_QA, 256, "qk"), (W_QA + 256, OFF_QA + 256, 256, "qk"),
    (W_KA, OFF_KA, 256, "qk"), (W_KA + 256, OFF_KA + 256, 256, "qk"),
    (W_QB, OFF_QB, 256, "qk"), (W_QB + 256, OFF_QB + 256, 256, "qk"),
    (W_KB, OFF_KB, 128, "qk"),
)
LOG2E = 1.4426950408889634
KV_BLOCK = 256
ODD_IN = 3 * C_WIDTH
ODD_CHUNKS = tuple(
    (w_i * C_WIDTH + half * 512, o_i * C_WIDTH + half * 512, 512, kind)
    for o_i, (w_i, kind) in enumerate(((0, "scale"), (1, "plain"), (3, "silu")))
    for half in range(2)
)

NEG_BIG = -1e30
EXP2_UNDERFLOW = 151.0
LOGIT_BOUND = 64.0
LOGIT_BOUND_MARGIN = 1.01
VMEM_LIMIT = 48 * 1024 * 1024


def _cparams(sem):
    return pltpu.CompilerParams(dimension_semantics=sem, vmem_limit_bytes=VMEM_LIMIT)


def _cast_kernel(w_ref, o_ref):
    o_ref[...] = w_ref[...].astype(BF16)


def _to_bf16(w, layer=None, rows=128):
    K, N = w.shape[-2:]
    if layer is None:
        in_spec = pl.BlockSpec((rows, N), lambda r: (r, 0))
    else:
        in_spec = pl.BlockSpec((None, rows, N), lambda r: (layer, r, 0))
    return pl.pallas_call(
        _cast_kernel,
        grid=(K // rows,),
        in_specs=[in_spec],
        out_specs=pl.BlockSpec((rows, N), lambda r: (r, 0)),
        out_shape=jax.ShapeDtypeStruct((K, N), BF16),
        compiler_params=_cparams(("parallel",)),
        name="weight_cast",
    )(w)


def _mod_kernel(c_ref, w_ref, b_ref, o_ref):
    c = c_ref[...]
    sc = c * jax.nn.sigmoid(c)
    o_ref[0] = jnp.dot(sc, w_ref[...], preferred_element_type=F32,
                       precision=lax.Precision.HIGHEST) + b_ref[0]


def _modulation(c, w_mod, b_mod):
    B, D = c.shape
    rows = 8
    cp = jnp.pad(c, ((0, rows - B), (0, 0)))
    w_mod, layer = w_mod
    out = pl.pallas_call(
        _mod_kernel,
        grid=(3,),
        in_specs=[
            pl.BlockSpec((rows, D), lambda j: (0, 0)),
            pl.BlockSpec((None, D, D), lambda j: (layer, 0, j)),
            pl.BlockSpec((1, 1, D), lambda j: (j, 0, 0)),
        ],
        out_specs=pl.BlockSpec((1, rows, D), lambda j: (j, 0, 0)),
        out_shape=jax.ShapeDtypeStruct((3, rows, D), F32),
        compiler_params=_cparams(("arbitrary",)),
        name="modulation",
    )(cp, w_mod, b_mod.reshape(3, 1, D))
    return out[:, :B].reshape(3, B, 1, D)


def _project_out(y_refs, w_ref):
    acc = None
    off = 0
    for y_ref in y_refs:
        wd = y_ref.shape[-1]
        part = jnp.dot(y_ref[0], w_ref[off:off + wd, :], preferred_element_type=F32)
        acc = part if acc is None else acc + part
        off += wd
    return acc


def _modulated_rows(x, mod_ref, ng_ref):
    ms = jnp.mean(x * x, axis=-1, keepdims=True)
    y = x * lax.rsqrt(ms + EPS) * ng_ref[...]
    return y * (1.0 + mod_ref[1, 0]) + mod_ref[0, 0]


def _store_transposed_values(wvt_ref, h_scr, vt_ref, tm):
    vt = lax.dot_general(wvt_ref[...], h_scr[...], (((1,), (1,)), ((), ())),
                         preferred_element_type=F32)
    for g in range(vt.shape[0] // LANES):
        for cb in range(tm // KV_BLOCK):
            vt_ref[0, g, cb] = vt[g * LANES:(g + 1) * LANES,
                                  cb * KV_BLOCK:(cb + 1) * KV_BLOCK].astype(BF16)


def _inproj_even_kernel(x_ref, mod_ref, ng_ref, w_ref, wvt_ref, pos_ref, inv_ref, gain_ref, p_ref,
                        o_ref, vt_ref, h_scr, *, tm):
    h_scr[...] = _modulated_rows(x_ref[0], mod_ref, ng_ref).astype(BF16)
    _store_transposed_values(wvt_ref, h_scr, vt_ref, tm)
    ang_t = pos_ref[0].astype(F32) * inv_ref[...]
    reps = LANES // ang_t.shape[0]
    cos = jnp.concatenate([jnp.cos(ang_t)] * reps, axis=0).T
    sin = jnp.concatenate([jnp.sin(ang_t)] * reps, axis=0).T
    lane = lax.broadcasted_iota(jnp.int32, (tm, LANES), 1)
    first = (lane & (HEAD_DIM // 2)) == 0
    sin_s = jnp.where(first, -sin, sin)
    for wcol, start, width, kind in EVEN_CHUNKS:
        u = jnp.dot(h_scr[...], w_ref[:, wcol:wcol + width], preferred_element_type=F32)
        if kind == "qk":
            ms = jnp.dot((u * u).astype(BF16), p_ref[:width, :width], preferred_element_type=F32)
            un = u * lax.rsqrt(ms + EPS) * gain_ref[:, start:start + width]
            for s in range(width // LANES):
                xs = un[:, s * LANES:(s + 1) * LANES]
                rot = jnp.where(first, pltpu.roll(xs, LANES - HEAD_DIM // 2, 1),
                                pltpu.roll(xs, HEAD_DIM // 2, 1))
                c0 = start + s * LANES
                o_ref[0, :, c0:c0 + LANES] = (xs * cos + rot * sin_s).astype(BF16)
        elif kind == "silu":
            o_ref[0, :, start:start + width] = (u * jax.nn.sigmoid(u)).astype(BF16)
        else:
            o_ref[0, :, start:start + width] = u.astype(BF16)


def _inproj_odd_kernel(*refs, tm, n_y):
    if n_y:
        y_refs, (wout_ref, modp_ref) = refs[:n_y], refs[n_y:n_y + 2]
        x_ref, mod_ref, ng_ref, w_ref, wvt_ref, x1_ref, o_ref, vt_ref, h_scr = refs[n_y + 2:]
        x = x_ref[0] + modp_ref[2, 0] * _project_out(y_refs, wout_ref)
        x1_ref[0] = x
    else:
        x_ref, mod_ref, ng_ref, w_ref, wvt_ref, o_ref, vt_ref, h_scr = refs
        x = x_ref[0]
    h_scr[...] = _modulated_rows(x, mod_ref, ng_ref).astype(BF16)
    _store_transposed_values(wvt_ref, h_scr, vt_ref, tm)
    for wcol, start, width, kind in ODD_CHUNKS:
        u = jnp.dot(h_scr[...], w_ref[:, wcol:wcol + width], preferred_element_type=F32)
        if kind == "scale":
            u = u * (QK_SCALE * LOG2E)
        elif kind == "silu":
            u = u * jax.nn.sigmoid(u)
        o_ref[0, :, start:start + width] = u.astype(BF16)


def _inproj_even(x, mod, norm_g, w_in, positions, gain_cols, tm):
    B, S, D = x.shape
    w_stack, layer = w_in
    w_rows = _to_bf16(w_stack, layer)
    w_vat = _to_bf16(jnp.concatenate([w_stack[layer, :, W_VA:W_GA],
                                      w_stack[layer, :, W_VB:W_GB]], axis=1).T)
    n_groups = (A_WIDTH + B_KV) // LANES
    half = HEAD_DIM // 2
    inv = (ROPE_THETA ** (-jnp.arange(half, dtype=F32) / half)).reshape(half, 1)
    blk = jnp.arange(256) // HEAD_DIM
    avg = jnp.where(blk[:, None] == blk[None, :], 1.0 / HEAD_DIM, 0.0).astype(BF16)
    return pl.pallas_call(
        functools.partial(_inproj_even_kernel, tm=tm),
        grid=(B, S // tm),
        in_specs=[
            pl.BlockSpec((1, tm, D), lambda b, i: (b, i, 0)),
            pl.BlockSpec((3, 1, 1, D), lambda b, i: (0, b, 0, 0)),
            pl.BlockSpec((1, D), lambda b, i: (0, 0)),
            pl.BlockSpec((D, W_END), lambda b, i: (0, 0)),
            pl.BlockSpec((A_WIDTH + B_KV, D), lambda b, i: (0, 0)),
            pl.BlockSpec((1, 1, tm), lambda b, i: (b, 0, i)),
            pl.BlockSpec((half, 1), lambda b, i: (0, 0)),
            pl.BlockSpec((1, EVEN_IN), lambda b, i: (0, 0)),
            pl.BlockSpec((256, 256), lambda b, i: (0, 0)),
        ],
        out_specs=[
            pl.BlockSpec((1, tm, EVEN_IN), lambda b, i: (b, i, 0)),
            pl.BlockSpec((1, n_groups, tm // KV_BLOCK, LANES, KV_BLOCK), lambda b, i: (b, 0, i, 0, 0)),
        ],
        out_shape=[
            jax.ShapeDtypeStruct((B, S, EVEN_IN), BF16),
            jax.ShapeDtypeStruct((B, n_groups, S // KV_BLOCK, LANES, KV_BLOCK), BF16),
        ],
        scratch_shapes=[pltpu.VMEM((tm, D), BF16)],
        compiler_params=_cparams(("parallel", "parallel")),
        name="inproj_even",
    )(x, mod, norm_g.reshape(1, D), w_rows, w_vat, positions.reshape(B, 1, S),
      inv, gain_cols, avg)


def _inproj_odd(x, mod, norm_g, w_in, tm, pending=None):
    B, S, D = x.shape
    w_stack, layer = w_in
    w_rows = _to_bf16(w_stack, layer)
    w_vt = _to_bf16(w_stack[layer, :, 2 * C_WIDTH:3 * C_WIDTH].T)
    n_groups = C_WIDTH // LANES
    row_tile = lambda width: pl.BlockSpec((1, tm, width), lambda b, i: (b, i, 0))
    mod_spec = pl.BlockSpec((3, 1, 1, D), lambda b, i: (0, b, 0, 0))
    const = lambda shape: pl.BlockSpec(shape, lambda b, i: (0,) * len(shape),
                                       pipeline_mode=pl.Buffered(1))
    in_specs = [row_tile(D), mod_spec, const((1, D)), const((D, 4 * C_WIDTH)), const((C_WIDTH, D))]
    args = [x, mod, norm_g.reshape(1, D), w_rows, w_vt]
    out_specs = [
        row_tile(ODD_IN),
        pl.BlockSpec((1, n_groups, tm // KV_BLOCK, LANES, KV_BLOCK), lambda b, i: (b, 0, i, 0, 0)),
    ]
    out_shape = [
        jax.ShapeDtypeStruct((B, S, ODD_IN), BF16),
        jax.ShapeDtypeStruct((B, n_groups, S // KV_BLOCK, LANES, KV_BLOCK), BF16),
    ]
    n_y = 0
    if pending is not None:
        ys, w_out, mod_prev = pending
        n_y = len(ys)
        in_specs = [row_tile(y.shape[-1]) for y in ys] + [const(w_out.shape), mod_spec] + in_specs
        args = list(ys) + [w_out.astype(BF16), mod_prev] + args
        out_specs = [row_tile(D)] + out_specs
        out_shape = [jax.ShapeDtypeStruct((B, S, D), F32)] + out_shape
    outs = pl.pallas_call(
        functools.partial(_inproj_odd_kernel, tm=tm, n_y=n_y),
        grid=(B, S // tm),
        in_specs=in_specs,
        out_specs=out_specs,
        out_shape=out_shape,
        scratch_shapes=[pltpu.VMEM((tm, D), BF16)],
        compiler_params=_cparams(("parallel", "parallel")),
        name="inproj_odd",
    )(*args)
    return (outs[0], outs[1], outs[2]) if n_y else (x, outs[0], outs[1])


def _outproj_kernel(*refs, n_in):
    y_refs = refs[:n_in]
    w_ref, x_ref, mod_ref, o_ref = refs[n_in:]
    o_ref[0] = x_ref[0] + mod_ref[2, 0] * _project_out(y_refs, w_ref)


def _outproj(ys, w_out, x, mod, tm):
    B, S, D = x.shape
    K = w_out.shape[0]
    n_in = len(ys)
    in_specs = [pl.BlockSpec((1, tm, y.shape[-1]), lambda b, i: (b, i, 0)) for y in ys]
    in_specs += [
        pl.BlockSpec((K, D), lambda b, i: (0, 0)),
        pl.BlockSpec((1, tm, D), lambda b, i: (b, i, 0)),
        pl.BlockSpec((3, 1, 1, D), lambda b, i: (0, b, 0, 0)),
    ]
    return pl.pallas_call(
        functools.partial(_outproj_kernel, n_in=n_in),
        grid=(B, S // tm),
        in_specs=in_specs,
        out_specs=pl.BlockSpec((1, tm, D), lambda b, i: (b, i, 0)),
        out_shape=jax.ShapeDtypeStruct((B, S, D), F32),
        compiler_params=_cparams(("parallel", "parallel")),
        name="outproj",
    )(*ys, w_out.astype(BF16), x, mod)


def _diff_attn_kernel(bound_ref, lam_ref, sub_ref, q_ref, k_ref, vt_ref, g_ref, o_ref,
                      qs_scr, s0_scr, s1_scr, smax0_scr, smax1_scr, m_scr, l_scr, acc_scr, *,
                      tq, tk, lambda_init):
    i = pl.program_id(2)
    q = q_ref[0]
    lane = lax.broadcasted_iota(jnp.int32, (tq, LANES), 1)
    zero = jnp.zeros_like(q)
    qs_scr[:tq] = jnp.where(lane < HEAD_DIM, q, zero)
    qs_scr[tq:] = jnp.where(lane >= HEAD_DIM, q, zero)

    slots = ((s0_scr, smax0_scr), (s1_scr, smax1_scr))

    def spans(diag):
        if diag is None:
            return ((0, 2 * tq),)
        return tuple((c * tq + diag * tk, (c + 1) * tq) for c in range(2))

    def scores(j, slot, bounded, diag=None):
        s_ref, smax_ref = slots[slot]
        k = k_ref[0, pl.ds(pl.multiple_of(j * tk, tk), tk), :]
        for a, b in spans(diag):
            s = lax.dot_general(k, qs_scr[a:b], (((1,), (1,)), ((), ())),
                                preferred_element_type=F32)
            s_ref[:, a:b] = s
            if not bounded and diag is None:
                smax_ref[:, a:b] = jnp.max(s, axis=0, keepdims=True)

    def softmax_pv(j, slot, bounded, diag=None):
        s_ref, smax_ref = slots[slot]
        for a, b in spans(diag):
            s = s_ref[:, a:b]
            if diag is not None:
                key = lax.broadcasted_iota(jnp.int32, (tk, b - a), 0) + diag * tk
                qry = (lax.broadcasted_iota(jnp.int32, (tk, b - a), 1) + a) & (tq - 1)
                s = jnp.where(key <= qry, s, NEG_BIG)
            if bounded:
                p = jnp.exp2(s)
                l_scr[:, a:b] += jnp.sum(p, axis=0, keepdims=True)
                acc_scr[:, a:b] += jnp.dot(vt_ref[0, 0, j], p.astype(BF16),
                                           preferred_element_type=F32)
                continue
            smax = smax_ref[:, a:b] if diag is None else jnp.max(s, axis=0, keepdims=True)
            m = m_scr[:, a:b]
            m_new = jnp.maximum(m, smax)
            alpha = jnp.exp2(m - m_new)
            p = jnp.exp2(s - m_new)
            l_scr[:, a:b] = alpha * l_scr[:, a:b] + jnp.sum(p, axis=0, keepdims=True)
            m_scr[:, a:b] = m_new
            pv = jnp.dot(vt_ref[0, 0, j], p.astype(BF16), preferred_element_type=F32)
            acc_scr[:, a:b] = alpha * acc_scr[:, a:b] + pv

    def sweep(bounded):
        m_scr[...] = jnp.full((1, 2 * tq), NEG_BIG, F32)
        l_scr[...] = jnp.zeros((1, 2 * tq), F32)
        acc_scr[...] = jnp.zeros((LANES, 2 * tq), F32)
        n_diag = tq // tk
        n_full = n_diag * i
        scores(0, 0, bounded)

        def body(jj, carry):
            j0 = n_diag * jj
            for d in range(n_diag):
                scores(j0 + d + 1, (d + 1) % 2, bounded)
                softmax_pv(j0 + d, d % 2, bounded)
            return carry

        lax.fori_loop(0, i, body, 0)
        for d in range(n_diag):
            if d + 1 < n_diag:
                scores(n_full + d + 1, (d + 1) % 2, bounded, diag=d + 1)
            softmax_pv(n_full + d, d % 2, bounded, diag=d)

    is_bounded = bound_ref[0] <= LOGIT_BOUND
    pl.when(is_bounded)(lambda: sweep(True))
    pl.when(jnp.logical_not(is_bounded))(lambda: sweep(False))
    l = l_scr[...]
    acc = acc_scr[...]

    lam_rows = lam_ref[...]
    lam = (jnp.exp(jnp.sum(lam_rows[0:1] * lam_rows[1:2], axis=1, keepdims=True))
           - jnp.exp(jnp.sum(lam_rows[2:3] * lam_rows[3:4], axis=1, keepdims=True))
           + lambda_init)
    ot = acc[:, :tq] / l[:, :tq] - lam * (acc[:, tq:] / l[:, tq:])
    ms = jnp.mean(ot * ot, axis=0, keepdims=True)
    yt = ot * lax.rsqrt(ms + EPS) * (sub_ref[...] * (1.0 - lambda_init))
    o_ref[0] = (yt.T * g_ref[0].astype(F32)).astype(BF16)


def _diff_attention(u, vat, logit_bound, lam_rows, subln_g, lambda_init, tq):
    B, S, _ = u.shape
    tk = KV_BLOCK
    assert tq % (2 * tk) == 0 and S % tq == 0
    qb, kb, gb = OFF_QA // LANES, OFF_KA // LANES, OFF_GA // LANES
    stat = pltpu.VMEM((1, 2 * tq), F32)
    score = pltpu.VMEM((tk, 2 * tq), F32)
    return pl.pallas_call(
        functools.partial(_diff_attn_kernel, tq=tq, tk=tk, lambda_init=lambda_init),
        grid=(B, A_HEADS, S // tq),
        in_specs=[
            pl.BlockSpec(memory_space=pltpu.SMEM),
            pl.BlockSpec((4, HEAD_DIM), lambda b, h, i: (0, 0)),
            pl.BlockSpec((LANES, 1), lambda b, h, i: (0, 0)),
            pl.BlockSpec((1, tq, LANES), lambda b, h, i: (b, i, qb + h)),
            pl.BlockSpec((1, S, LANES), lambda b, h, i: (b, 0, kb + h)),
            pl.BlockSpec((1, 1, S // tk, LANES, tk), lambda b, h, i: (b, h, 0, 0, 0)),
            pl.BlockSpec((1, tq, LANES), lambda b, h, i: (b, i, gb + h)),
        ],
        out_specs=pl.BlockSpec((1, tq, LANES), lambda b, h, i: (b, i, h)),
        out_shape=jax.ShapeDtypeStruct((B, S, A_WIDTH), BF16),
        scratch_shapes=[pltpu.VMEM((2 * tq, LANES), BF16), score, score, stat, stat, stat, stat,
                        pltpu.VMEM((LANES, 2 * tq), F32)],
        compiler_params=_cparams(("parallel", "parallel", "arbitrary")),
        name="diff_attention",
    )(logit_bound.reshape(1).astype(F32), lam_rows, subln_g.reshape(LANES, 1), u, u, vat, u)


def _swa_kernel(sink_ref, q_ref, kc_ref, kp_ref, vtc_ref, vtp_ref, g_ref, o_ref, qs_scr, *, t):
    i = pl.program_id(1)
    group = B_HEADS // B_KV_HEADS
    width = B_HEADS * t
    lane = lax.broadcasted_iota(jnp.int32, (t, LANES), 1)
    for h in range(B_HEADS):
        kv = h // group
        qc = q_ref[0, :, (h // 2) * LANES:(h // 2 + 1) * LANES].astype(F32)
        if h % 2 != kv:
            qc = pltpu.roll(qc, HEAD_DIM, 1)
        keep = (lane >= kv * HEAD_DIM) & (lane < (kv + 1) * HEAD_DIM)
        qs_scr[h * t:(h + 1) * t] = jnp.where(keep, qc, 0.0).astype(BF16)
    kk = jnp.concatenate([kp_ref[0], kc_ref[0]], axis=0)
    vvt = jnp.concatenate([vtp_ref[0, 0, 0][:, KV_BLOCK - WINDOW:], vtc_ref[0, 0, 0]], axis=1)
    s = lax.dot_general(kk, qs_scr[...], (((1,), (1,)), ((), ())),
                        preferred_element_type=F32)
    key = lax.broadcasted_iota(jnp.int32, (WINDOW + t, width), 0)
    seg = lax.broadcasted_iota(jnp.int32, (WINDOW + t, width), 1)
    qry = seg & (t - 1)
    mask = (key > qry) & (key <= qry + WINDOW) & jnp.logical_or(i > 0, key >= WINDOW)
    s = jnp.where(mask, s, NEG_BIG)
    head = lax.broadcasted_iota(jnp.int32, (1, width), 1) // t
    sink = jnp.zeros((1, width), F32)
    for h in range(B_HEADS):
        sink = jnp.where(head == h, sink_ref[h] * LOG2E, sink)
    m = jnp.maximum(jnp.max(s, axis=0, keepdims=True), sink)
    e = jnp.exp2(s - m)
    den = jnp.sum(e, axis=0, keepdims=True) + jnp.exp2(sink - m)
    pv = jnp.dot(vvt, e.astype(BF16), preferred_element_type=F32) / den
    for c in range(B_HEADS // 2):
        rows = []
        for h in (2 * c, 2 * c + 1):
            kv = h // group
            rows.append(pv[kv * HEAD_DIM:(kv + 1) * HEAD_DIM, h * t:(h + 1) * t])
        ot = jnp.concatenate(rows, axis=0)
        gate = g_ref[0, :, c * LANES:(c + 1) * LANES].astype(F32)
        o_ref[0, :, c * LANES:(c + 1) * LANES] = (ot.T * gate).astype(BF16)


def _swa_attention(u, vt, sinks):
    B, S, _ = u.shape
    t = KV_BLOCK
    per = t // WINDOW
    qb, gb, kb = OFF_QB // B_WIDTH, OFF_GB // B_WIDTH, OFF_KB // LANES
    assert OFF_QB % B_WIDTH == 0 and OFF_GB % B_WIDTH == 0
    vgroup = A_WIDTH // LANES
    return pl.pallas_call(
        functools.partial(_swa_kernel, t=t),
        grid=(B, S // t),
        in_specs=[
            pl.BlockSpec(memory_space=pltpu.SMEM),
            pl.BlockSpec((1, t, B_WIDTH), lambda b, i: (b, i, qb)),
            pl.BlockSpec((1, t, LANES), lambda b, i: (b, i, kb)),
            pl.BlockSpec((1, WINDOW, LANES), lambda b, i: (b, jnp.maximum(i * per - 1, 0), kb)),
            pl.BlockSpec((1, 1, 1, LANES, t), lambda b, i: (b, vgroup, i, 0, 0)),
            pl.BlockSpec((1, 1, 1, LANES, t), lambda b, i: (b, vgroup, jnp.maximum(i - 1, 0), 0, 0)),
            pl.BlockSpec((1, t, B_WIDTH), lambda b, i: (b, i, gb)),
        ],
        out_specs=pl.BlockSpec((1, t, B_WIDTH), lambda b, i: (b, i, 0)),
        out_shape=jax.ShapeDtypeStruct((B, S, B_WIDTH), BF16),
        scratch_shapes=[pltpu.VMEM((B_HEADS * t, LANES), BF16)],
        compiler_params=_cparams(("parallel", "parallel")),
        name="swa_attention",
    )(sinks.astype(F32), u, u, u, vt, vt, u)


def _stick_kernel(tri_ref, q_ref, k_ref, vt_ref, g_ref, o_ref, qs_scr, run_scr, acc_scr, *, t, pairs):
    i = pl.program_id(2)
    heads = 2 * pairs
    width = heads * t
    lane = lax.broadcasted_iota(jnp.int32, (t, LANES), 1)
    for p in range(pairs):
        q = q_ref[0, :, p * LANES:(p + 1) * LANES]
        zero = jnp.zeros_like(q)
        qs_scr[(2 * p) * t:(2 * p + 1) * t] = jnp.where(lane < HEAD_DIM, q, zero)
        qs_scr[(2 * p + 1) * t:(2 * p + 2) * t] = jnp.where(lane >= HEAD_DIM, q, zero)
    tri = tri_ref[...]
    acc_scr[...] = jnp.zeros((LANES, width), F32)

    def prepare(j, diagonal):
        off = pl.multiple_of(j * t, t)
        z = jnp.concatenate([
            lax.dot_general(k_ref[0, pl.ds(off, t), p * LANES:(p + 1) * LANES],
                            qs_scr[2 * p * t:(2 * p + 2) * t],
                            (((1,), (1,)), ((), ())), preferred_element_type=F32)
            for p in range(pairs)], axis=1)
        if diagonal:
            key = lax.broadcasted_iota(jnp.int32, (t, width), 0)
            qry = lax.broadcasted_iota(jnp.int32, (t, width), 1) & (t - 1)
            z = jnp.where(key < qry, z, NEG_BIG)
        sp = jnp.maximum(z, 0.0) + jnp.log(1.0 + jnp.exp2(-jnp.abs(z))) * LOG2E
        csum = jnp.dot(tri, sp.astype(BF16), preferred_element_type=F32)
        return z, csum

    def apply(j, z, csum, run):
        w = jnp.exp2(z - csum - run).astype(BF16)
        for p in range(pairs):
            cols = slice(2 * p * t, (2 * p + 2) * t)
            acc_scr[:, cols] += jnp.dot(vt_ref[0, p, j], w[:, cols], preferred_element_type=F32)
        return run + csum[0:1, :]

    prev = jnp.maximum(i - 1, 0)
    zd, cd = prepare(i, True)
    zp, cp = prepare(prev, False)
    run = apply(i, zd, cd, jnp.zeros((1, width), F32))
    run = apply(prev, zp, cp, run + jnp.where(i > 0, 0.0, -NEG_BIG))
    run_scr[...] = run

    def cond(c):
        jj, go = c
        return jnp.logical_and(jj < i - 1, go)

    def body(c):
        jj, _ = c
        j = i - 2 - jj
        z, csum = prepare(j, False)
        run = apply(j, z, csum, run_scr[...])
        run_scr[...] = run
        return jj + 1, jnp.min(run) < EXP2_UNDERFLOW

    lax.while_loop(cond, body, (jnp.int32(0), jnp.min(run) < EXP2_UNDERFLOW))
    first = lax.broadcasted_iota(jnp.int32, (LANES, t), 0) < HEAD_DIM
    for p in range(pairs):
        a = 2 * p * t
        ot = jnp.where(first, acc_scr[:, a:a + t], acc_scr[:, a + t:a + 2 * t])
        gate = g_ref[0, :, p * LANES:(p + 1) * LANES].astype(F32)
        o_ref[0, :, p * LANES:(p + 1) * LANES] = (ot.T * gate).astype(BF16)


def _stick_attention(u, vt, pairs):
    B, S, _ = u.shape
    t = KV_BLOCK
    width = pairs * LANES
    groups = C_WIDTH // width
    row = jnp.arange(t)
    tri = (row[None, :] >= row[:, None]).astype(BF16)
    return pl.pallas_call(
        functools.partial(_stick_kernel, t=t, pairs=pairs),
        grid=(B, groups, S // t),
        in_specs=[
            pl.BlockSpec((t, t), lambda b, h, i: (0, 0)),
            pl.BlockSpec((1, t, width), lambda b, h, i: (b, i, h)),
            pl.BlockSpec((1, S, width), lambda b, h, i: (b, 0, groups + h)),
            pl.BlockSpec((1, pairs, S // t, LANES, t), lambda b, h, i: (b, h, 0, 0, 0)),
            pl.BlockSpec((1, t, width), lambda b, h, i: (b, i, 2 * groups + h)),
        ],
        out_specs=pl.BlockSpec((1, t, width), lambda b, h, i: (b, i, h)),
        out_shape=jax.ShapeDtypeStruct((B, S, C_WIDTH), BF16),
        scratch_shapes=[pltpu.VMEM((2 * pairs * t, LANES), BF16),
                        pltpu.VMEM((1, 2 * pairs * t), F32),
                        pltpu.VMEM((LANES, 2 * pairs * t), F32)],
        compiler_params=_cparams(("parallel", "parallel", "arbitrary")),
        name="stick_attention",
    )(tri, u, u, vt, u)


def _tile(n, pref):
    t = min(pref, n)
    assert n % t == 0
    return t


def _even_layer(x, c, positions, layer, norm_g, w_mod, b_mod, w_in, a_q_gain, a_k_gain,
                lq1, lk1, lq2, lk2, a_subln_g, b_q_gain, b_k_gain, b_sinks, w_out):
    B, S, D = x.shape
    mod = _modulation(c, w_mod, b_mod)
    ones = lambda n: jnp.ones((n,), F32)
    gain_cols = jnp.concatenate([
        jnp.tile(a_q_gain * (QK_SCALE * LOG2E), A_QK // HEAD_DIM),
        jnp.tile(a_k_gain, A_QK // HEAD_DIM),
        ones(A_WIDTH),
        jnp.tile(b_q_gain * (QK_SCALE * LOG2E), B_HEADS),
        ones(B_WIDTH),
        jnp.tile(b_k_gain, B_KV_HEADS),
    ]).astype(F32).reshape(1, EVEN_IN)
    u, vat = _inproj_even(x, mod, norm_g, w_in, positions, gain_cols, _tile(S, 512))
    lambda_init = 0.8 - 0.6 * math.exp(-0.3 * layer)
    lam_rows = jnp.stack([lq1, lk1, lq2, lk2]).astype(F32)
    logit_bound = (HEAD_DIM * QK_SCALE * LOG2E * LOGIT_BOUND_MARGIN
                   * jnp.max(jnp.abs(a_q_gain)) * jnp.max(jnp.abs(a_k_gain)))
    ya = _diff_attention(u, vat, logit_bound, lam_rows, a_subln_g.astype(F32), lambda_init,
                         _tile(S, 2048))
    yb = _swa_attention(u, vat, b_sinks)
    return x, ([ya, yb], w_out, mod)


def _odd_layer(x, pending, c, norm_g, w_mod, b_mod, w_in, w_out):
    B, S, D = x.shape
    mod = _modulation(c, w_mod, b_mod)
    x, u, vt = _inproj_odd(x, mod, norm_g, w_in, _tile(S, 512), pending)
    y = _stick_attention(u, vt, pairs=4)
    return x, ([y], w_out, mod)


def kernel(x, c, positions, even_norm_g, even_w_mod, even_b_mod, even_w_in, a_q_gain, a_k_gain,
           a_lambda_q1, a_lambda_k1, a_lambda_q2, a_lambda_k2, a_subln_g, b_q_gain, b_k_gain,
           b_sinks, even_w_out, odd_norm_g, odd_w_mod, odd_b_mod, odd_w_in, odd_w_out):
    depth = even_norm_g.shape[0] + odd_norm_g.shape[0]
    tm = _tile(x.shape[1], 512)
    pending = None
    for layer in range(depth):
        j = layer // 2
        if layer % 2 == 0:
            if pending is not None:
                x = _outproj(pending[0], pending[1], x, pending[2], tm)
            x, pending = _even_layer(
                x, c, positions, layer, even_norm_g[j], (even_w_mod, j), even_b_mod[j],
                (even_w_in, j), a_q_gain[j], a_k_gain[j], a_lambda_q1[j], a_lambda_k1[j],
                a_lambda_q2[j], a_lambda_k2[j], a_subln_g[j], b_q_gain[j], b_k_gain[j],
                b_sinks[j], even_w_out[j])
        else:
            x, pending = _odd_layer(x, pending, c, odd_norm_g[j], (odd_w_mod, j), odd_b_mod[j],
                                    (odd_w_in, j), odd_w_out[j])
    return _outproj(pending[0], pending[1], x, pending[2], tm)
```

```python
import functools
import math

import jax
import jax.numpy as jnp
from jax import lax
from jax.experimental import pallas as pl
from jax.experimental.pallas import tpu as pltpu

F32 = jnp.float32
BF16 = jnp.bfloat16

HEAD_DIM = 64
ROPE_THETA = 10000.0
EPS = 1e-6
WINDOW = 128
LANES = 128
QK_SCALE = HEAD_DIM ** -0.5

A_HEADS = 4
A_QK = A_HEADS * 2 * HEAD_DIM
A_WIDTH = A_HEADS * 2 * HEAD_DIM
B_HEADS = 8
B_KV_HEADS = 2
B_WIDTH = B_HEADS * HEAD_DIM
B_KV = B_KV_HEADS * HEAD_DIM
C_HEADS = 16
C_WIDTH = C_HEADS * HEAD_DIM

W_QA = 0
W_KA = W_QA + A_QK
W_VA = W_KA + A_QK
W_GA = W_VA + A_WIDTH
W_QB = W_GA + A_WIDTH
W_KB = W_QB + B_WIDTH
W_VB = W_KB + B_KV
W_GB = W_VB + B_KV
W_END = W_GB + B_WIDTH

OFF_QA = 0
OFF_KA = OFF_QA + A_QK
OFF_GA = OFF_KA + A_QK
OFF_QB = OFF_GA + A_WIDTH
OFF_GB = OFF_QB + B_WIDTH
OFF_KB = OFF_GB + B_WIDTH
EVEN_IN = OFF_KB + B_KV

EVEN_CHUNKS = (
    (W_GA, OFF_GA, 512, "silu"), (W_GB, OFF_GB, 512, "silu"),
    (W_QA, OFF_QA, 256, "qk"), (W_QA + 256, OFF_QA + 256, 256, "qk"),
    (W_KA, OFF_KA, 256, "qk"), (W_KA + 256, OFF_KA + 256, 256, "qk"),
    (W_QB, OFF_QB, 256, "qk"), (W_QB + 256, OFF_QB + 256, 256, "qk"),
    (W_KB, OFF_KB, 128, "qk"),
)
LOG2E = 1.4426950408889634
KV_BLOCK = 256
ODD_IN = 3 * C_WIDTH
ODD_CHUNKS = tuple(
    (w_i * C_WIDTH + half * 512, o_i * C_WIDTH + half * 512, 512, kind)
    for o_i, (w_i, kind) in enumerate(((0, "scale"), (1, "plain"), (3, "silu")))
    for half in range(2)
)

NEG_BIG = -1e30
EXP2_UNDERFLOW = 151.0
LOGIT_BOUND = 64.0
LOGIT_BOUND_MARGIN = 1.01
VMEM_LIMIT = 48 * 1024 * 1024


def _cparams(sem):
    return pltpu.CompilerParams(dimension_semantics=sem, vmem_limit_bytes=VMEM_LIMIT)


def _cast_kernel(w_ref, o_ref):
    o_ref[...] = w_ref[...].astype(BF16)


def _to_bf16(w, layer=None, rows=128):
    K, N = w.shape[-2:]
    if layer is None:
        in_spec = pl.BlockSpec((rows, N), lambda r: (r, 0))
    else:
        in_spec = pl.BlockSpec((None, rows, N), lambda r: (layer, r, 0))
    return pl.pallas_call(
        _cast_kernel,
        grid=(K // rows,),
        in_specs=[in_spec],
        out_specs=pl.BlockSpec((rows, N), lambda r: (r, 0)),
        out_shape=jax.ShapeDtypeStruct((K, N), BF16),
        compiler_params=_cparams(("parallel",)),
        name="weight_cast",
    )(w)


def _mod_kernel(c_ref, w_ref, b_ref, o_ref):
    c = c_ref[...]
    sc = c * jax.nn.sigmoid(c)
    o_ref[0] = jnp.dot(sc, w_ref[...], preferred_element_type=F32,
                       precision=lax.Precision.HIGHEST) + b_ref[0]


def _modulation(c, w_mod, b_mod):
    B, D = c.shape
    rows = 8
    cp = jnp.pad(c, ((0, rows - B), (0, 0)))
    w_mod, layer = w_mod
    out = pl.pallas_call(
        _mod_kernel,
        grid=(3,),
        in_specs=[
            pl.BlockSpec((rows, D), lambda j: (0, 0)),
            pl.BlockSpec((None, D, D), lambda j: (layer, 0, j)),
            pl.BlockSpec((1, 1, D), lambda j: (j, 0, 0)),
        ],
        out_specs=pl.BlockSpec((1, rows, D), lambda j: (j, 0, 0)),
        out_shape=jax.ShapeDtypeStruct((3, rows, D), F32),
        compiler_params=_cparams(("arbitrary",)),
        name="modulation",
    )(cp, w_mod, b_mod.reshape(3, 1, D))
    return out[:, :B].reshape(3, B, 1, D)


def _project_out(y_refs, w_ref):
    acc = None
    off = 0
    for y_ref in y_refs:
        wd = y_ref.shape[-1]
        part = jnp.dot(y_ref[0], w_ref[off:off + wd, :], preferred_element_type=F32)
        acc = part if acc is None else acc + part
        off += wd
    return acc


def _modulated_rows(x, mod_ref, ng_ref):
    ms = jnp.mean(x * x, axis=-1, keepdims=True)
    y = x * lax.rsqrt(ms + EPS) * ng_ref[...]
    return y * (1.0 + mod_ref[1, 0]) + mod_ref[0, 0]


def _store_transposed_values(wvt_ref, h_scr, vt_ref, tm):
    vt = lax.dot_general(wvt_ref[...], h_scr[...], (((1,), (1,)), ((), ())),
                         preferred_element_type=F32)
    for g in range(vt.shape[0] // LANES):
        for cb in range(tm // KV_BLOCK):
            vt_ref[0, g, cb] = vt[g * LANES:(g + 1) * LANES,
                                  cb * KV_BLOCK:(cb + 1) * KV_BLOCK].astype(BF16)


def _inproj_even_kernel(x_ref, mod_ref, ng_ref, w_ref, wvt_ref, pos_ref, inv_ref, gain_ref, p_ref,
                        o_ref, vt_ref, h_scr, *, tm):
    h_scr[...] = _modulated_rows(x_ref[0], mod_ref, ng_ref).astype(BF16)
    _store_transposed_values(wvt_ref, h_scr, vt_ref, tm)
    ang_t = pos_ref[0].astype(F32) * inv_ref[...]
    reps = LANES // ang_t.shape[0]
    cos = jnp.concatenate([jnp.cos(ang_t)] * reps, axis=0).T
    sin = jnp.concatenate([jnp.sin(ang_t)] * reps, axis=0).T
    lane = lax.broadcasted_iota(jnp.int32, (tm, LANES), 1)
    first = (lane & (HEAD_DIM // 2)) == 0
    sin_s = jnp.where(first, -sin, sin)
    for wcol, start, width, kind in EVEN_CHUNKS:
        u = jnp.dot(h_scr[...], w_ref[:, wcol:wcol + width], preferred_element_type=F32)
        if kind == "qk":
            ms = jnp.dot((u * u).astype(BF16), p_ref[:width, :width], preferred_element_type=F32)
            un = u * lax.rsqrt(ms + EPS) * gain_ref[:, start:start + width]
            for s in range(width // LANES):
                xs = un[:, s * LANES:(s + 1) * LANES]
                rot = jnp.where(first, pltpu.roll(xs, LANES - HEAD_DIM // 2, 1),
                                pltpu.roll(xs, HEAD_DIM // 2, 1))
                c0 = start + s * LANES
                o_ref[0, :, c0:c0 + LANES] = (xs * cos + rot * sin_s).astype(BF16)
        elif kind == "silu":
            o_ref[0, :, start:start + width] = (u * jax.nn.sigmoid(u)).astype(BF16)
        else:
            o_ref[0, :, start:start + width] = u.astype(BF16)


def _inproj_odd_kernel(*refs, tm, n_y):
    if n_y:
        y_refs, (wout_ref, modp_ref) = refs[:n_y], refs[n_y:n_y + 2]
        x_ref, mod_ref, ng_ref, w_ref, wvt_ref, x1_ref, o_ref, vt_ref, h_scr = refs[n_y + 2:]
        x = x_ref[0] + modp_ref[2, 0] * _project_out(y_refs, wout_ref)
        x1_ref[0] = x
    else:
        x_ref, mod_ref, ng_ref, w_ref, wvt_ref, o_ref, vt_ref, h_scr = refs
        x = x_ref[0]
    h_scr[...] = _modulated_rows(x, mod_ref, ng_ref).astype(BF16)
    _store_transposed_values(wvt_ref, h_scr, vt_ref, tm)
    for wcol, start, width, kind in ODD_CHUNKS:
        u = jnp.dot(h_scr[...], w_ref[:, wcol:wcol + width], preferred_element_type=F32)
        if kind == "scale":
            u = u * (QK_SCALE * LOG2E)
        elif kind == "silu":
            u = u * jax.nn.sigmoid(u)
        o_ref[0, :, start:start + width] = u.astype(BF16)


def _inproj_even(x, mod, norm_g, w_in, positions, gain_cols, tm):
    B, S, D = x.shape
    w_stack, layer = w_in
    w_rows = _to_bf16(w_stack, layer)
    w_vat = _to_bf16(jnp.concatenate([w_stack[layer, :, W_VA:W_GA],
                                      w_stack[layer, :, W_VB:W_GB]], axis=1).T)
    n_groups = (A_WIDTH + B_KV) // LANES
    half = HEAD_DIM // 2
    inv = (ROPE_THETA ** (-jnp.arange(half, dtype=F32) / half)).reshape(half, 1)
    blk = jnp.arange(256) // HEAD_DIM
    avg = jnp.where(blk[:, None] == blk[None, :], 1.0 / HEAD_DIM, 0.0).astype(BF16)
    return pl.pallas_call(
        functools.partial(_inproj_even_kernel, tm=tm),
        grid=(B, S // tm),
        in_specs=[
            pl.BlockSpec((1, tm, D), lambda b, i: (b, i, 0)),
            pl.BlockSpec((3, 1, 1, D), lambda b, i: (0, b, 0, 0)),
            pl.BlockSpec((1, D), lambda b, i: (0, 0)),
            pl.BlockSpec((D, W_END), lambda b, i: (0, 0)),
            pl.BlockSpec((A_WIDTH + B_KV, D), lambda b, i: (0, 0)),
            pl.BlockSpec((1, 1, tm), lambda b, i: (b, 0, i)),
            pl.BlockSpec((half, 1), lambda b, i: (0, 0)),
            pl.BlockSpec((1, EVEN_IN), lambda b, i: (0, 0)),
            pl.BlockSpec((256, 256), lambda b, i: (0, 0)),
        ],
        out_specs=[
            pl.BlockSpec((1, tm, EVEN_IN), lambda b, i: (b, i, 0)),
            pl.BlockSpec((1, n_groups, tm // KV_BLOCK, LANES, KV_BLOCK), lambda b, i: (b, 0, i, 0, 0)),
        ],
        out_shape=[
            jax.ShapeDtypeStruct((B, S, EVEN_IN), BF16),
            jax.ShapeDtypeStruct((B, n_groups, S // KV_BLOCK, LANES, KV_BLOCK), BF16),
        ],
        scratch_shapes=[pltpu.VMEM((tm, D), BF16)],
        compiler_params=_cparams(("parallel", "parallel")),
        name="inproj_even",
    )(x, mod, norm_g.reshape(1, D), w_rows, w_vat, positions.reshape(B, 1, S),
      inv, gain_cols, avg)


def _inproj_odd(x, mod, norm_g, w_in, tm, pending=None):
    B, S, D = x.shape
    w_stack, layer = w_in
    w_rows = _to_bf16(w_stack, layer)
    w_vt = _to_bf16(w_stack[layer, :, 2 * C_WIDTH:3 * C_WIDTH].T)
    n_groups = C_WIDTH // LANES
    row_tile = lambda width: pl.BlockSpec((1, tm, width), lambda b, i: (b, i, 0))
    mod_spec = pl.BlockSpec((3, 1, 1, D), lambda b, i: (0, b, 0, 0))
    const = lambda shape: pl.BlockSpec(shape, lambda b, i: (0,) * len(shape),
                                       pipeline_mode=pl.Buffered(1))
    in_specs = [row_tile(D), mod_spec, const((1, D)), const((D, 4 * C_WIDTH)), const((C_WIDTH, D))]
    args = [x, mod, norm_g.reshape(1, D), w_rows, w_vt]
    out_specs = [
        row_tile(ODD_IN),
        pl.BlockSpec((1, n_groups, tm // KV_BLOCK, LANES, KV_BLOCK), lambda b, i: (b, 0, i, 0, 0)),
    ]
    out_shape = [
        jax.ShapeDtypeStruct((B, S, ODD_IN), BF16),
        jax.ShapeDtypeStruct((B, n_groups, S // KV_BLOCK, LANES, KV_BLOCK), BF16),
    ]
    n_y = 0
    if pending is not None:
        ys, w_out, mod_prev = pending
        n_y = len(ys)
        in_specs = [row_tile(y.shape[-1]) for y in ys] + [const(w_out.shape), mod_spec] + in_specs
        args = list(ys) + [w_out.astype(BF16), mod_prev] + args
        out_specs = [row_tile(D)] + out_specs
        out_shape = [jax.ShapeDtypeStruct((B, S, D), F32)] + out_shape
    outs = pl.pallas_call(
        functools.partial(_inproj_odd_kernel, tm=tm, n_y=n_y),
        grid=(B, S // tm),
        in_specs=in_specs,
        out_specs=out_specs,
        out_shape=out_shape,
        scratch_shapes=[pltpu.VMEM((tm, D), BF16)],
        compiler_params=_cparams(("parallel", "parallel")),
        name="inproj_odd",
    )(*args)
    return (outs[0], outs[1], outs[2]) if n_y else (x, outs[0], outs[1])


def _outproj_kernel(*refs, n_in):
    y_refs = refs[:n_in]
    w_ref, x_ref, mod_ref, o_ref = refs[n_in:]
    o_ref[0] = x_ref[0] + mod_ref[2, 0] * _project_out(y_refs, w_ref)


def _outproj(ys, w_out, x, mod, tm):
    B, S, D = x.shape
    K = w_out.shape[0]
    n_in = len(ys)
    in_specs = [pl.BlockSpec((1, tm, y.shape[-1]), lambda b, i: (b, i, 0)) for y in ys]
    in_specs += [
        pl.BlockSpec((K, D), lambda b, i: (0, 0)),
        pl.BlockSpec((1, tm, D), lambda b, i: (b, i, 0)),
        pl.BlockSpec((3, 1, 1, D), lambda b, i: (0, b, 0, 0)),
    ]
    return pl.pallas_call(
        functools.partial(_outproj_kernel, n_in=n_in),
        grid=(B, S // tm),
        in_specs=in_specs,
        out_specs=pl.BlockSpec((1, tm, D), lambda b, i: (b, i, 0)),
        out_shape=jax.ShapeDtypeStruct((B, S, D), F32),
        compiler_params=_cparams(("parallel", "parallel")),
        name="outproj",
    )(*ys, w_out.astype(BF16), x, mod)


def _diff_attn_kernel(bound_ref, lam_ref, sub_ref, q_ref, k_ref, vt_ref, g_ref, o_ref,
                      qs_scr, s0_scr, s1_scr, smax0_scr, smax1_scr, m_scr, l_scr, acc_scr, *,
                      tq, tk, lambda_init):
    i = pl.program_id(2)
    q = q_ref[0]
    lane = lax.broadcasted_iota(jnp.int32, (tq, LANES), 1)
    zero = jnp.zeros_like(q)
    qs_scr[:tq] = jnp.where(lane < HEAD_DIM, q, zero)
    qs_scr[tq:] = jnp.where(lane >= HEAD_DIM, q, zero)

    slots = ((s0_scr, smax0_scr), (s1_scr, smax1_scr))

    def spans(diag):
        if diag is None:
            return ((0, 2 * tq),)
        return tuple((c * tq + diag * tk, (c + 1) * tq) for c in range(2))

    def scores(j, slot, bounded, diag=None):
        s_ref, smax_ref = slots[slot]
        k = k_ref[0, pl.ds(pl.multiple_of(j * tk, tk), tk), :]
        for a, b in spans(diag):
            s = lax.dot_general(k, qs_scr[a:b], (((1,), (1,)), ((), ())),
                                preferred_element_type=F32)
            s_ref[:, a:b] = s
            if not bounded and diag is None:
                smax_ref[:, a:b] = jnp.max(s, axis=0, keepdims=True)

    def softmax_pv(j, slot, bounded, diag=None):
        s_ref, smax_ref = slots[slot]
        for a, b in spans(diag):
            s = s_ref[:, a:b]
            if diag is not None:
                key = lax.broadcasted_iota(jnp.int32, (tk, b - a), 0) + diag * tk
                qry = (lax.broadcasted_iota(jnp.int32, (tk, b - a), 1) + a) & (tq - 1)
                s = jnp.where(key <= qry, s, NEG_BIG)
            if bounded:
                p = jnp.exp2(s)
                l_scr[:, a:b] += jnp.sum(p, axis=0, keepdims=True)
                acc_scr[:, a:b] += jnp.dot(vt_ref[0, 0, j], p.astype(BF16),
                                           preferred_element_type=F32)
                continue
            smax = smax_ref[:, a:b] if diag is None else jnp.max(s, axis=0, keepdims=True)
            m = m_scr[:, a:b]
            m_new = jnp.maximum(m, smax)
            alpha = jnp.exp2(m - m_new)
            p = jnp.exp2(s - m_new)
            l_scr[:, a:b] = alpha * l_scr[:, a:b] + jnp.sum(p, axis=0, keepdims=True)
            m_scr[:, a:b] = m_new
            pv = jnp.dot(vt_ref[0, 0, j], p.astype(BF16), preferred_element_type=F32)
            acc_scr[:, a:b] = alpha * acc_scr[:, a:b] + pv

    def sweep(bounded):
        m_scr[...] = jnp.full((1, 2 * tq), NEG_BIG, F32)
        l_scr[...] = jnp.zeros((1, 2 * tq), F32)
        acc_scr[...] = jnp.zeros((LANES, 2 * tq), F32)
        n_diag = tq // tk
        n_full = n_diag * i
        scores(0, 0, bounded)

        def body(jj, carry):
            j0 = n_diag * jj
            for d in range(n_diag):
                scores(j0 + d + 1, (d + 1) % 2, bounded)
                softmax_pv(j0 + d, d % 2, bounded)
            return carry

        lax.fori_loop(0, i, body, 0)
        for d in range(n_diag):
            if d + 1 < n_diag:
                scores(n_full + d + 1, (d + 1) % 2, bounded, diag=d + 1)
            softmax_pv(n_full + d, d % 2, bounded, diag=d)

    is_bounded = bound_ref[0] <= LOGIT_BOUND
    pl.when(is_bounded)(lambda: sweep(True))
    pl.when(jnp.logical_not(is_bounded))(lambda: sweep(False))
    l = l_scr[...]
    acc = acc_scr[...]

    lam_rows = lam_ref[...]
    lam = (jnp.exp(jnp.sum(lam_rows[0:1] * lam_rows[1:2], axis=1, keepdims=True))
           - jnp.exp(jnp.sum(lam_rows[2:3] * lam_rows[3:4], axis=1, keepdims=True))
           + lambda_init)
    ot = acc[:, :tq] / l[:, :tq] - lam * (acc[:, tq:] / l[:, tq:])
    ms = jnp.mean(ot * ot, axis=0, keepdims=True)
    yt = ot * lax.rsqrt(ms + EPS) * (sub_ref[...] * (1.0 - lambda_init))
    o_ref[0] = (yt.T * g_ref[0].astype(F32)).astype(BF16)


def _diff_attention(u, vat, logit_bound, lam_rows, subln_g, lambda_init, tq):
    B, S, _ = u.shape
    tk = KV_BLOCK
    assert tq % (2 * tk) == 0 and S % tq == 0
    qb, kb, gb = OFF_QA // LANES, OFF_KA // LANES, OFF_GA // LANES
    stat = pltpu.VMEM((1, 2 * tq), F32)
    score = pltpu.VMEM((tk, 2 * tq), F32)
    return pl.pallas_call(
        functools.partial(_diff_attn_kernel, tq=tq, tk=tk, lambda_init=lambda_init),
        grid=(B, A_HEADS, S // tq),
        in_specs=[
            pl.BlockSpec(memory_space=pltpu.SMEM),
            pl.BlockSpec((4, HEAD_DIM), lambda b, h, i: (0, 0)),
            pl.BlockSpec((LANES, 1), lambda b, h, i: (0, 0)),
            pl.BlockSpec((1, tq, LANES), lambda b, h, i: (b, i, qb + h)),
            pl.BlockSpec((1, S, LANES), lambda b, h, i: (b, 0, kb + h)),
            pl.BlockSpec((1, 1, S // tk, LANES, tk), lambda b, h, i: (b, h, 0, 0, 0)),
            pl.BlockSpec((1, tq, LANES), lambda b, h, i: (b, i, gb + h)),
        ],
        out_specs=pl.BlockSpec((1, tq, LANES), lambda b, h, i: (b, i, h)),
        out_shape=jax.ShapeDtypeStruct((B, S, A_WIDTH), BF16),
        scratch_shapes=[pltpu.VMEM((2 * tq, LANES), BF16), score, score, stat, stat, stat, stat,
                        pltpu.VMEM((LANES, 2 * tq), F32)],
        compiler_params=_cparams(("parallel", "parallel", "arbitrary")),
        name="diff_attention",
    )(logit_bound.reshape(1).astype(F32), lam_rows, subln_g.reshape(LANES, 1), u, u, vat, u)


def _swa_kernel(sink_ref, q_ref, kc_ref, kp_ref, vtc_ref, vtp_ref, g_ref, o_ref, qs_scr, *, t):
    i = pl.program_id(1)
    group = B_HEADS // B_KV_HEADS
    width = B_HEADS * t
    lane = lax.broadcasted_iota(jnp.int32, (t, LANES), 1)
    for h in range(B_HEADS):
        kv = h // group
        qc = q_ref[0, :, (h // 2) * LANES:(h // 2 + 1) * LANES].astype(F32)
        if h % 2 != kv:
            qc = pltpu.roll(qc, HEAD_DIM, 1)
        keep = (lane >= kv * HEAD_DIM) & (lane < (kv + 1) * HEAD_DIM)
        qs_scr[h * t:(h + 1) * t] = jnp.where(keep, qc, 0.0).astype(BF16)
    kk = jnp.concatenate([kp_ref[0], kc_ref[0]], axis=0)
    vvt = jnp.concatenate([vtp_ref[0, 0, 0][:, KV_BLOCK - WINDOW:], vtc_ref[0, 0, 0]], axis=1)
    s = lax.dot_general(kk, qs_scr[...], (((1,), (1,)), ((), ())),
                        preferred_element_type=F32)
    key = lax.broadcasted_iota(jnp.int32, (WINDOW + t, width), 0)
    seg = lax.broadcasted_iota(jnp.int32, (WINDOW + t, width), 1)
    qry = seg & (t - 1)
    mask = (key > qry) & (key <= qry + WINDOW) & jnp.logical_or(i > 0, key >= WINDOW)
    s = jnp.where(mask, s, NEG_BIG)
    head = lax.broadcasted_iota(jnp.int32, (1, width), 1) // t
    sink = jnp.zeros((1, width), F32)
    for h in range(B_HEADS):
        sink = jnp.where(head == h, sink_ref[h] * LOG2E, sink)
    m = jnp.maximum(jnp.max(s, axis=0, keepdims=True), sink)
    e = jnp.exp2(s - m)
    den = jnp.sum(e, axis=0, keepdims=True) + jnp.exp2(sink - m)
    pv = jnp.dot(vvt, e.astype(BF16), preferred_element_type=F32) / den
    for c in range(B_HEADS // 2):
        rows = []
        for h in (2 * c, 2 * c + 1):
            kv = h // group
            rows.append(pv[kv * HEAD_DIM:(kv + 1) * HEAD_DIM, h * t:(h + 1) * t])
        ot = jnp.concatenate(rows, axis=0)
        gate = g_ref[0, :, c * LANES:(c + 1) * LANES].astype(F32)
        o_ref[0, :, c * LANES:(c + 1) * LANES] = (ot.T * gate).astype(BF16)


def _swa_attention(u, vt, sinks):
    B, S, _ = u.shape
    t = KV_BLOCK
    per = t // WINDOW
    qb, gb, kb = OFF_QB // B_WIDTH, OFF_GB // B_WIDTH, OFF_KB // LANES
    assert OFF_QB % B_WIDTH == 0 and OFF_GB % B_WIDTH == 0
    vgroup = A_WIDTH // LANES
    return pl.pallas_call(
        functools.partial(_swa_kernel, t=t),
        grid=(B, S // t),
        in_specs=[
            pl.BlockSpec(memory_space=pltpu.SMEM),
            pl.BlockSpec((1, t, B_WIDTH), lambda b, i: (b, i, qb)),
            pl.BlockSpec((1, t, LANES), lambda b, i: (b, i, kb)),
            pl.BlockSpec((1, WINDOW, LANES), lambda b, i: (b, jnp.maximum(i * per - 1, 0), kb)),
            pl.BlockSpec((1, 1, 1, LANES, t), lambda b, i: (b, vgroup, i, 0, 0)),
            pl.BlockSpec((1, 1, 1, LANES, t), lambda b, i: (b, vgroup, jnp.maximum(i - 1, 0), 0, 0)),
            pl.BlockSpec((1, t, B_WIDTH), lambda b, i: (b, i, gb)),
        ],
        out_specs=pl.BlockSpec((1, t, B_WIDTH), lambda b, i: (b, i, 0)),
        out_shape=jax.ShapeDtypeStruct((B, S, B_WIDTH), BF16),
        scratch_shapes=[pltpu.VMEM((B_HEADS * t, LANES), BF16)],
        compiler_params=_cparams(("parallel", "parallel")),
        name="swa_attention",
    )(sinks.astype(F32), u, u, u, vt, vt, u)


def _stick_kernel(tri_ref, q_ref, k_ref, vt_ref, g_ref, o_ref, qs_scr, run_scr, acc_scr, *, t, pairs):
    i = pl.program_id(2)
    heads = 2 * pairs
    width = heads * t
    lane = lax.broadcasted_iota(jnp.int32, (t, LANES), 1)
    for p in range(pairs):
        q = q_ref[0, :, p * LANES:(p + 1) * LANES]
        zero = jnp.zeros_like(q)
        qs_scr[(2 * p) * t:(2 * p + 1) * t] = jnp.where(lane < HEAD_DIM, q, zero)
        qs_scr[(2 * p + 1) * t:(2 * p + 2) * t] = jnp.where(lane >= HEAD_DIM, q, zero)
    tri = tri_ref[...]
    acc_scr[...] = jnp.zeros((LANES, width), F32)

    def prepare(j, diagonal):
        off = pl.multiple_of(j * t, t)
        z = jnp.concatenate([
            lax.dot_general(k_ref[0, pl.ds(off, t), p * LANES:(p + 1) * LANES],
                            qs_scr[2 * p * t:(2 * p + 2) * t],
                            (((1,), (1,)), ((), ())), preferred_element_type=F32)
            for p in range(pairs)], axis=1)
        if diagonal:
            key = lax.broadcasted_iota(jnp.int32, (t, width), 0)
            qry = lax.broadcasted_iota(jnp.int32, (t, width), 1) & (t - 1)
            z = jnp.where(key < qry, z, NEG_BIG)
        sp = jnp.maximum(z, 0.0) + jnp.log(1.0 + jnp.exp2(-jnp.abs(z))) * LOG2E
        csum = jnp.dot(tri, sp.astype(BF16), preferred_element_type=F32)
        return z, csum

    def apply(j, z, csum, run):
        w = jnp.exp2(z - csum - run).astype(BF16)
        for p in range(pairs):
            cols = slice(2 * p * t, (2 * p + 2) * t)
            acc_scr[:, cols] += jnp.dot(vt_ref[0, p, j], w[:, cols], preferred_element_type=F32)
        return run + csum[0:1, :]

    prev = jnp.maximum(i - 1, 0)
    zd, cd = prepare(i, True)
    zp, cp = prepare(prev, False)
    run = apply(i, zd, cd, jnp.zeros((1, width), F32))
    run = apply(prev, zp, cp, run + jnp.where(i > 0, 0.0, -NEG_BIG))
    run_scr[...] = run

    def cond(c):
        jj, go = c
        return jnp.logical_and(jj < i - 1, go)

    def body(c):
        jj, _ = c
        j = i - 2 - jj
        z, csum = prepare(j, False)
        run = apply(j, z, csum, run_scr[...])
        run_scr[...] = run
        return jj + 1, jnp.min(run) < EXP2_UNDERFLOW

    lax.while_loop(cond, body, (jnp.int32(0), jnp.min(run) < EXP2_UNDERFLOW))
    first = lax.broadcasted_iota(jnp.int32, (LANES, t), 0) < HEAD_DIM
    for p in range(pairs):
        a = 2 * p * t
        ot = jnp.where(first, acc_scr[:, a:a + t], acc_scr[:, a + t:a + 2 * t])
        gate = g_ref[0, :, p * LANES:(p + 1) * LANES].astype(F32)
        o_ref[0, :, p * LANES:(p + 1) * LANES] = (ot.T * gate).astype(BF16)


def _stick_attention(u, vt, pairs):
    B, S, _ = u.shape
    t = KV_BLOCK
    width = pairs * LANES
    groups = C_WIDTH // width
    row = jnp.arange(t)
    tri = (row[None, :] >= row[:, None]).astype(BF16)
    return pl.pallas_call(
        functools.partial(_stick_kernel, t=t, pairs=pairs),
        grid=(B, groups, S // t),
        in_specs=[
            pl.BlockSpec((t, t), lambda b, h, i: (0, 0)),
            pl.BlockSpec((1, t, width), lambda b, h, i: (b, i, h)),
            pl.BlockSpec((1, S, width), lambda b, h, i: (b, 0, groups + h)),
            pl.BlockSpec((1, pairs, S // t, LANES, t), lambda b, h, i: (b, h, 0, 0, 0)),
            pl.BlockSpec((1, t, width), lambda b, h, i: (b, i, 2 * groups + h)),
        ],
        out_specs=pl.BlockSpec((1, t, width), lambda b, h, i: (b, i, h)),
        out_shape=jax.ShapeDtypeStruct((B, S, C_WIDTH), BF16),
        scratch_shapes=[pltpu.VMEM((2 * pairs * t, LANES), BF16),
                        pltpu.VMEM((1, 2 * pairs * t), F32),
                        pltpu.VMEM((LANES, 2 * pairs * t), F32)],
        compiler_params=_cparams(("parallel", "parallel", "arbitrary")),
        name="stick_attention",
    )(tri, u, u, vt, u)


def _tile(n, pref):
    t = min(pref, n)
    assert n % t == 0
    return t


def _even_layer(x, c, positions, layer, norm_g, w_mod, b_mod, w_in, a_q_gain, a_k_gain,
                lq1, lk1, lq2, lk2, a_subln_g, b_q_gain, b_k_gain, b_sinks, w_out):
    B, S, D = x.shape
    mod = _modulation(c, w_mod, b_mod)
    ones = lambda n: jnp.ones((n,), F32)
    gain_cols = jnp.concatenate([
        jnp.tile(a_q_gain * (QK_SCALE * LOG2E), A_QK // HEAD_DIM),
        jnp.tile(a_k_gain, A_QK // HEAD_DIM),
        ones(A_WIDTH),
        jnp.tile(b_q_gain * (QK_SCALE * LOG2E), B_HEADS),
        ones(B_WIDTH),
        jnp.tile(b_k_gain, B_KV_HEADS),
    ]).astype(F32).reshape(1, EVEN_IN)
    u, vat = _inproj_even(x, mod, norm_g, w_in, positions, gain_cols, _tile(S, 512))
    lambda_init = 0.8 - 0.6 * math.exp(-0.3 * layer)
    lam_rows = jnp.stack([lq1, lk1, lq2, lk2]).astype(F32)
    logit_bound = (HEAD_DIM * QK_SCALE * LOG2E * LOGIT_BOUND_MARGIN
                   * jnp.max(jnp.abs(a_q_gain)) * jnp.max(jnp.abs(a_k_gain)))
    ya = _diff_attention(u, vat, logit_bound, lam_rows, a_subln_g.astype(F32), lambda_init,
                         _tile(S, 2048))
    yb = _swa_attention(u, vat, b_sinks)
    return x, ([ya, yb], w_out, mod)


def _odd_layer(x, pending, c, norm_g, w_mod, b_mod, w_in, w_out):
    B, S, D = x.shape
    mod = _modulation(c, w_mod, b_mod)
    x, u, vt = _inproj_odd(x, mod, norm_g, w_in, _tile(S, 512), pending)
    y = _stick_attention(u, vt, pairs=4)
    return x, ([y], w_out, mod)


def kernel(x, c, positions, even_norm_g, even_w_mod, even_b_mod, even_w_in, a_q_gain, a_k_gain,
           a_lambda_q1, a_lambda_k1, a_lambda_q2, a_lambda_k2, a_subln_g, b_q_gain, b_k_gain,
           b_sinks, even_w_out, odd_norm_g, odd_w_mod, odd_b_mod, odd_w_in, odd_w_out):
    depth = even_norm_g.shape[0] + odd_norm_g.shape[0]
    tm = _tile(x.shape[1], 1024)
    pending = None
    for layer in range(depth):
        j = layer // 2
        if layer % 2 == 0:
            if pending is not None:
                x = _outproj(pending[0], pending[1], x, pending[2], tm)
            x, pending = _even_layer(
                x, c, positions, layer, even_norm_g[j], (even_w_mod, j), even_b_mod[j],
                (even_w_in, j), a_q_gain[j], a_k_gain[j], a_lambda_q1[j], a_lambda_k1[j],
                a_lambda_q2[j], a_lambda_k2[j], a_subln_g[j], b_q_gain[j], b_k_gain[j],
                b_sinks[j], even_w_out[j])
        else:
            x, pending = _odd_layer(x, pending, c, odd_norm_g[j], (odd_w_mod, j), odd_b_mod[j],
                                    (odd_w_in, j), odd_w_out[j])
    return _outproj(pending[0], pending[1], x, pending[2], tm)
```
